```python
import jax, jax.numpy as jnp
from jax import lax
import numpy as np

D_MODEL = 2048
BATCH = 4
SEQ = 2048
DEPTH = 1
DEC_BATCH = 128
DEC_SEQ = 1
PAST_LEN = 16384
PAGE_SIZE = 128

BR_W = D_MODEL // 2
N_BRANCH = 3
GMLP_GROUPS = 4
CHUNK = 128
LRU_BLOCKS = 8
LRU_BLOCK = BR_W // LRU_BLOCKS
CONV_W = 4
LRU_C = 8.0
N_MEM = 256
XA_HEADS = 4
XA_HEAD_DIM = BR_W // XA_HEADS
D_FF = 5632
IN_COLS = 5 * BR_W + N_BRANCH * D_MODEL
LN_EPS = 1e-5
ALPHA = (2.0 * DEPTH) ** 0.25
BETA = (8.0 * DEPTH) ** -0.25

kernel_name = 'hybrid_gmlp_rglru_memxattn_decoder_step'


def layer_norm(x, g, b):
    xf = x.astype(jnp.float32)
    mu = jnp.mean(xf, axis=-1, keepdims=True)
    var = jnp.mean(jnp.square(xf - mu), axis=-1, keepdims=True)
    return ((xf - mu) * lax.rsqrt(var + LN_EPS) * g.astype(jnp.float32) + b.astype(jnp.float32)).astype(x.dtype)


def swiglu_ffn(x, w_gu, w_down):
    g, u = jnp.split(x @ w_gu, 2, axis=-1)
    return (jax.nn.silu(g) * u) @ w_down


def chunk_spatial_gate(u, v, w_s, b_s):
    bsz, t, c = v.shape
    cl = min(t, CHUNK)
    pad = (-t) % cl
    vp = jnp.pad(v, ((0, 0), (0, pad), (0, 0)))
    nc = (t + pad) // cl
    vc = vp.reshape(bsz, nc, cl, GMLP_GROUPS, c // GMLP_GROUPS)
    mask = jnp.tril(jnp.ones((cl, cl), dtype=bool))
    w = jnp.where(mask, w_s[:, :cl, :cl], jnp.zeros((), w_s.dtype))
    s = jnp.einsum('gts,bnsgc->bntgc', w, vc) + b_s[:, :cl].T[None, None, :, :, None]
    s = s.reshape(bsz, nc * cl, c)[:, :t]
    return u * s


def causal_conv(x, buf, w, b):
    t = x.shape[1]
    xp = jnp.concatenate([buf.astype(x.dtype), x], axis=1)
    out = b
    for k in range(CONV_W):
        out = out + xp[:, k:k + t] * w[k]
    return out, xp[:, -(CONV_W - 1):]


def rg_lru(x, h0, w_a, b_a, w_x, b_x, lam):
    bsz, t, c = x.shape
    xb = x.reshape(bsz, t, LRU_BLOCKS, LRU_BLOCK)
    r = jax.nn.sigmoid((jnp.einsum('btki,kij->btkj', xb, w_a).reshape(bsz, t, c) + b_a).astype(jnp.float32))
    i = jax.nn.sigmoid((jnp.einsum('btki,kij->btkj', xb, w_x).reshape(bsz, t, c) + b_x).astype(jnp.float32))
    log_a = -LRU_C * r * jax.nn.softplus(-lam.astype(jnp.float32))
    a = jnp.exp(log_a)
    bterm = jnp.sqrt(-jnp.expm1(2.0 * log_a)) * (i * x.astype(jnp.float32))
    bterm = bterm.at[:, 0].add(a[:, 0] * h0.astype(jnp.float32))

    def combine(p, q):
        a1, b1 = p
        a2, b2 = q
        return a1 * a2, a2 * b1 + b2

    _, h = lax.associative_scan(combine, (a, bterm), axis=1)
    return h.astype(x.dtype), h[:, -1].astype(x.dtype)


def cross_attention(q, k, v):
    s = jnp.einsum('bthd,bmhd->bhtm', q, k).astype(jnp.float32) * (XA_HEAD_DIM ** -0.5)
    p = jax.nn.softmax(s, axis=-1).astype(v.dtype)
    return jnp.einsum('bhtm,bmhd->bthd', p, v)


def memory_kv(mem, g, b, w_kv):
    bsz = mem.shape[0]
    k, v = jnp.split(layer_norm(mem, g, b) @ w_kv, 2, axis=-1)
    return (k.reshape(bsz, N_MEM, XA_HEADS, XA_HEAD_DIM), v.reshape(bsz, N_MEM, XA_HEADS, XA_HEAD_DIM))


def token_mixer(x, mem_k, mem_v, conv_buf, h0, p):
    bsz, t, _ = x.shape
    z = x @ p['w_in']
    u, v, rx, rg, q, gl = jnp.split(z, [BR_W, 2 * BR_W, 3 * BR_W, 4 * BR_W, 5 * BR_W], axis=-1)
    u = jax.nn.gelu(u)
    v = layer_norm(jax.nn.gelu(v), p['gmlp_ln_g'], p['gmlp_ln_b'])
    y_gmlp = chunk_spatial_gate(u, v, p['gmlp_w_s'], p['gmlp_b_s'])
    xc, new_buf = causal_conv(rx, conv_buf, p['conv_w'], p['conv_b'])
    h, h_last = rg_lru(xc, h0, p['lru_w_a'], p['lru_b_a'], p['lru_w_x'], p['lru_b_x'], p['lru_lambda'])
    y_lru = jax.nn.gelu(rg) * h
    y_xa = cross_attention(q.reshape(bsz, t, XA_HEADS, XA_HEAD_DIM), mem_k, mem_v).reshape(bsz, t, BR_W)
    ys = jnp.stack([y_gmlp, y_lru, y_xa], axis=2)
    proj = jnp.einsum('btkc,kcd->btkd', ys, p['w_branch'])
    gates = jax.nn.sigmoid(gl.reshape(bsz, t, N_BRANCH, D_MODEL) + p['gate_b'])
    merged = jnp.einsum('btkd,btkd->btd', gates, proj)
    return merged @ p['w_out'], v, new_buf, h_last


def decoder_layer(x, mem_k, mem_v, conv_buf, h0, p):
    x = layer_norm(ALPHA * x + 0.5 * swiglu_ffn(x, p['ffn1_w_gu'], p['ffn1_w_down']), p['ln1_g'], p['ln1_b'])
    m, v_rows, new_buf, h_last = token_mixer(x, mem_k, mem_v, conv_buf, h0, p)
    x = layer_norm(ALPHA * x + m, p['ln2_g'], p['ln2_b'])
    x = layer_norm(ALPHA * x + 0.5 * swiglu_ffn(x, p['ffn2_w_gu'], p['ffn2_w_down']), p['ln3_g'], p['ln3_b'])
    return x, v_rows, new_buf, h_last


def _nrm(key, shape, scale):
    return scale * jax.random.normal(key, shape, jnp.float32)


def _gain(key, shape):
    return 1.0 + _nrm(key, shape, 0.02)


def setup_inputs(seed: int = 0) -> dict:
    key = jax.random.key(seed)
    ks = list(jax.random.split(key, 48))
    L, D = DEPTH, D_MODEL
    x_prompt = _nrm(ks[0], (BATCH, SEQ, D), 1.0)
    x_sample = _nrm(ks[1], (DEC_BATCH, DEC_SEQ, D), 1.0)
    mem_prompt = _nrm(ks[2], (BATCH, N_MEM, D), 1.0)
    cache_mem_k = _nrm(ks[3], (L, DEC_BATCH, N_MEM, XA_HEADS, XA_HEAD_DIM), 1.0)
    cache_mem_v = _nrm(ks[4], (L, DEC_BATCH, N_MEM, XA_HEADS, XA_HEAD_DIM), BETA)
    state_conv = _nrm(ks[5], (L, DEC_BATCH, CONV_W - 1, BR_W), 1.0)
    state_lru_h = _nrm(ks[6], (L, DEC_BATCH, BR_W), 0.5)
    ffn1_w_gu = _nrm(ks[7], (L, D, 2 * D_FF), BETA * D ** -0.5)
    ffn1_w_down = _nrm(ks[8], (L, D_FF, D), BETA * D_FF ** -0.5)
    ln1_g = _gain(ks[9], (L, D))
    ln1_b = _nrm(ks[10], (L, D), 0.02)
    w_in = _nrm(ks[11], (L, D, IN_COLS), D ** -0.5)
    gate_b = _nrm(ks[12], (L, N_BRANCH, D), 0.1)
    gmlp_ln_g = _gain(ks[13], (L, BR_W))
    gmlp_ln_b = _nrm(ks[14], (L, BR_W), 0.02)
    gmlp_w_s = _nrm(ks[15], (L, GMLP_GROUPS, CHUNK, CHUNK), CHUNK ** -0.5)
    gmlp_b_s = 1.0 + _nrm(ks[16], (L, GMLP_GROUPS, CHUNK), 0.1)
    conv_w = _nrm(ks[17], (L, CONV_W, BR_W), CONV_W ** -0.5)
    conv_b = _nrm(ks[18], (L, BR_W), 0.02)
    lru_w_a = _nrm(ks[19], (L, LRU_BLOCKS, LRU_BLOCK, LRU_BLOCK), LRU_BLOCK ** -0.5)
    lru_b_a = _nrm(ks[20], (L, BR_W), 0.02)
    lru_w_x = _nrm(ks[21], (L, LRU_BLOCKS, LRU_BLOCK, LRU_BLOCK), LRU_BLOCK ** -0.5)
    lru_b_x = _nrm(ks[22], (L, BR_W), 0.02)
    a_c = jax.random.uniform(ks[23], (L, BR_W), jnp.float32, 0.9, 0.999)
    a_base = a_c ** (1.0 / LRU_C)
    lru_lambda = jnp.log(a_base) - jnp.log1p(-a_base)
    mem_ln_g = _gain(ks[24], (L, D))
    mem_ln_b = _nrm(ks[25], (L, D), 0.02)
    w_mem_kv = jnp.concatenate([_nrm(ks[26], (L, D, BR_W), D ** -0.5),
                                _nrm(ks[27], (L, D, BR_W), BETA * D ** -0.5)], axis=-1)
    w_branch = _nrm(ks[28], (L, N_BRANCH, BR_W, D), BETA * BR_W ** -0.5)
    w_out = _nrm(ks[29], (L, D, D), BETA * D ** -0.5)
    ln2_g = _gain(ks[30], (L, D))
    ln2_b = _nrm(ks[31], (L, D), 0.02)
    ffn2_w_gu = _nrm(ks[32], (L, D, 2 * D_FF), BETA * D ** -0.5)
    ffn2_w_down = _nrm(ks[33], (L, D_FF, D), BETA * D_FF ** -0.5)
    ln3_g = _gain(ks[34], (L, D))
    ln3_b = _nrm(ks[35], (L, D), 0.02)
    return {'x_prompt': x_prompt, 'x_sample': x_sample, 'mem_prompt': mem_prompt,
            'cache_mem_k': cache_mem_k, 'cache_mem_v': cache_mem_v,
            'state_conv': state_conv, 'state_lru_h': state_lru_h,
            'ffn1_w_gu': ffn1_w_gu, 'ffn1_w_down': ffn1_w_down, 'ln1_g': ln1_g, 'ln1_b': ln1_b,
            'w_in': w_in, 'gate_b': gate_b, 'gmlp_ln_g': gmlp_ln_g, 'gmlp_ln_b': gmlp_ln_b,
            'gmlp_w_s': gmlp_w_s, 'gmlp_b_s': gmlp_b_s, 'conv_w': conv_w, 'conv_b': conv_b,
            'lru_w_a': lru_w_a, 'lru_b_a': lru_b_a, 'lru_w_x': lru_w_x, 'lru_b_x': lru_b_x,
            'lru_lambda': lru_lambda, 'mem_ln_g': mem_ln_g, 'mem_ln_b': mem_ln_b, 'w_mem_kv': w_mem_kv,
            'w_branch': w_branch, 'w_out': w_out, 'ln2_g': ln2_g, 'ln2_b': ln2_b,
            'ffn2_w_gu': ffn2_w_gu, 'ffn2_w_down': ffn2_w_down, 'ln3_g': ln3_g, 'ln3_b': ln3_b}


def reference(x_prompt, x_sample, mem_prompt, cache_mem_k, cache_mem_v, state_conv, state_lru_h,
              ffn1_w_gu, ffn1_w_down, ln1_g, ln1_b, w_in, gate_b, gmlp_ln_g, gmlp_ln_b, gmlp_w_s, gmlp_b_s,
              conv_w, conv_b, lru_w_a, lru_b_a, lru_w_x, lru_b_x, lru_lambda, mem_ln_g, mem_ln_b, w_mem_kv,
              w_branch, w_out, ln2_g, ln2_b, ffn2_w_gu, ffn2_w_down, ln3_g, ln3_b):
    xp, xs = x_prompt, x_sample
    bsz_p = x_prompt.shape[0]
    mk_p, mv_p, cb_p, hh_p, cb_s, hh_s, vr_s = [], [], [], [], [], [], []
    for l in range(DEPTH):
        p = {'ffn1_w_gu': ffn1_w_gu[l], 'ffn1_w_down': ffn1_w_down[l], 'ln1_g': ln1_g[l], 'ln1_b': ln1_b[l],
             'w_in': w_in[l], 'gate_b': gate_b[l], 'gmlp_ln_g': gmlp_ln_g[l], 'gmlp_ln_b': gmlp_ln_b[l],
             'gmlp_w_s': gmlp_w_s[l], 'gmlp_b_s': gmlp_b_s[l], 'conv_w': conv_w[l], 'conv_b': conv_b[l],
             'lru_w_a': lru_w_a[l], 'lru_b_a': lru_b_a[l], 'lru_w_x': lru_w_x[l], 'lru_b_x': lru_b_x[l],
             'lru_lambda': lru_lambda[l], 'w_branch': w_branch[l], 'w_out': w_out[l],
             'ln2_g': ln2_g[l], 'ln2_b': ln2_b[l], 'ffn2_w_gu': ffn2_w_gu[l], 'ffn2_w_down': ffn2_w_down[l],
             'ln3_g': ln3_g[l], 'ln3_b': ln3_b[l]}
        k_p, v_p = memory_kv(mem_prompt, mem_ln_g[l], mem_ln_b[l], w_mem_kv[l])
        zbuf = jnp.zeros((bsz_p, CONV_W - 1, BR_W), xp.dtype)
        zh = jnp.zeros((bsz_p, BR_W), xp.dtype)
        xp, _, nb_p, nh_p = decoder_layer(xp, k_p, v_p, zbuf, zh, p)
        xs, vrows_s, nb_s, nh_s = decoder_layer(xs, cache_mem_k[l], cache_mem_v[l], state_conv[l], state_lru_h[l], p)
        mk_p.append(k_p)
        mv_p.append(v_p)
        cb_p.append(nb_p)
        hh_p.append(nh_p)
        cb_s.append(nb_s)
        hh_s.append(nh_s)
        vr_s.append(vrows_s)
    return (xp, xs, jnp.stack(mk_p), jnp.stack(mv_p), jnp.stack(cb_p), jnp.stack(hh_p),
            jnp.stack(cb_s), jnp.stack(hh_s), jnp.stack(vr_s))
```

```python
import functools

import jax
import jax.numpy as jnp
from jax import lax
from jax.experimental import pallas as pl
from jax.experimental.pallas import tpu as pltpu

F32 = jnp.float32
BF16 = jnp.bfloat16

D_MODEL = 2048
BR_W = 1024
D_FF = 5632
N_MEM = 256
XA_HEADS = 4
XA_HEAD_DIM = 256
GMLP_GROUPS = 4
GROUP_W = BR_W // GMLP_GROUPS
CHUNK = 128
LRU_BLOCKS = 8
LRU_BLOCK = 128
CONV_W = 4
LRU_C = 8.0
LN_EPS = 1e-5
ALPHA = 2.0 ** 0.25
IN_BLOCKS = 11
GATE_BLOCK0 = 5
SUBLANES = 8
MiB = 1024 * 1024


def _params(semantics, vmem_mib):
    return pltpu.CompilerParams(dimension_semantics=semantics, vmem_limit_bytes=vmem_mib * MiB)


def _layer_norm(x, g, b):
    mu = jnp.mean(x, axis=-1, keepdims=True)
    xc = x - mu
    var = jnp.mean(xc * xc, axis=-1, keepdims=True)
    return xc * lax.rsqrt(var + LN_EPS) * g + b


def _dot(a, b):
    return jnp.dot(a, b, preferred_element_type=F32)


def _ffn_kernel(x_ref, wg_ref, wu_ref, wd_ref, g_ref, b_ref, o_ref, xb_ref, acc_ref):
    j = pl.program_id(1)

    @pl.when(j == 0)
    def _():
        xb_ref[...] = x_ref[...].astype(BF16)

    xb = xb_ref[...]
    g = _dot(xb, wg_ref[...])
    u = _dot(xb, wu_ref[...])
    h = (g * jax.nn.sigmoid(g) * u).astype(BF16)
    part = _dot(h, wd_ref[...])

    @pl.when(j == 0)
    def _():
        acc_ref[...] = part

    @pl.when(j > 0)
    def _():
        acc_ref[...] += part

    @pl.when(j == pl.num_programs(1) - 1)
    def _():
        y = ALPHA * x_ref[...] + 0.5 * acc_ref[...]
        o_ref[...] = _layer_norm(y, g_ref[...], b_ref[...])


def _ffn(x, w_gu, w_down, ln_g, ln_b, *, tm, tf=512):
    m = x.shape[0]
    nf = D_FF // tf
    return pl.pallas_call(
        _ffn_kernel,
        grid=(m // tm, nf),
        in_specs=[
            pl.BlockSpec((tm, D_MODEL), lambda i, j: (i, 0)),
            pl.BlockSpec((D_MODEL, tf), lambda i, j: (0, j)),
            pl.BlockSpec((D_MODEL, tf), lambda i, j: (0, j + nf)),
            pl.BlockSpec((tf, D_MODEL), lambda i, j: (j, 0)),
            pl.BlockSpec((1, D_MODEL), lambda i, j: (0, 0)),
            pl.BlockSpec((1, D_MODEL), lambda i, j: (0, 0)),
        ],
        out_specs=pl.BlockSpec((tm, D_MODEL), lambda i, j: (i, 0)),
        out_shape=jax.ShapeDtypeStruct((m, D_MODEL), F32),
        scratch_shapes=[pltpu.VMEM((tm, D_MODEL), BF16), pltpu.VMEM((tm, D_MODEL), F32)],
        compiler_params=_params(("arbitrary", "arbitrary"), 48),
        name="ffn_ln",
    )(x, w_gu, w_gu, w_down, ln_g, ln_b)


def _memkv_kernel(x_ref, wk_ref, wv_ref, g_ref, b_ref, k_ref, v_ref, xb_ref):
    @pl.when(pl.program_id(1) == 0)
    def _():
        xb_ref[...] = _layer_norm(x_ref[...], g_ref[...], b_ref[...]).astype(BF16)

    xb = xb_ref[...]
    k_ref[...] = _dot(xb, wk_ref[...])
    v_ref[...] = _dot(xb, wv_ref[...])


def _memkv(mem, w_kv, ln_g, ln_b, *, tm=512, tn=512):
    m = mem.shape[0]
    nn = BR_W // tn
    return pl.pallas_call(
        _memkv_kernel,
        grid=(m // tm, nn),
        in_specs=[
            pl.BlockSpec((tm, D_MODEL), lambda i, j: (i, 0)),
            pl.BlockSpec((D_MODEL, tn), lambda i, j: (0, j)),
            pl.BlockSpec((D_MODEL, tn), lambda i, j: (0, j + nn)),
            pl.BlockSpec((1, D_MODEL), lambda i, j: (0, 0)),
            pl.BlockSpec((1, D_MODEL), lambda i, j: (0, 0)),
        ],
        out_specs=[pl.BlockSpec((tm, tn), lambda i, j: (i, j)),
                   pl.BlockSpec((tm, tn), lambda i, j: (i, j))],
        out_shape=[jax.ShapeDtypeStruct((m, BR_W), F32), jax.ShapeDtypeStruct((m, BR_W), F32)],
        scratch_shapes=[pltpu.VMEM((tm, D_MODEL), BF16)],
        compiler_params=_params(("arbitrary", "arbitrary"), 32),
        name="mem_kv",
    )(mem, w_kv, w_kv, ln_g, ln_b)


def _inproj_kernel(x_ref, w_ref, gb_ref, lg_ref, lb_ref, o_ref, xb_ref):
    j = pl.program_id(1)

    @pl.when(j == 0)
    def _():
        xb_ref[...] = x_ref[...].astype(BF16)

    z = _dot(xb_ref[...], w_ref[...])

    @pl.when((j == 0) | (j == 3))
    def _():
        o_ref[...] = jax.nn.gelu(z)

    @pl.when(j == 1)
    def _():
        o_ref[...] = _layer_norm(jax.nn.gelu(z), lg_ref[...], lb_ref[...])

    @pl.when((j == 2) | (j == 4))
    def _():
        o_ref[...] = z

    @pl.when(j >= GATE_BLOCK0)
    def _():
        o_ref[...] = jax.nn.sigmoid(z + gb_ref[...])


def _inproj(x, w_in, gate_b, gln_g, gln_b, *, tm):
    m = x.shape[0]
    return pl.pallas_call(
        _inproj_kernel,
        grid=(m // tm, IN_BLOCKS),
        in_specs=[
            pl.BlockSpec((tm, D_MODEL), lambda i, j: (i, 0)),
            pl.BlockSpec((D_MODEL, BR_W), lambda i, j: (0, j)),
            pl.BlockSpec((1, BR_W), lambda i, j: (0, jnp.maximum(j - GATE_BLOCK0, 0))),
            pl.BlockSpec((1, BR_W), lambda i, j: (0, 0)),
            pl.BlockSpec((1, BR_W), lambda i, j: (0, 0)),
        ],
        out_specs=pl.BlockSpec((tm, BR_W), lambda i, j: (i, j)),
        out_shape=jax.ShapeDtypeStruct((m, IN_BLOCKS * BR_W), F32),
        scratch_shapes=[pltpu.VMEM((tm, D_MODEL), BF16)],
        compiler_params=_params(("arbitrary", "arbitrary"), 48),
        name="in_proj",
    )(x, w_in, gate_b, gln_g, gln_b)


def _softplus(x):
    return jnp.maximum(x, 0.0) + jnp.log1p(jnp.exp(-jnp.abs(x)))


def _lru_coeffs(xc, wa_ref, ba_ref, wx_ref, bx_ref, lam_ref):
    xcb = xc.astype(BF16)
    r_parts, i_parts = [], []
    for k in range(LRU_BLOCKS):
        xk = xcb[:, k * LRU_BLOCK:(k + 1) * LRU_BLOCK]
        r_parts.append(_dot(xk, wa_ref[k]))
        i_parts.append(_dot(xk, wx_ref[k]))
    r = jax.nn.sigmoid(jnp.concatenate(r_parts, axis=1) + ba_ref[...])
    i = jax.nn.sigmoid(jnp.concatenate(i_parts, axis=1) + bx_ref[...])
    log_a = (-LRU_C) * r * _softplus(-lam_ref[...])
    a = jnp.exp(log_a)
    b = jnp.sqrt(-jnp.tanh(log_a) * (a * a + 1.0)) * (i * xc)
    return a, b


def _mixer_kernel(gu_ref, v_ref, rx_ref, grg_ref, q_ref, k_ref, vm_ref,
                  ws_ref, bs_ref, cw_ref, cb_ref, wa_ref, ba_ref, wx_ref, bx_ref, lam_ref,
                  y_ref, conv_ref, hlast_ref,
                  xpad_ref, a_ref, b_ref, h_ref, hc_ref, *, tm):
    t = pl.program_id(1)

    tri = (lax.broadcasted_iota(jnp.int32, (CHUNK, CHUNK), 0)
           >= lax.broadcasted_iota(jnp.int32, (CHUNK, CHUNK), 1))
    for g in range(GMLP_GROUPS):
        wg = jnp.where(tri, ws_ref[g], 0.0).astype(BF16)
        cols = slice(g * GROUP_W, (g + 1) * GROUP_W)
        for c in range(tm // CHUNK):
            rows = slice(c * CHUNK, (c + 1) * CHUNK)
            s = _dot(wg, v_ref[rows, cols].astype(BF16)) + bs_ref[:, cols]
            y_ref[0, rows, cols] = (gu_ref[rows, cols] * s).astype(BF16)

    @pl.when(t == 0)
    def _():
        xpad_ref[0:SUBLANES, :] = jnp.zeros((SUBLANES, BR_W), F32)
        hc_ref[...] = jnp.zeros((1, BR_W), F32)

    @pl.when(t > 0)
    def _():
        xpad_ref[0:SUBLANES, :] = xpad_ref[tm:tm + SUBLANES, :]

    xpad_ref[SUBLANES:SUBLANES + tm, :] = rx_ref[...]
    xc = cb_ref[...]
    for k in range(CONV_W):
        off = SUBLANES - (CONV_W - 1) + k
        xc = xc + xpad_ref[off:off + tm, :] * cw_ref[k:k + 1, :]

    a, b = _lru_coeffs(xc, wa_ref, ba_ref, wx_ref, bx_ref, lam_ref)
    a_ref[...] = a
    b_ref[...] = b

    row = lax.broadcasted_iota(jnp.int32, (SUBLANES, BR_W), 0)
    keep = [row >= d for d in (1, 2, 4)]

    def scan_block(blk, h):
        base = pl.multiple_of(blk * SUBLANES, SUBLANES)
        ca = a_ref[pl.ds(base, SUBLANES), :]
        cb = b_ref[pl.ds(base, SUBLANES), :]
        for d, kp in zip((1, 2, 4), keep):
            a_sh = jnp.where(kp, pltpu.roll(ca, d, 0), 1.0)
            b_sh = jnp.where(kp, pltpu.roll(cb, d, 0), 0.0)
            cb = ca * b_sh + cb
            ca = ca * a_sh
        hh = cb + ca * h
        h_ref[pl.ds(base, SUBLANES), :] = hh
        return hh[SUBLANES - 1:SUBLANES, :]

    h_end = lax.fori_loop(0, tm // SUBLANES, scan_block, hc_ref[...])
    hc_ref[...] = h_end
    y_ref[1] = (grg_ref[...] * h_ref[...]).astype(BF16)

    @pl.when(t == pl.num_programs(1) - 1)
    def _():
        conv_ref[0] = rx_ref[tm - (CONV_W - 1):tm, :]
        hlast_ref[0] = h_end

    scale = XA_HEAD_DIM ** -0.5
    for hd in range(XA_HEADS):
        cols = slice(hd * XA_HEAD_DIM, (hd + 1) * XA_HEAD_DIM)
        qh = q_ref[:, cols].astype(BF16)
        kh = k_ref[0, :, cols].astype(BF16)
        vh = vm_ref[0, :, cols].astype(BF16)
        s = lax.dot_general(qh, kh, (((1,), (1,)), ((), ())), preferred_element_type=F32) * scale
        e = jnp.exp(s - jnp.max(s, axis=-1, keepdims=True))
        p = e / jnp.sum(e, axis=-1, keepdims=True)
        y_ref[2, :, cols] = _dot(p.astype(BF16), vh).astype(BF16)


def _mixer_prompt(z, k_mem, v_mem, ws, bs_mat, conv_w, conv_b, wa, ba, wx, bx, lam, *, bsz, seq, tm=256):
    nt = seq // tm
    zspec = lambda c: pl.BlockSpec((tm, BR_W), lambda b, t, c=c: (b * nt + t, c))
    full = lambda shape: pl.BlockSpec(shape, lambda b, t: (0,) * len(shape))
    return pl.pallas_call(
        functools.partial(_mixer_kernel, tm=tm),
        grid=(bsz, nt),
        in_specs=[zspec(0), zspec(1), zspec(2), zspec(3), zspec(4),
                  pl.BlockSpec((1, N_MEM, BR_W), lambda b, t: (b, 0, 0)),
                  pl.BlockSpec((1, N_MEM, BR_W), lambda b, t: (b, 0, 0)),
                  full((GMLP_GROUPS, CHUNK, CHUNK)), full((CHUNK, BR_W)),
                  full((CONV_W, BR_W)), full((1, BR_W)),
                  full((LRU_BLOCKS, LRU_BLOCK, LRU_BLOCK)), full((1, BR_W)),
                  full((LRU_BLOCKS, LRU_BLOCK, LRU_BLOCK)), full((1, BR_W)),
                  full((1, BR_W))],
        out_specs=[pl.BlockSpec((3, tm, BR_W), lambda b, t: (0, b * nt + t, 0)),
                   pl.BlockSpec((1, CONV_W - 1, BR_W), lambda b, t: (b, 0, 0)),
                   pl.BlockSpec((1, 1, BR_W), lambda b, t: (b, 0, 0))],
        out_shape=[jax.ShapeDtypeStruct((3, bsz * seq, BR_W), BF16),
                   jax.ShapeDtypeStruct((bsz, CONV_W - 1, BR_W), F32),
                   jax.ShapeDtypeStruct((bsz, 1, BR_W), F32)],
        scratch_shapes=[pltpu.VMEM((tm + 2 * SUBLANES, BR_W), F32),
                        pltpu.VMEM((tm, BR_W), F32), pltpu.VMEM((tm, BR_W), F32),
                        pltpu.VMEM((tm, BR_W), F32), pltpu.VMEM((1, BR_W), F32)],
        compiler_params=_params(("arbitrary", "arbitrary"), 40),
        name="mixer_prompt",
    )(z, z, z, z, z, k_mem, v_mem, ws, bs_mat, conv_w, conv_b, wa, ba, wx, bx, lam)


def _mixer_sample_kernel(gu_ref, v_ref, rx_ref, grg_ref, cs_ref, h0_ref,
                         wv_ref, bv_ref, cw_ref, cb_ref, wa_ref, ba_ref, wx_ref, bx_ref, lam_ref,
                         y_ref, conv_ref, h_ref):
    y_ref[0] = (gu_ref[...] * (wv_ref[...] * v_ref[...] + bv_ref[...])).astype(BF16)

    rx = rx_ref[...]
    xc = cb_ref[...] + rx * cw_ref[CONV_W - 1:CONV_W, :]
    for k in range(CONV_W - 1):
        xc = xc + cs_ref[k] * cw_ref[k:k + 1, :]
    for k in range(CONV_W - 2):
        conv_ref[k] = cs_ref[k + 1]
    conv_ref[CONV_W - 2] = rx

    a, b = _lru_coeffs(xc, wa_ref, ba_ref, wx_ref, bx_ref, lam_ref)
    h = a * h0_ref[...] + b
    h_ref[...] = h
    y_ref[1] = (grg_ref[...] * h).astype(BF16)


def _mixer_sample(z, conv_state, h0, wvec, bvec, conv_w, conv_b, wa, ba, wx, bx, lam):
    n = z.shape[0]
    zspec = lambda c: pl.BlockSpec((n, BR_W), lambda i, c=c: (0, c))
    full = lambda shape: pl.BlockSpec(shape, lambda i: (0,) * len(shape))
    return pl.pallas_call(
        _mixer_sample_kernel,
        grid=(1,),
        in_specs=[zspec(0), zspec(1), zspec(2), zspec(3),
                  full((CONV_W - 1, n, BR_W)), full((n, BR_W)),
                  full((1, BR_W)), full((1, BR_W)), full((CONV_W, BR_W)), full((1, BR_W)),
                  full((LRU_BLOCKS, LRU_BLOCK, LRU_BLOCK)), full((1, BR_W)),
                  full((LRU_BLOCKS, LRU_BLOCK, LRU_BLOCK)), full((1, BR_W)),
                  full((1, BR_W))],
        out_specs=[full((2, n, BR_W)), full((CONV_W - 1, n, BR_W)), full((n, BR_W))],
        out_shape=[jax.ShapeDtypeStruct((2, n, BR_W), BF16),
                   jax.ShapeDtypeStruct((CONV_W - 1, n, BR_W), F32),
                   jax.ShapeDtypeStruct((n, BR_W), F32)],
        compiler_params=_params(("arbitrary",), 32),
        name="mixer_sample",
    )(z, z, z, z, conv_state, h0, wvec, bvec, conv_w, conv_b, wa, ba, wx, bx, lam)


def _xattn_sample_kernel(q_ref, k_ref, v_ref, o_ref, *, tb):
    scale = XA_HEAD_DIM ** -0.5

    def one(b, carry):
        q = q_ref[b]
        prod = k_ref[b] * q
        v = v_ref[b]
        outs = []
        for hd in range(XA_HEADS):
            cols = slice(hd * XA_HEAD_DIM, (hd + 1) * XA_HEAD_DIM)
            s = jnp.sum(prod[:, cols], axis=1, keepdims=True) * scale
            e = jnp.exp(s - jnp.max(s, axis=0, keepdims=True))
            p = e / jnp.sum(e, axis=0, keepdims=True)
            outs.append(jnp.sum(p * v[:, cols], axis=0, keepdims=True))
        o_ref[b] = jnp.concatenate(outs, axis=1).astype(BF16)
        return carry

    lax.fori_loop(0, tb, one, 0)


def _xattn_sample(q, k_cache, v_cache, *, tb=4):
    n = q.shape[0]
    return pl.pallas_call(
        functools.partial(_xattn_sample_kernel, tb=tb),
        grid=(n // tb,),
        in_specs=[pl.BlockSpec((tb, 1, BR_W), lambda i: (i, 0, 0)),
                  pl.BlockSpec((tb, N_MEM, BR_W), lambda i: (i, 0, 0)),
                  pl.BlockSpec((tb, N_MEM, BR_W), lambda i: (i, 0, 0))],
        out_specs=pl.BlockSpec((tb, 1, BR_W), lambda i: (i, 0, 0)),
        out_shape=jax.ShapeDtypeStruct((n, 1, BR_W), BF16),
        compiler_params=_params(("arbitrary",), 32),
        name="xattn_sample",
    )(q, k_cache, v_cache)


def _merge_kernel(y_ref, wb_ref, gate_ref, wo_ref, x_ref, g_ref, b_ref, o_ref, merged_ref, mb_ref):
    s = pl.program_id(1)
    n_branch_steps = 6

    @pl.when(s == 0)
    def _():
        merged_ref[...] = jnp.zeros_like(merged_ref)

    for n in range(2):
        cols = slice(n * BR_W, (n + 1) * BR_W)

        @pl.when((s < n_branch_steps) & (s % 2 == n))
        def _():
            merged_ref[:, cols] += gate_ref[...] * _dot(y_ref[0], wb_ref[0])

        @pl.when(s == n_branch_steps + n)
        def _():
            if n == 0:
                mb_ref[...] = merged_ref[...].astype(BF16)
            o_ref[:, cols] = _dot(mb_ref[...], wo_ref[...])

    @pl.when(s == n_branch_steps + 1)
    def _():
        o_ref[...] = _layer_norm(ALPHA * x_ref[...] + o_ref[...], g_ref[...], b_ref[...])


def _merge(y, w_branch, z, w_out, x, ln_g, ln_b, *, tm):
    m = x.shape[0]
    return pl.pallas_call(
        _merge_kernel,
        grid=(m // tm, 8),
        in_specs=[
            pl.BlockSpec((1, tm, BR_W), lambda i, s: (jnp.minimum(s // 2, 2), i, 0)),
            pl.BlockSpec((1, BR_W, BR_W), lambda i, s: (jnp.minimum(s // 2, 2), 0, jnp.where(s < 6, s % 2, 1))),
            pl.BlockSpec((tm, BR_W), lambda i, s: (i, GATE_BLOCK0 + jnp.minimum(s, 5))),
            pl.BlockSpec((D_MODEL, BR_W), lambda i, s: (0, jnp.maximum(s - 6, 0))),
            pl.BlockSpec((tm, D_MODEL), lambda i, s: (i, 0)),
            pl.BlockSpec((1, D_MODEL), lambda i, s: (0, 0)),
            pl.BlockSpec((1, D_MODEL), lambda i, s: (0, 0)),
        ],
        out_specs=pl.BlockSpec((tm, D_MODEL), lambda i, s: (i, 0)),
        out_shape=jax.ShapeDtypeStruct((m, D_MODEL), F32),
        scratch_shapes=[pltpu.VMEM((tm, D_MODEL), F32), pltpu.VMEM((tm, D_MODEL), BF16)],
        compiler_params=_params(("arbitrary", "arbitrary"), 48),
        name="merge_out",
    )(y, w_branch, z, w_out, x, ln_g, ln_b)


def kernel(x_prompt, x_sample, mem_prompt, cache_mem_k, cache_mem_v, state_conv, state_lru_h, ffn1_w_gu, ffn1_w_down, ln1_g, ln1_b, w_in, gate_b, gmlp_ln_g, gmlp_ln_b, gmlp_w_s, gmlp_b_s, conv_w, conv_b, lru_w_a, lru_b_a, lru_w_x, lru_b_x, lru_lambda, mem_ln_g, mem_ln_b, w_mem_kv, w_branch, w_out, ln2_g, ln2_b, ffn2_w_gu, ffn2_w_down, ln3_g, ln3_b):
    bsz, seq, _ = x_prompt.shape
    n_s = x_sample.shape[0]
    l = 0

    w1gu, w1d = ffn1_w_gu[l].astype(BF16), ffn1_w_down[l].astype(BF16)
    w2gu, w2d = ffn2_w_gu[l].astype(BF16), ffn2_w_down[l].astype(BF16)
    win, wkv = w_in[l].astype(BF16), w_mem_kv[l].astype(BF16)
    wbr, wout = w_branch[l].astype(BF16), w_out[l].astype(BF16)
    wa, wx = lru_w_a[l].astype(BF16), lru_w_x[l].astype(BF16)

    row = lambda p: p[l].reshape(1, -1)
    gb = gate_b[l].reshape(1, -1)
    bs_mat = jnp.repeat(gmlp_b_s[l].T, GROUP_W, axis=1)
    wvec = jnp.repeat(gmlp_w_s[l][:, 0, 0], GROUP_W).reshape(1, -1)
    bvec = jnp.repeat(gmlp_b_s[l][:, 0], GROUP_W).reshape(1, -1)
    lru = (wa, row(lru_b_a), wx, row(lru_b_x), row(lru_lambda))

    xp = x_prompt.reshape(bsz * seq, D_MODEL)
    xs = x_sample.reshape(n_s, D_MODEL)

    k_p, v_p = _memkv(mem_prompt.reshape(bsz * N_MEM, D_MODEL), wkv, row(mem_ln_g), row(mem_ln_b))
    x1p = _ffn(xp, w1gu, w1d, row(ln1_g), row(ln1_b), tm=512)
    zp = _inproj(x1p, win, gb, row(gmlp_ln_g), row(gmlp_ln_b), tm=1024)
    yp, conv_p, hlast_p = _mixer_prompt(
        zp, k_p.reshape(bsz, N_MEM, BR_W), v_p.reshape(bsz, N_MEM, BR_W), gmlp_w_s[l], bs_mat,
        conv_w[l], row(conv_b), *lru, bsz=bsz, seq=seq)
    x2p = _merge(yp, wbr, zp, wout, x1p, row(ln2_g), row(ln2_b), tm=512)
    x3p = _ffn(x2p, w2gu, w2d, row(ln3_g), row(ln3_b), tm=512)

    x1s = _ffn(xs, w1gu, w1d, row(ln1_g), row(ln1_b), tm=n_s)
    zs = _inproj(x1s, win, gb, row(gmlp_ln_g), row(gmlp_ln_b), tm=n_s)
    y01, conv_s, h_s = _mixer_sample(
        zs, jnp.swapaxes(state_conv[l], 0, 1), state_lru_h[l], wvec, bvec, conv_w[l], row(conv_b), *lru)
    q_s = zs[:, 4 * BR_W:5 * BR_W].reshape(n_s, 1, BR_W)
    yxa = _xattn_sample(q_s, cache_mem_k[l].reshape(n_s, N_MEM, BR_W), cache_mem_v[l].reshape(n_s, N_MEM, BR_W))
    ys = jnp.concatenate([y01, yxa.reshape(1, n_s, BR_W)], axis=0)
    x2s = _merge(ys, wbr, zs, wout, x1s, row(ln2_g), row(ln2_b), tm=n_s)
    x3s = _ffn(x2s, w2gu, w2d, row(ln3_g), row(ln3_b), tm=n_s)

    kv_shape = (1, bsz, N_MEM, XA_HEADS, XA_HEAD_DIM)
    return (x3p.reshape(bsz, seq, D_MODEL),
            x3s.reshape(n_s, 1, D_MODEL),
            k_p.reshape(kv_shape),
            v_p.reshape(kv_shape),
            conv_p[None],
            hlast_p.reshape(1, bsz, BR_W),
            jnp.swapaxes(conv_s, 0, 1)[None],
            h_s[None],
            zs[:, BR_W:2 * BR_W].reshape(1, n_s, 1, BR_W))
```

```python
import functools

import jax
import jax.numpy as jnp
from jax import lax
from jax.experimental import pallas as pl
from jax.experimental.pallas import tpu as pltpu

F32 = jnp.float32
BF16 = jnp.bfloat16

D_MODEL = 2048
BR_W = 1024
D_FF = 5632
N_MEM = 256
XA_HEADS = 4
XA_HEAD_DIM = 256
GMLP_GROUPS = 4
GROUP_W = BR_W // GMLP_GROUPS
CHUNK = 128
LRU_BLOCKS = 8
LRU_BLOCK = 128
CONV_W = 4
LRU_C = 8.0
LN_EPS = 1e-5
ALPHA = 2.0 ** 0.25
IN_BLOCKS = 11
RX_BLOCK = 2
GATE_BLOCK0 = 5
Z_BLOCKS = IN_BLOCKS - 1
Z_GU, Z_V, Z_GRG, Z_Q, Z_GATE0 = 0, 1, 2, 3, 4
SUBLANES = 8
MiB = 1024 * 1024


def _params(semantics, vmem_mib):
    return pltpu.CompilerParams(dimension_semantics=semantics, vmem_limit_bytes=vmem_mib * MiB)


def _layer_norm(x, g, b):
    mu = jnp.mean(x, axis=-1, keepdims=True)
    xc = x - mu
    var = jnp.mean(xc * xc, axis=-1, keepdims=True)
    return xc * lax.rsqrt(var + LN_EPS) * g + b


def _dot(a, b):
    return jnp.dot(a, b, preferred_element_type=F32)


def _ffn_kernel(x_ref, wg_ref, wu_ref, wd_ref, g_ref, b_ref, o_ref, xb_ref, acc_ref):
    j = pl.program_id(1)

    @pl.when(j == 0)
    def _():
        xb_ref[...] = x_ref[...].astype(BF16)
        acc_ref[...] = jnp.zeros_like(acc_ref)

    xb = xb_ref[...]
    g = _dot(xb, wg_ref[...])
    u = _dot(xb, wu_ref[...])
    h = (g * jax.nn.sigmoid(g) * u).astype(BF16)
    acc_ref[...] += _dot(h, wd_ref[...])

    @pl.when(j == pl.num_programs(1) - 1)
    def _():
        y = ALPHA * x_ref[...] + 0.5 * acc_ref[...]
        o_ref[...] = _layer_norm(y, g_ref[...], b_ref[...])


def _ffn(x, w_gu, w_down, ln_g, ln_b, *, tm, tf=512):
    m = x.shape[0]
    nf = D_FF // tf
    return pl.pallas_call(
        _ffn_kernel,
        grid=(m // tm, nf),
        in_specs=[
            pl.BlockSpec((tm, D_MODEL), lambda i, j: (i, 0)),
            pl.BlockSpec((D_MODEL, tf), lambda i, j: (0, j)),
            pl.BlockSpec((D_MODEL, tf), lambda i, j: (0, j + nf)),
            pl.BlockSpec((tf, D_MODEL), lambda i, j: (j, 0)),
            pl.BlockSpec((1, D_MODEL), lambda i, j: (0, 0)),
            pl.BlockSpec((1, D_MODEL), lambda i, j: (0, 0)),
        ],
        out_specs=pl.BlockSpec((tm, D_MODEL), lambda i, j: (i, 0)),
        out_shape=jax.ShapeDtypeStruct((m, D_MODEL), F32),
        scratch_shapes=[pltpu.VMEM((tm, D_MODEL), BF16), pltpu.VMEM((tm, D_MODEL), F32)],
        compiler_params=_params(("arbitrary", "arbitrary"), 48),
        name="ffn_ln",
    )(x, w_gu, w_gu, w_down, ln_g, ln_b)


def _memkv_kernel(x_ref, wk_ref, wv_ref, g_ref, b_ref, k_ref, v_ref, xb_ref):
    @pl.when(pl.program_id(1) == 0)
    def _():
        xb_ref[...] = _layer_norm(x_ref[...], g_ref[...], b_ref[...]).astype(BF16)

    xb = xb_ref[...]
    k_ref[...] = _dot(xb, wk_ref[...])
    v_ref[...] = _dot(xb, wv_ref[...])


def _memkv(mem, w_kv, ln_g, ln_b, *, tm=512, tn=512):
    m = mem.shape[0]
    nn = BR_W // tn
    return pl.pallas_call(
        _memkv_kernel,
        grid=(m // tm, nn),
        in_specs=[
            pl.BlockSpec((tm, D_MODEL), lambda i, j: (i, 0)),
            pl.BlockSpec((D_MODEL, tn), lambda i, j: (0, j)),
            pl.BlockSpec((D_MODEL, tn), lambda i, j: (0, j + nn)),
            pl.BlockSpec((1, D_MODEL), lambda i, j: (0, 0)),
            pl.BlockSpec((1, D_MODEL), lambda i, j: (0, 0)),
        ],
        out_specs=[pl.BlockSpec((tm, tn), lambda i, j: (i, j)),
                   pl.BlockSpec((tm, tn), lambda i, j: (i, j))],
        out_shape=[jax.ShapeDtypeStruct((m, BR_W), F32), jax.ShapeDtypeStruct((m, BR_W), F32)],
        scratch_shapes=[pltpu.VMEM((tm, D_MODEL), BF16)],
        compiler_params=_params(("arbitrary", "arbitrary"), 32),
        name="mem_kv",
    )(mem, w_kv, w_kv, ln_g, ln_b)


def _inproj_kernel(x_ref, w_ref, gb_ref, lg_ref, lb_ref, o_ref, rx_ref, xb_ref):
    j = pl.program_id(1)

    @pl.when(j == 0)
    def _():
        xb_ref[...] = x_ref[...].astype(BF16)

    def z():
        return _dot(xb_ref[...], w_ref[...])

    @pl.when((j == 0) | (j == 3))
    def _():
        o_ref[...] = jax.nn.gelu(z()).astype(o_ref.dtype)

    @pl.when(j == 1)
    def _():
        o_ref[...] = _layer_norm(jax.nn.gelu(z()), lg_ref[...], lb_ref[...]).astype(o_ref.dtype)

    @pl.when(j == 2)
    def _():
        rx_ref[...] = z()

    @pl.when(j == 4)
    def _():
        o_ref[...] = z().astype(o_ref.dtype)

    @pl.when(j >= GATE_BLOCK0)
    def _():
        o_ref[...] = jax.nn.sigmoid(z() + gb_ref[...]).astype(o_ref.dtype)


def _inproj(x, w_in, gate_b, gln_g, gln_b, *, tm, z_dtype):
    m = x.shape[0]
    return pl.pallas_call(
        _inproj_kernel,
        grid=(m // tm, IN_BLOCKS),
        in_specs=[
            pl.BlockSpec((tm, D_MODEL), lambda i, j: (i, 0)),
            pl.BlockSpec((D_MODEL, BR_W), lambda i, j: (0, j)),
            pl.BlockSpec((1, BR_W), lambda i, j: (0, jnp.maximum(j - GATE_BLOCK0, 0))),
            pl.BlockSpec((1, BR_W), lambda i, j: (0, 0)),
            pl.BlockSpec((1, BR_W), lambda i, j: (0, 0)),
        ],
        out_specs=[pl.BlockSpec((tm, BR_W), lambda i, j: (i, jnp.where(j < RX_BLOCK, j, j - 1))),
                   pl.BlockSpec((tm, BR_W), lambda i, j: (i, 0))],
        out_shape=[jax.ShapeDtypeStruct((m, Z_BLOCKS * BR_W), z_dtype),
                   jax.ShapeDtypeStruct((m, BR_W), F32)],
        scratch_shapes=[pltpu.VMEM((tm, D_MODEL), BF16)],
        compiler_params=_params(("arbitrary", "arbitrary"), 48),
        name="in_proj",
    )(x, w_in, gate_b, gln_g, gln_b)


def _softplus(x):
    return jnp.maximum(x, 0.0) + jnp.log1p(jnp.exp(-jnp.abs(x)))


def _lru_coeffs(xc, wa_ref, ba_ref, wx_ref, bx_ref, lam_ref):
    xcb = xc.astype(BF16)
    r_parts, i_parts = [], []
    for k in range(LRU_BLOCKS):
        xk = xcb[:, k * LRU_BLOCK:(k + 1) * LRU_BLOCK]
        r_parts.append(_dot(xk, wa_ref[k]))
        i_parts.append(_dot(xk, wx_ref[k]))
    r = jax.nn.sigmoid(jnp.concatenate(r_parts, axis=1) + ba_ref[...])
    i = jax.nn.sigmoid(jnp.concatenate(i_parts, axis=1) + bx_ref[...])
    log_a = (-LRU_C) * r * _softplus(-lam_ref[...])
    a = jnp.exp(log_a)
    b = jnp.sqrt(-jnp.tanh(log_a) * (a * a + 1.0)) * (i * xc)
    return a, b


def _mixer_kernel(gu_ref, v_ref, rx_ref, grg_ref, q_ref, k_ref, vm_ref,
                  ws_ref, bs_ref, cw_ref, cb_ref, wa_ref, ba_ref, wx_ref, bx_ref, lam_ref,
                  y_ref, conv_ref, hlast_ref,
                  xpad_ref, a_ref, b_ref, h_ref, hc_ref, *, tm):
    t = pl.program_id(1)

    tri = (lax.broadcasted_iota(jnp.int32, (CHUNK, CHUNK), 0)
           >= lax.broadcasted_iota(jnp.int32, (CHUNK, CHUNK), 1))
    for g in range(GMLP_GROUPS):
        wg = jnp.where(tri, ws_ref[g], 0.0).astype(BF16)
        cols = slice(g * GROUP_W, (g + 1) * GROUP_W)
        for c in range(tm // CHUNK):
            rows = slice(c * CHUNK, (c + 1) * CHUNK)
            s = _dot(wg, v_ref[rows, cols].astype(BF16)) + bs_ref[:, cols]
            y_ref[0, rows, cols] = (gu_ref[rows, cols] * s).astype(BF16)

    @pl.when(t == 0)
    def _():
        xpad_ref[0:SUBLANES, :] = jnp.zeros((SUBLANES, BR_W), F32)
        hc_ref[...] = jnp.zeros((1, BR_W), F32)

    @pl.when(t > 0)
    def _():
        xpad_ref[0:SUBLANES, :] = xpad_ref[tm:tm + SUBLANES, :]

    xpad_ref[SUBLANES:SUBLANES + tm, :] = rx_ref[...]
    xc = cb_ref[...]
    for k in range(CONV_W):
        off = SUBLANES - (CONV_W - 1) + k
        xc = xc + xpad_ref[off:off + tm, :] * cw_ref[k:k + 1, :]

    a, b = _lru_coeffs(xc, wa_ref, ba_ref, wx_ref, bx_ref, lam_ref)
    a_ref[...] = a
    b_ref[...] = b

    row = lax.broadcasted_iota(jnp.int32, (SUBLANES, BR_W), 0)
    keep = [row >= d for d in (1, 2, 4)]

    def scan_block(blk, h):
        base = pl.multiple_of(blk * SUBLANES, SUBLANES)
        ca = a_ref[pl.ds(base, SUBLANES), :]
        cb = b_ref[pl.ds(base, SUBLANES), :]
        for d, kp in zip((1, 2, 4), keep):
            a_sh = jnp.where(kp, pltpu.roll(ca, d, 0), 1.0)
            b_sh = jnp.where(kp, pltpu.roll(cb, d, 0), 0.0)
            cb = ca * b_sh + cb
            ca = ca * a_sh
        hh = cb + ca * h
        h_ref[pl.ds(base, SUBLANES), :] = hh
        return hh[SUBLANES - 1:SUBLANES, :]

    h_end = lax.fori_loop(0, tm // SUBLANES, scan_block, hc_ref[...])
    hc_ref[...] = h_end
    y_ref[1] = (grg_ref[...] * h_ref[...]).astype(BF16)

    @pl.when(t == pl.num_programs(1) - 1)
    def _():
        conv_ref[0] = rx_ref[tm - (CONV_W - 1):tm, :]
        hlast_ref[0] = h_end

    scale = XA_HEAD_DIM ** -0.5
    for hd in range(XA_HEADS):
        cols = slice(hd * XA_HEAD_DIM, (hd + 1) * XA_HEAD_DIM)
        qh = q_ref[:, cols].astype(BF16)
        kh = k_ref[0, :, cols].astype(BF16)
        vh = vm_ref[0, :, cols].astype(BF16)
        s = lax.dot_general(qh, kh, (((1,), (1,)), ((), ())), preferred_element_type=F32) * scale
        e = jnp.exp(s - jnp.max(s, axis=-1, keepdims=True))
        p = e / jnp.sum(e, axis=-1, keepdims=True)
        y_ref[2, :, cols] = _dot(p.astype(BF16), vh).astype(BF16)


def _mixer_prompt(z, rx, k_mem, v_mem, ws, bs_mat, conv_w, conv_b, wa, ba, wx, bx, lam, *, bsz, seq, tm=256):
    nt = seq // tm
    zspec = lambda c: pl.BlockSpec((tm, BR_W), lambda b, t, c=c: (b * nt + t, c))
    full = lambda shape: pl.BlockSpec(shape, lambda b, t: (0,) * len(shape))
    return pl.pallas_call(
        functools.partial(_mixer_kernel, tm=tm),
        grid=(bsz, nt),
        in_specs=[zspec(Z_GU), zspec(Z_V), zspec(0), zspec(Z_GRG), zspec(Z_Q),
                  pl.BlockSpec((1, N_MEM, BR_W), lambda b, t: (b, 0, 0)),
                  pl.BlockSpec((1, N_MEM, BR_W), lambda b, t: (b, 0, 0)),
                  full((GMLP_GROUPS, CHUNK, CHUNK)), full((CHUNK, BR_W)),
                  full((CONV_W, BR_W)), full((1, BR_W)),
                  full((LRU_BLOCKS, LRU_BLOCK, LRU_BLOCK)), full((1, BR_W)),
                  full((LRU_BLOCKS, LRU_BLOCK, LRU_BLOCK)), full((1, BR_W)),
                  full((1, BR_W))],
        out_specs=[pl.BlockSpec((3, tm, BR_W), lambda b, t: (0, b * nt + t, 0)),
                   pl.BlockSpec((1, CONV_W - 1, BR_W), lambda b, t: (b, 0, 0)),
                   pl.BlockSpec((1, 1, BR_W), lambda b, t: (b, 0, 0))],
        out_shape=[jax.ShapeDtypeStruct((3, bsz * seq, BR_W), BF16),
                   jax.ShapeDtypeStruct((bsz, CONV_W - 1, BR_W), F32),
                   jax.ShapeDtypeStruct((bsz, 1, BR_W), F32)],
        scratch_shapes=[pltpu.VMEM((tm + 2 * SUBLANES, BR_W), F32),
                        pltpu.VMEM((tm, BR_W), F32), pltpu.VMEM((tm, BR_W), F32),
                        pltpu.VMEM((tm, BR_W), F32), pltpu.VMEM((1, BR_W), F32)],
        compiler_params=_params(("arbitrary", "arbitrary"), 40),
        name="mixer_prompt",
    )(z, z, rx, z, z, k_mem, v_mem, ws, bs_mat, conv_w, conv_b, wa, ba, wx, bx, lam)


def _mixer_sample_kernel(gu_ref, v_ref, rx_ref, grg_ref, cs_ref, h0_ref,
                         wv_ref, bv_ref, cw_ref, cb_ref, wa_ref, ba_ref, wx_ref, bx_ref, lam_ref,
                         y_ref, conv_ref, h_ref):
    y_ref[0] = (gu_ref[...] * (wv_ref[...] * v_ref[...] + bv_ref[...])).astype(BF16)

    rx = rx_ref[...]
    xc = cb_ref[...] + rx * cw_ref[CONV_W - 1:CONV_W, :]
    for k in range(CONV_W - 1):
        xc = xc + cs_ref[k] * cw_ref[k:k + 1, :]
    for k in range(CONV_W - 2):
        conv_ref[k] = cs_ref[k + 1]
    conv_ref[CONV_W - 2] = rx

    a, b = _lru_coeffs(xc, wa_ref, ba_ref, wx_ref, bx_ref, lam_ref)
    h = a * h0_ref[...] + b
    h_ref[...] = h
    y_ref[1] = (grg_ref[...] * h).astype(BF16)


def _mixer_sample(z, rx, conv_state, h0, wvec, bvec, conv_w, conv_b, wa, ba, wx, bx, lam):
    n = z.shape[0]
    zspec = lambda c: pl.BlockSpec((n, BR_W), lambda i, c=c: (0, c))
    full = lambda shape: pl.BlockSpec(shape, lambda i: (0,) * len(shape))
    return pl.pallas_call(
        _mixer_sample_kernel,
        grid=(1,),
        in_specs=[zspec(Z_GU), zspec(Z_V), zspec(0), zspec(Z_GRG),
                  full((CONV_W - 1, n, BR_W)), full((n, BR_W)),
                  full((1, BR_W)), full((1, BR_W)), full((CONV_W, BR_W)), full((1, BR_W)),
                  full((LRU_BLOCKS, LRU_BLOCK, LRU_BLOCK)), full((1, BR_W)),
                  full((LRU_BLOCKS, LRU_BLOCK, LRU_BLOCK)), full((1, BR_W)),
                  full((1, BR_W))],
        out_specs=[full((2, n, BR_W)), full((CONV_W - 1, n, BR_W)), full((n, BR_W))],
        out_shape=[jax.ShapeDtypeStruct((2, n, BR_W), BF16),
                   jax.ShapeDtypeStruct((CONV_W - 1, n, BR_W), F32),
                   jax.ShapeDtypeStruct((n, BR_W), F32)],
        compiler_params=_params(("arbitrary",), 32),
        name="mixer_sample",
    )(z, z, rx, z, conv_state, h0, wvec, bvec, conv_w, conv_b, wa, ba, wx, bx, lam)


def _xattn_sample_kernel(q_ref, k_ref, v_ref, o_ref, *, tb):
    scale = XA_HEAD_DIM ** -0.5

    def one(b, carry):
        q = q_ref[b]
        prod = k_ref[b].astype(F32) * q
        v = v_ref[b].astype(F32)
        outs = []
        for hd in range(XA_HEADS):
            cols = slice(hd * XA_HEAD_DIM, (hd + 1) * XA_HEAD_DIM)
            s = jnp.sum(prod[:, cols], axis=1, keepdims=True) * scale
            e = jnp.exp(s - jnp.max(s, axis=0, keepdims=True))
            p = e / jnp.sum(e, axis=0, keepdims=True)
            outs.append(jnp.sum(p * v[:, cols], axis=0, keepdims=True))
        o_ref[b] = jnp.concatenate(outs, axis=1).astype(BF16)
        return carry

    lax.fori_loop(0, tb, one, 0)


def _xattn_sample(q, k_cache, v_cache, *, tb=4):
    n = q.shape[0]
    return pl.pallas_call(
        functools.partial(_xattn_sample_kernel, tb=tb),
        grid=(n // tb,),
        in_specs=[pl.BlockSpec((tb, 1, BR_W), lambda i: (i, 0, 0)),
                  pl.BlockSpec((tb, N_MEM, BR_W), lambda i: (i, 0, 0)),
                  pl.BlockSpec((tb, N_MEM, BR_W), lambda i: (i, 0, 0))],
        out_specs=pl.BlockSpec((tb, 1, BR_W), lambda i: (i, 0, 0)),
        out_shape=jax.ShapeDtypeStruct((n, 1, BR_W), BF16),
        compiler_params=_params(("arbitrary",), 32),
        name="xattn_sample",
    )(q, k_cache, v_cache)


N_BRANCH = 3


def _merge_kernel(y_ref, wb_ref, g0_ref, g1_ref, g2_ref, o_ref):
    acc = None
    for k, gate_ref in enumerate((g0_ref, g1_ref, g2_ref)):
        term = gate_ref[...].astype(F32) * _dot(y_ref[k], wb_ref[k])
        acc = term if acc is None else acc + term
    o_ref[...] = acc.astype(BF16)


def _merge(y, w_branch, z, *, tm):
    m = y.shape[1]
    gate_spec = lambda k: pl.BlockSpec((tm, BR_W), lambda i, n, k=k: (i, Z_GATE0 + 2 * k + n))
    return pl.pallas_call(
        _merge_kernel,
        grid=(m // tm, D_MODEL // BR_W),
        in_specs=[
            pl.BlockSpec((N_BRANCH, tm, BR_W), lambda i, n: (0, i, 0)),
            pl.BlockSpec((N_BRANCH, BR_W, BR_W), lambda i, n: (0, 0, n)),
            gate_spec(0), gate_spec(1), gate_spec(2),
        ],
        out_specs=pl.BlockSpec((tm, BR_W), lambda i, n: (i, n)),
        out_shape=jax.ShapeDtypeStruct((m, D_MODEL), BF16),
        compiler_params=_params(("arbitrary", "arbitrary"), 56),
        name="merge",
    )(y, w_branch, z, z, z)


def _outproj_kernel(m_ref, wo_ref, x_ref, g_ref, b_ref, o_ref, *, row_splits):
    rows_per = m_ref.shape[0] // row_splits
    for r in range(row_splits):
        rows = slice(r * rows_per, (r + 1) * rows_per)
        y = ALPHA * x_ref[rows, :] + _dot(m_ref[rows, :], wo_ref[...])
        o_ref[rows, :] = _layer_norm(y, g_ref[...], b_ref[...])


def _outproj(merged, w_out, x, ln_g, ln_b, *, tm):
    m = x.shape[0]
    return pl.pallas_call(
        functools.partial(_outproj_kernel, row_splits=2 if tm >= 512 else 1),
        grid=(m // tm,),
        in_specs=[
            pl.BlockSpec((tm, D_MODEL), lambda i: (i, 0)),
            pl.BlockSpec((D_MODEL, D_MODEL), lambda i: (0, 0)),
            pl.BlockSpec((tm, D_MODEL), lambda i: (i, 0)),
            pl.BlockSpec((1, D_MODEL), lambda i: (0, 0)),
            pl.BlockSpec((1, D_MODEL), lambda i: (0, 0)),
        ],
        out_specs=pl.BlockSpec((tm, D_MODEL), lambda i: (i, 0)),
        out_shape=jax.ShapeDtypeStruct((m, D_MODEL), F32),
        compiler_params=_params(("arbitrary",), 48),
        name="out_proj",
    )(merged, w_out, x, ln_g, ln_b)


def kernel(x_prompt, x_sample, mem_prompt, cache_mem_k, cache_mem_v, state_conv, state_lru_h, ffn1_w_gu, ffn1_w_down, ln1_g, ln1_b, w_in, gate_b, gmlp_ln_g, gmlp_ln_b, gmlp_w_s, gmlp_b_s, conv_w, conv_b, lru_w_a, lru_b_a, lru_w_x, lru_b_x, lru_lambda, mem_ln_g, mem_ln_b, w_mem_kv, w_branch, w_out, ln2_g, ln2_b, ffn2_w_gu, ffn2_w_down, ln3_g, ln3_b):
    bsz, seq, _ = x_prompt.shape
    n_s = x_sample.shape[0]
    l = 0

    w1gu, w1d = ffn1_w_gu[l].astype(BF16), ffn1_w_down[l].astype(BF16)
    w2gu, w2d = ffn2_w_gu[l].astype(BF16), ffn2_w_down[l].astype(BF16)
    win, wkv = w_in[l].astype(BF16), w_mem_kv[l].astype(BF16)
    wbr, wout = w_branch[l].astype(BF16), w_out[l].astype(BF16)
    wa, wx = lru_w_a[l].astype(BF16), lru_w_x[l].astype(BF16)

    row = lambda p: p[l].reshape(1, -1)
    gb = gate_b[l].reshape(1, -1)
    bs_mat = jnp.repeat(gmlp_b_s[l].T, GROUP_W, axis=1)
    wvec = jnp.repeat(gmlp_w_s[l][:, 0, 0], GROUP_W).reshape(1, -1)
    bvec = jnp.repeat(gmlp_b_s[l][:, 0], GROUP_W).reshape(1, -1)
    lru = (wa, row(lru_b_a), wx, row(lru_b_x), row(lru_lambda))

    xp = x_prompt.reshape(bsz * seq, D_MODEL)
    xs = x_sample.reshape(n_s, D_MODEL)

    k_p, v_p = _memkv(mem_prompt.reshape(bsz * N_MEM, D_MODEL), wkv, row(mem_ln_g), row(mem_ln_b))
    x1p = _ffn(xp, w1gu, w1d, row(ln1_g), row(ln1_b), tm=512)
    zp, rxp = _inproj(x1p, win, gb, row(gmlp_ln_g), row(gmlp_ln_b), tm=1024, z_dtype=BF16)
    yp, conv_p, hlast_p = _mixer_prompt(
        zp, rxp, k_p.reshape(bsz, N_MEM, BR_W), v_p.reshape(bsz, N_MEM, BR_W), gmlp_w_s[l], bs_mat,
        conv_w[l], row(conv_b), *lru, bsz=bsz, seq=seq)
    mp = _merge(yp, wbr, zp, tm=1024)
    x2p = _outproj(mp, wout, x1p, row(ln2_g), row(ln2_b), tm=512)
    x3p = _ffn(x2p, w2gu, w2d, row(ln3_g), row(ln3_b), tm=512)

    x1s = _ffn(xs, w1gu, w1d, row(ln1_g), row(ln1_b), tm=n_s)
    zs, rxs = _inproj(x1s, win, gb, row(gmlp_ln_g), row(gmlp_ln_b), tm=n_s, z_dtype=F32)
    y01, conv_s, h_s = _mixer_sample(
        zs, rxs, jnp.swapaxes(state_conv[l], 0, 1), state_lru_h[l], wvec, bvec, conv_w[l], row(conv_b), *lru)
    q_s = zs[:, Z_Q * BR_W:(Z_Q + 1) * BR_W].reshape(n_s, 1, BR_W)
    kc = cache_mem_k[l].astype(BF16).reshape(n_s, N_MEM, BR_W)
    vc = cache_mem_v[l].astype(BF16).reshape(n_s, N_MEM, BR_W)
    yxa = _xattn_sample(q_s, kc, vc)
    ys = jnp.concatenate([y01, yxa.reshape(1, n_s, BR_W)], axis=0)
    ms = _merge(ys, wbr, zs, tm=n_s)
    x2s = _outproj(ms, wout, x1s, row(ln2_g), row(ln2_b), tm=n_s)
    x3s = _ffn(x2s, w2gu, w2d, row(ln3_g), row(ln3_b), tm=n_s)

    kv_shape = (1, bsz, N_MEM, XA_HEADS, XA_HEAD_DIM)
    return (x3p.reshape(bsz, seq, D_MODEL),
            x3s.reshape(n_s, 1, D_MODEL),
            k_p.reshape(kv_shape),
            v_p.reshape(kv_shape),
            conv_p[None],
            hlast_p.reshape(1, bsz, BR_W),
            jnp.swapaxes(conv_s, 0, 1)[None],
            h_s[None],
            zs[:, Z_V * BR_W:(Z_V + 1) * BR_W].reshape(1, n_s, 1, BR_W))
```

```python
import functools

import jax
import jax.numpy as jnp
from jax import lax
from jax.experimental import pallas as pl
from jax.experimental.pallas import tpu as pltpu

F32 = jnp.float32
BF16 = jnp.bfloat16

D_MODEL = 2048
BR_W = 1024
D_FF = 5632
N_MEM = 256
XA_HEADS = 4
XA_HEAD_DIM = 256
GMLP_GROUPS = 4
GROUP_W = BR_W // GMLP_GROUPS
CHUNK = 128
LRU_BLOCKS = 8
LRU_BLOCK = 128
CONV_W = 4
LRU_C = 8.0
LN_EPS = 1e-5
ALPHA = 2.0 ** 0.25
IN_BLOCKS = 11
RX_BLOCK = 2
GATE_BLOCK0 = 5
Z_BLOCKS = IN_BLOCKS - 1
Z_GU, Z_V, Z_GRG, Z_Q, Z_GATE0 = 0, 1, 2, 3, 4
SUBLANES = 8
MiB = 1024 * 1024


def _params(semantics, vmem_mib):
    return pltpu.CompilerParams(dimension_semantics=semantics, vmem_limit_bytes=vmem_mib * MiB)


def _layer_norm(x, g, b):
    mu = jnp.mean(x, axis=-1, keepdims=True)
    xc = x - mu
    var = jnp.mean(xc * xc, axis=-1, keepdims=True)
    return xc * lax.rsqrt(var + LN_EPS) * g + b


def _dot(a, b):
    return jnp.dot(a, b, preferred_element_type=F32)


def _ffn_kernel(x_ref, wg_ref, wu_ref, wd_ref, g_ref, b_ref, o_ref, xb_ref, acc_ref):
    j = pl.program_id(1)

    @pl.when(j == 0)
    def _():
        xb_ref[...] = x_ref[...].astype(BF16)
        acc_ref[...] = jnp.zeros_like(acc_ref)

    xb = xb_ref[...]
    g = _dot(xb, wg_ref[...])
    u = _dot(xb, wu_ref[...])
    h = (g * jax.nn.sigmoid(g) * u).astype(BF16)
    acc_ref[...] += _dot(h, wd_ref[...])

    @pl.when(j == pl.num_programs(1) - 1)
    def _():
        y = ALPHA * x_ref[...] + 0.5 * acc_ref[...]
        o_ref[...] = _layer_norm(y, g_ref[...], b_ref[...])


def _ffn(x, w_gu, w_down, ln_g, ln_b, *, tm, tf=512):
    m = x.shape[0]
    nf = D_FF // tf
    return pl.pallas_call(
        _ffn_kernel,
        grid=(m // tm, nf),
        in_specs=[
            pl.BlockSpec((tm, D_MODEL), lambda i, j: (i, 0)),
            pl.BlockSpec((D_MODEL, tf), lambda i, j: (0, j)),
            pl.BlockSpec((D_MODEL, tf), lambda i, j: (0, j + nf)),
            pl.BlockSpec((tf, D_MODEL), lambda i, j: (j, 0)),
            pl.BlockSpec((1, D_MODEL), lambda i, j: (0, 0)),
            pl.BlockSpec((1, D_MODEL), lambda i, j: (0, 0)),
        ],
        out_specs=pl.BlockSpec((tm, D_MODEL), lambda i, j: (i, 0)),
        out_shape=jax.ShapeDtypeStruct((m, D_MODEL), F32),
        scratch_shapes=[pltpu.VMEM((tm, D_MODEL), BF16), pltpu.VMEM((tm, D_MODEL), F32)],
        compiler_params=_params(("arbitrary", "arbitrary"), 48),
        name="ffn_ln",
    )(x, w_gu, w_gu, w_down, ln_g, ln_b)


def _memkv_kernel(x_ref, wk_ref, wv_ref, g_ref, b_ref, k_ref, v_ref, xb_ref):
    @pl.when(pl.program_id(1) == 0)
    def _():
        xb_ref[...] = _layer_norm(x_ref[...], g_ref[...], b_ref[...]).astype(BF16)

    xb = xb_ref[...]
    k_ref[...] = _dot(xb, wk_ref[...])
    v_ref[...] = _dot(xb, wv_ref[...])


def _memkv(mem, w_kv, ln_g, ln_b, *, tm=512, tn=512):
    m = mem.shape[0]
    nn = BR_W // tn
    return pl.pallas_call(
        _memkv_kernel,
        grid=(m // tm, nn),
        in_specs=[
            pl.BlockSpec((tm, D_MODEL), lambda i, j: (i, 0)),
            pl.BlockSpec((D_MODEL, tn), lambda i, j: (0, j)),
            pl.BlockSpec((D_MODEL, tn), lambda i, j: (0, j + nn)),
            pl.BlockSpec((1, D_MODEL), lambda i, j: (0, 0)),
            pl.BlockSpec((1, D_MODEL), lambda i, j: (0, 0)),
        ],
        out_specs=[pl.BlockSpec((tm, tn), lambda i, j: (i, j)),
                   pl.BlockSpec((tm, tn), lambda i, j: (i, j))],
        out_shape=[jax.ShapeDtypeStruct((m, BR_W), F32), jax.ShapeDtypeStruct((m, BR_W), F32)],
        scratch_shapes=[pltpu.VMEM((tm, D_MODEL), BF16)],
        compiler_params=_params(("arbitrary", "arbitrary"), 32),
        name="mem_kv",
    )(mem, w_kv, w_kv, ln_g, ln_b)


def _inproj_kernel(x_ref, w_ref, gb_ref, lg_ref, lb_ref, o_ref, rx_ref, xb_ref):
    j = pl.program_id(1)

    @pl.when(j == 0)
    def _():
        xb_ref[...] = x_ref[...].astype(BF16)

    def z():
        return _dot(xb_ref[...], w_ref[...])

    @pl.when((j == 0) | (j == 3))
    def _():
        o_ref[...] = jax.nn.gelu(z()).astype(o_ref.dtype)

    @pl.when(j == 1)
    def _():
        o_ref[...] = _layer_norm(jax.nn.gelu(z()), lg_ref[...], lb_ref[...]).astype(o_ref.dtype)

    @pl.when(j == 2)
    def _():
        rx_ref[...] = z()

    @pl.when(j == 4)
    def _():
        o_ref[...] = z().astype(o_ref.dtype)

    @pl.when(j >= GATE_BLOCK0)
    def _():
        o_ref[...] = jax.nn.sigmoid(z() + gb_ref[...]).astype(o_ref.dtype)


def _inproj(x, w_in, gate_b, gln_g, gln_b, *, tm, z_dtype):
    m = x.shape[0]
    return pl.pallas_call(
        _inproj_kernel,
        grid=(m // tm, IN_BLOCKS),
        in_specs=[
            pl.BlockSpec((tm, D_MODEL), lambda i, j: (i, 0)),
            pl.BlockSpec((D_MODEL, BR_W), lambda i, j: (0, j)),
            pl.BlockSpec((1, BR_W), lambda i, j: (0, jnp.maximum(j - GATE_BLOCK0, 0))),
            pl.BlockSpec((1, BR_W), lambda i, j: (0, 0)),
            pl.BlockSpec((1, BR_W), lambda i, j: (0, 0)),
        ],
        out_specs=[pl.BlockSpec((tm, BR_W), lambda i, j: (i, jnp.where(j < RX_BLOCK, j, j - 1))),
                   pl.BlockSpec((tm, BR_W), lambda i, j: (i, 0))],
        out_shape=[jax.ShapeDtypeStruct((m, Z_BLOCKS * BR_W), z_dtype),
                   jax.ShapeDtypeStruct((m, BR_W), F32)],
        scratch_shapes=[pltpu.VMEM((tm, D_MODEL), BF16)],
        compiler_params=_params(("arbitrary", "arbitrary"), 48),
        name="in_proj",
    )(x, w_in, gate_b, gln_g, gln_b)


def _softplus(x):
    return jnp.maximum(x, 0.0) + jnp.log1p(jnp.exp(-jnp.abs(x)))


def _lru_coeffs(xc, wa_ref, ba_ref, wx_ref, bx_ref, lam_ref):
    xcb = xc.astype(BF16)
    r_parts, i_parts = [], []
    for k in range(LRU_BLOCKS):
        xk = xcb[:, k * LRU_BLOCK:(k + 1) * LRU_BLOCK]
        r_parts.append(_dot(xk, wa_ref[k]))
        i_parts.append(_dot(xk, wx_ref[k]))
    r = jax.nn.sigmoid(jnp.concatenate(r_parts, axis=1) + ba_ref[...])
    i = jax.nn.sigmoid(jnp.concatenate(i_parts, axis=1) + bx_ref[...])
    log_a = (-LRU_C) * r * _softplus(-lam_ref[...])
    a = jnp.exp(log_a)
    b = jnp.sqrt(-jnp.tanh(log_a) * (a * a + 1.0)) * (i * xc)
    return a, b


def _mixer_kernel(gu_ref, v_ref, rx_ref, grg_ref, q_ref, k_ref, vm_ref,
                  ws_ref, bs_ref, cw_ref, cb_ref, wa_ref, ba_ref, wx_ref, bx_ref, lam_ref,
                  y_ref, conv_ref, hlast_ref,
                  xpad_ref, a_ref, b_ref, h_ref, hc_ref, *, tm):
    t = pl.program_id(1)

    tri = (lax.broadcasted_iota(jnp.int32, (CHUNK, CHUNK), 0)
           >= lax.broadcasted_iota(jnp.int32, (CHUNK, CHUNK), 1))
    for g in range(GMLP_GROUPS):
        wg = jnp.where(tri, ws_ref[g], 0.0).astype(BF16)
        cols = slice(g * GROUP_W, (g + 1) * GROUP_W)
        for c in range(tm // CHUNK):
            rows = slice(c * CHUNK, (c + 1) * CHUNK)
            s = _dot(wg, v_ref[rows, cols].astype(BF16)) + bs_ref[:, cols]
            y_ref[0, rows, cols] = (gu_ref[rows, cols] * s).astype(BF16)

    @pl.when(t == 0)
    def _():
        xpad_ref[0:SUBLANES, :] = jnp.zeros((SUBLANES, BR_W), F32)
        hc_ref[...] = jnp.zeros((1, BR_W), F32)

    @pl.when(t > 0)
    def _():
        xpad_ref[0:SUBLANES, :] = xpad_ref[tm:tm + SUBLANES, :]

    xpad_ref[SUBLANES:SUBLANES + tm, :] = rx_ref[...]
    xc = cb_ref[...]
    for k in range(CONV_W):
        off = SUBLANES - (CONV_W - 1) + k
        xc = xc + xpad_ref[off:off + tm, :] * cw_ref[k:k + 1, :]

    a, b = _lru_coeffs(xc, wa_ref, ba_ref, wx_ref, bx_ref, lam_ref)
    a_ref[...] = a
    b_ref[...] = b

    row = lax.broadcasted_iota(jnp.int32, (SUBLANES, BR_W), 0)
    keep = [row >= d for d in (1, 2, 4)]

    def scan_block(blk, h):
        base = pl.multiple_of(blk * SUBLANES, SUBLANES)
        ca = a_ref[pl.ds(base, SUBLANES), :]
        cb = b_ref[pl.ds(base, SUBLANES), :]
        for d, kp in zip((1, 2, 4), keep):
            a_sh = jnp.where(kp, pltpu.roll(ca, d, 0), 1.0)
            b_sh = jnp.where(kp, pltpu.roll(cb, d, 0), 0.0)
            cb = ca * b_sh + cb
            ca = ca * a_sh
        hh = cb + ca * h
        h_ref[pl.ds(base, SUBLANES), :] = hh
        return hh[SUBLANES - 1:SUBLANES, :]

    h_end = lax.fori_loop(0, tm // SUBLANES, scan_block, hc_ref[...])
    hc_ref[...] = h_end
    y_ref[1] = (grg_ref[...] * h_ref[...]).astype(BF16)

    @pl.when(t == pl.num_programs(1) - 1)
    def _():
        conv_ref[0] = rx_ref[tm - (CONV_W - 1):tm, :]
        hlast_ref[0] = h_end

    scale = XA_HEAD_DIM ** -0.5
    for hd in range(XA_HEADS):
        cols = slice(hd * XA_HEAD_DIM, (hd + 1) * XA_HEAD_DIM)
        qh = q_ref[:, cols].astype(BF16)
        kh = k_ref[0, :, cols].astype(BF16)
        vh = vm_ref[0, :, cols].astype(BF16)
        s = lax.dot_general(qh, kh, (((1,), (1,)), ((), ())), preferred_element_type=F32) * scale
        e = jnp.exp(s - jnp.max(s, axis=-1, keepdims=True))
        p = e / jnp.sum(e, axis=-1, keepdims=True)
        y_ref[2, :, cols] = _dot(p.astype(BF16), vh).astype(BF16)


def _mixer_prompt(z, rx, k_mem, v_mem, ws, bs_mat, conv_w, conv_b, wa, ba, wx, bx, lam, *, bsz, seq, tm=256):
    nt = seq // tm
    zspec = lambda c: pl.BlockSpec((tm, BR_W), lambda b, t, c=c: (b * nt + t, c))
    full = lambda shape: pl.BlockSpec(shape, lambda b, t: (0,) * len(shape))
    return pl.pallas_call(
        functools.partial(_mixer_kernel, tm=tm),
        grid=(bsz, nt),
        in_specs=[zspec(Z_GU), zspec(Z_V), zspec(0), zspec(Z_GRG), zspec(Z_Q),
                  pl.BlockSpec((1, N_MEM, BR_W), lambda b, t: (b, 0, 0)),
                  pl.BlockSpec((1, N_MEM, BR_W), lambda b, t: (b, 0, 0)),
                  full((GMLP_GROUPS, CHUNK, CHUNK)), full((CHUNK, BR_W)),
                  full((CONV_W, BR_W)), full((1, BR_W)),
                  full((LRU_BLOCKS, LRU_BLOCK, LRU_BLOCK)), full((1, BR_W)),
                  full((LRU_BLOCKS, LRU_BLOCK, LRU_BLOCK)), full((1, BR_W)),
                  full((1, BR_W))],
        out_specs=[pl.BlockSpec((3, tm, BR_W), lambda b, t: (0, b * nt + t, 0)),
                   pl.BlockSpec((1, CONV_W - 1, BR_W), lambda b, t: (b, 0, 0)),
                   pl.BlockSpec((1, 1, BR_W), lambda b, t: (b, 0, 0))],
        out_shape=[jax.ShapeDtypeStruct((3, bsz * seq, BR_W), BF16),
                   jax.ShapeDtypeStruct((bsz, CONV_W - 1, BR_W), F32),
                   jax.ShapeDtypeStruct((bsz, 1, BR_W), F32)],
        scratch_shapes=[pltpu.VMEM((tm + 2 * SUBLANES, BR_W), F32),
                        pltpu.VMEM((tm, BR_W), F32), pltpu.VMEM((tm, BR_W), F32),
                        pltpu.VMEM((tm, BR_W), F32), pltpu.VMEM((1, BR_W), F32)],
        compiler_params=_params(("arbitrary", "arbitrary"), 40),
        name="mixer_prompt",
    )(z, z, rx, z, z, k_mem, v_mem, ws, bs_mat, conv_w, conv_b, wa, ba, wx, bx, lam)


def _mixer_sample_kernel(gu_ref, v_ref, rx_ref, grg_ref, cs_ref, h0_ref,
                         wv_ref, bv_ref, cw_ref, cb_ref, wa_ref, ba_ref, wx_ref, bx_ref, lam_ref,
                         y_ref, conv_ref, h_ref):
    y_ref[0] = (gu_ref[...] * (wv_ref[...] * v_ref[...] + bv_ref[...])).astype(BF16)

    rx = rx_ref[...]
    xc = cb_ref[...] + rx * cw_ref[CONV_W - 1:CONV_W, :]
    for k in range(CONV_W - 1):
        xc = xc + cs_ref[k] * cw_ref[k:k + 1, :]
    for k in range(CONV_W - 2):
        conv_ref[k] = cs_ref[k + 1]
    conv_ref[CONV_W - 2] = rx

    a, b = _lru_coeffs(xc, wa_ref, ba_ref, wx_ref, bx_ref, lam_ref)
    h = a * h0_ref[...] + b
    h_ref[...] = h
    y_ref[1] = (grg_ref[...] * h).astype(BF16)


def _mixer_sample(z, rx, conv_state, h0, wvec, bvec, conv_w, conv_b, wa, ba, wx, bx, lam):
    n = z.shape[0]
    zspec = lambda c: pl.BlockSpec((n, BR_W), lambda i, c=c: (0, c))
    full = lambda shape: pl.BlockSpec(shape, lambda i: (0,) * len(shape))
    return pl.pallas_call(
        _mixer_sample_kernel,
        grid=(1,),
        in_specs=[zspec(Z_GU), zspec(Z_V), zspec(0), zspec(Z_GRG),
                  full((CONV_W - 1, n, BR_W)), full((n, BR_W)),
                  full((1, BR_W)), full((1, BR_W)), full((CONV_W, BR_W)), full((1, BR_W)),
                  full((LRU_BLOCKS, LRU_BLOCK, LRU_BLOCK)), full((1, BR_W)),
                  full((LRU_BLOCKS, LRU_BLOCK, LRU_BLOCK)), full((1, BR_W)),
                  full((1, BR_W))],
        out_specs=[full((2, n, BR_W)), full((CONV_W - 1, n, BR_W)), full((n, BR_W))],
        out_shape=[jax.ShapeDtypeStruct((2, n, BR_W), BF16),
                   jax.ShapeDtypeStruct((CONV_W - 1, n, BR_W), F32),
                   jax.ShapeDtypeStruct((n, BR_W), F32)],
        compiler_params=_params(("arbitrary",), 32),
        name="mixer_sample",
    )(z, z, rx, z, conv_state, h0, wvec, bvec, conv_w, conv_b, wa, ba, wx, bx, lam)


def _xattn_sample_kernel(q_ref, k_ref, v_ref, o_ref, *, tb):
    scale = XA_HEAD_DIM ** -0.5

    def one(b, carry):
        q = q_ref[b]
        s = jnp.sum(k_ref[b] * q[None], axis=-1, keepdims=True) * scale
        e = jnp.exp(s - jnp.max(s, axis=0, keepdims=True))
        p = e / jnp.sum(e, axis=0, keepdims=True)
        o_ref[b] = jnp.sum(p * v_ref[b], axis=0).astype(BF16)
        return carry

    lax.fori_loop(0, tb, one, 0)


def _xattn_sample(q, k_cache, v_cache, *, tb=2):
    n = q.shape[0]
    kv_spec = pl.BlockSpec((None, tb, N_MEM, XA_HEADS, XA_HEAD_DIM), lambda i: (0, i, 0, 0, 0))
    return pl.pallas_call(
        functools.partial(_xattn_sample_kernel, tb=tb),
        grid=(n // tb,),
        in_specs=[pl.BlockSpec((tb, XA_HEADS, XA_HEAD_DIM), lambda i: (i, 0, 0)), kv_spec, kv_spec],
        out_specs=pl.BlockSpec((tb, XA_HEADS, XA_HEAD_DIM), lambda i: (i, 0, 0)),
        out_shape=jax.ShapeDtypeStruct((n, XA_HEADS, XA_HEAD_DIM), BF16),
        compiler_params=_params(("arbitrary",), 40),
        name="xattn_sample",
    )(q, k_cache, v_cache)


N_BRANCH = 3


def _merge_kernel(y_ref, wb_ref, g0_ref, g1_ref, g2_ref, o_ref):
    acc = None
    for k, gate_ref in enumerate((g0_ref, g1_ref, g2_ref)):
        term = gate_ref[...].astype(F32) * _dot(y_ref[k], wb_ref[k])
        acc = term if acc is None else acc + term
    o_ref[...] = acc.astype(BF16)


def _merge(y, w_branch, z, *, tm):
    m = y.shape[1]
    gate_spec = lambda k: pl.BlockSpec((tm, BR_W), lambda i, n, k=k: (i, Z_GATE0 + 2 * k + n))
    return pl.pallas_call(
        _merge_kernel,
        grid=(m // tm, D_MODEL // BR_W),
        in_specs=[
            pl.BlockSpec((N_BRANCH, tm, BR_W), lambda i, n: (0, i, 0)),
            pl.BlockSpec((N_BRANCH, BR_W, BR_W), lambda i, n: (0, 0, n)),
            gate_spec(0), gate_spec(1), gate_spec(2),
        ],
        out_specs=pl.BlockSpec((tm, BR_W), lambda i, n: (i, n)),
        out_shape=jax.ShapeDtypeStruct((m, D_MODEL), BF16),
        compiler_params=_params(("arbitrary", "arbitrary"), 56),
        name="merge",
    )(y, w_branch, z, z, z)


def _outproj_kernel(m_ref, wo_ref, x_ref, g_ref, b_ref, o_ref, *, row_splits):
    rows_per = m_ref.shape[0] // row_splits
    for r in range(row_splits):
        rows = slice(r * rows_per, (r + 1) * rows_per)
        y = ALPHA * x_ref[rows, :] + _dot(m_ref[rows, :], wo_ref[...])
        o_ref[rows, :] = _layer_norm(y, g_ref[...], b_ref[...])


def _outproj(merged, w_out, x, ln_g, ln_b, *, tm):
    m = x.shape[0]
    return pl.pallas_call(
        functools.partial(_outproj_kernel, row_splits=2 if tm >= 512 else 1),
        grid=(m // tm,),
        in_specs=[
            pl.BlockSpec((tm, D_MODEL), lambda i: (i, 0)),
            pl.BlockSpec((D_MODEL, D_MODEL), lambda i: (0, 0)),
            pl.BlockSpec((tm, D_MODEL), lambda i: (i, 0)),
            pl.BlockSpec((1, D_MODEL), lambda i: (0, 0)),
            pl.BlockSpec((1, D_MODEL), lambda i: (0, 0)),
        ],
        out_specs=pl.BlockSpec((tm, D_MODEL), lambda i: (i, 0)),
        out_shape=jax.ShapeDtypeStruct((m, D_MODEL), F32),
        compiler_params=_params(("arbitrary",), 48),
        name="out_proj",
    )(merged, w_out, x, ln_g, ln_b)


def kernel(x_prompt, x_sample, mem_prompt, cache_mem_k, cache_mem_v, state_conv, state_lru_h, ffn1_w_gu, ffn1_w_down, ln1_g, ln1_b, w_in, gate_b, gmlp_ln_g, gmlp_ln_b, gmlp_w_s, gmlp_b_s, conv_w, conv_b, lru_w_a, lru_b_a, lru_w_x, lru_b_x, lru_lambda, mem_ln_g, mem_ln_b, w_mem_kv, w_branch, w_out, ln2_g, ln2_b, ffn2_w_gu, ffn2_w_down, ln3_g, ln3_b):
    bsz, seq, _ = x_prompt.shape
    n_s = x_sample.shape[0]
    l = 0

    w1gu, w1d = ffn1_w_gu[l].astype(BF16), ffn1_w_down[l].astype(BF16)
    w2gu, w2d = ffn2_w_gu[l].astype(BF16), ffn2_w_down[l].astype(BF16)
    win, wkv = w_in[l].astype(BF16), w_mem_kv[l].astype(BF16)
    wbr, wout = w_branch[l].astype(BF16), w_out[l].astype(BF16)
    wa, wx = lru_w_a[l].astype(BF16), lru_w_x[l].astype(BF16)

    row = lambda p: p[l].reshape(1, -1)
    gb = gate_b[l].reshape(1, -1)
    bs_mat = jnp.repeat(gmlp_b_s[l].T, GROUP_W, axis=1)
    wvec = jnp.repeat(gmlp_w_s[l][:, 0, 0], GROUP_W).reshape(1, -1)
    bvec = jnp.repeat(gmlp_b_s[l][:, 0], GROUP_W).reshape(1, -1)
    lru = (wa, row(lru_b_a), wx, row(lru_b_x), row(lru_lambda))

    xp = x_prompt.reshape(bsz * seq, D_MODEL)
    xs = x_sample.reshape(n_s, D_MODEL)

    k_p, v_p = _memkv(mem_prompt.reshape(bsz * N_MEM, D_MODEL), wkv, row(mem_ln_g), row(mem_ln_b))
    x1p = _ffn(xp, w1gu, w1d, row(ln1_g), row(ln1_b), tm=512)
    zp, rxp = _inproj(x1p, win, gb, row(gmlp_ln_g), row(gmlp_ln_b), tm=1024, z_dtype=BF16)
    yp, conv_p, hlast_p = _mixer_prompt(
        zp, rxp, k_p.reshape(bsz, N_MEM, BR_W), v_p.reshape(bsz, N_MEM, BR_W), gmlp_w_s[l], bs_mat,
        conv_w[l], row(conv_b), *lru, bsz=bsz, seq=seq)
    mp = _merge(yp, wbr, zp, tm=1024)
    x2p = _outproj(mp, wout, x1p, row(ln2_g), row(ln2_b), tm=512)
    x3p = _ffn(x2p, w2gu, w2d, row(ln3_g), row(ln3_b), tm=512)

    x1s = _ffn(xs, w1gu, w1d, row(ln1_g), row(ln1_b), tm=n_s)
    zs, rxs = _inproj(x1s, win, gb, row(gmlp_ln_g), row(gmlp_ln_b), tm=n_s, z_dtype=F32)
    y01, conv_s, h_s = _mixer_sample(
        zs, rxs, jnp.swapaxes(state_conv[l], 0, 1), state_lru_h[l], wvec, bvec, conv_w[l], row(conv_b), *lru)
    q_s = zs[:, Z_Q * BR_W:(Z_Q + 1) * BR_W].reshape(n_s, XA_HEADS, XA_HEAD_DIM)
    yxa = _xattn_sample(q_s, cache_mem_k, cache_mem_v)
    ys = jnp.concatenate([y01, yxa.reshape(1, n_s, BR_W)], axis=0)
    ms = _merge(ys, wbr, zs, tm=n_s)
    x2s = _outproj(ms, wout, x1s, row(ln2_g), row(ln2_b), tm=n_s)
    x3s = _ffn(x2s, w2gu, w2d, row(ln3_g), row(ln3_b), tm=n_s)

    kv_shape = (1, bsz, N_MEM, XA_HEADS, XA_HEAD_DIM)
    return (x3p.reshape(bsz, seq, D_MODEL),
            x3s.reshape(n_s, 1, D_MODEL),
            k_p.reshape(kv_shape),
            v_p.reshape(kv_shape),
            conv_p[None],
            hlast_p.reshape(1, bsz, BR_W),
            jnp.swapaxes(conv_s, 0, 1)[None],
            h_s[None],
            zs[:, Z_V * BR_W:(Z_V + 1) * BR_W].reshape(1, n_s, 1, BR_W))
```

```python
import functools

import jax
import jax.numpy as jnp
from jax import lax
from jax.experimental import pallas as pl
from jax.experimental.pallas import tpu as pltpu

F32 = jnp.float32
BF16 = jnp.bfloat16

D_MODEL = 2048
BR_W = 1024
D_FF = 5632
N_MEM = 256
XA_HEADS = 4
XA_HEAD_DIM = 256
GMLP_GROUPS = 4
GROUP_W = BR_W // GMLP_GROUPS
CHUNK = 128
LRU_BLOCKS = 8
LRU_BLOCK = 128
CONV_W = 4
LRU_C = 8.0
LN_EPS = 1e-5
ALPHA = 2.0 ** 0.25
IN_BLOCKS = 11
RX_BLOCK = 2
GATE_BLOCK0 = 5
Z_BLOCKS = IN_BLOCKS - 1
Z_GU, Z_V, Z_GRG, Z_Q, Z_GATE0 = 0, 1, 2, 3, 4
SUBLANES = 8
MiB = 1024 * 1024


def _params(semantics, vmem_mib):
    return pltpu.CompilerParams(dimension_semantics=semantics, vmem_limit_bytes=vmem_mib * MiB)


def _layer_norm(x, g, b):
    mu = jnp.mean(x, axis=-1, keepdims=True)
    xc = x - mu
    var = jnp.mean(xc * xc, axis=-1, keepdims=True)
    return xc * lax.rsqrt(var + LN_EPS) * g + b


def _dot(a, b):
    return jnp.dot(a, b, preferred_element_type=F32)


def _mxu_operand(w_ref, w16_ref=None):
    w = w_ref[...].astype(BF16)
    if w16_ref is not None:
        w16_ref[...] = w
    return w


def _ffn_kernel(x_ref, wg_ref, wu_ref, wd_ref, g_ref, b_ref, o_ref, *rest, emit_bf16):
    if emit_bf16:
        wg16_ref, wu16_ref, wd16_ref, xb_ref, acc_ref = rest
    else:
        wg16_ref = wu16_ref = wd16_ref = None
        xb_ref, acc_ref = rest
    j = pl.program_id(1)

    @pl.when(j == 0)
    def _():
        xb_ref[...] = x_ref[...].astype(BF16)
        acc_ref[...] = jnp.zeros_like(acc_ref)

    xb = xb_ref[...]
    g = _dot(xb, _mxu_operand(wg_ref, wg16_ref))
    u = _dot(xb, _mxu_operand(wu_ref, wu16_ref))
    h = (g * jax.nn.sigmoid(g) * u).astype(BF16)
    acc_ref[...] += _dot(h, _mxu_operand(wd_ref, wd16_ref))

    @pl.when(j == pl.num_programs(1) - 1)
    def _():
        y = ALPHA * x_ref[...] + 0.5 * acc_ref[...]
        o_ref[...] = _layer_norm(y, g_ref[...], b_ref[...])


def _ffn(x, w_g, w_u, w_down, ln_g, ln_b, *, tm, emit_bf16=False, tf=512):
    m = x.shape[0]
    nf = D_FF // tf
    u_off = nf if w_u.shape[1] == 2 * D_FF else 0
    assert not emit_bf16 or m == tm
    out_specs = [pl.BlockSpec((tm, D_MODEL), lambda i, j: (i, 0))]
    out_shape = [jax.ShapeDtypeStruct((m, D_MODEL), F32)]
    if emit_bf16:
        out_specs += [pl.BlockSpec((D_MODEL, tf), lambda i, j: (0, j)),
                      pl.BlockSpec((D_MODEL, tf), lambda i, j: (0, j)),
                      pl.BlockSpec((tf, D_MODEL), lambda i, j: (j, 0))]
        out_shape += [jax.ShapeDtypeStruct((D_MODEL, D_FF), BF16),
                      jax.ShapeDtypeStruct((D_MODEL, D_FF), BF16),
                      jax.ShapeDtypeStruct((D_FF, D_MODEL), BF16)]
    outs = pl.pallas_call(
        functools.partial(_ffn_kernel, emit_bf16=emit_bf16),
        grid=(m // tm, nf),
        in_specs=[
            pl.BlockSpec((tm, D_MODEL), lambda i, j: (i, 0)),
            pl.BlockSpec((D_MODEL, tf), lambda i, j: (0, j)),
            pl.BlockSpec((D_MODEL, tf), lambda i, j: (0, j + u_off)),
            pl.BlockSpec((tf, D_MODEL), lambda i, j: (j, 0)),
            pl.BlockSpec((1, D_MODEL), lambda i, j: (0, 0)),
            pl.BlockSpec((1, D_MODEL), lambda i, j: (0, 0)),
        ],
        out_specs=out_specs,
        out_shape=out_shape,
        scratch_shapes=[pltpu.VMEM((tm, D_MODEL), BF16), pltpu.VMEM((tm, D_MODEL), F32)],
        compiler_params=_params(("arbitrary", "arbitrary"), 48),
        name="ffn_ln",
    )(x, w_g, w_u, w_down, ln_g, ln_b)
    return outs if emit_bf16 else outs[0]


def _memkv_kernel(x_ref, wk_ref, wv_ref, g_ref, b_ref, k_ref, v_ref, xb_ref):
    @pl.when(pl.program_id(1) == 0)
    def _():
        xb_ref[...] = _layer_norm(x_ref[...], g_ref[...], b_ref[...]).astype(BF16)

    xb = xb_ref[...]
    k_ref[...] = _dot(xb, _mxu_operand(wk_ref))
    v_ref[...] = _dot(xb, _mxu_operand(wv_ref))


def _memkv(mem, w_kv, ln_g, ln_b, *, tm=1024, tn=512):
    m = mem.shape[0]
    nn = BR_W // tn
    return pl.pallas_call(
        _memkv_kernel,
        grid=(m // tm, nn),
        in_specs=[
            pl.BlockSpec((tm, D_MODEL), lambda i, j: (i, 0)),
            pl.BlockSpec((D_MODEL, tn), lambda i, j: (0, j)),
            pl.BlockSpec((D_MODEL, tn), lambda i, j: (0, j + nn)),
            pl.BlockSpec((1, D_MODEL), lambda i, j: (0, 0)),
            pl.BlockSpec((1, D_MODEL), lambda i, j: (0, 0)),
        ],
        out_specs=[pl.BlockSpec((tm, tn), lambda i, j: (i, j)),
                   pl.BlockSpec((tm, tn), lambda i, j: (i, j))],
        out_shape=[jax.ShapeDtypeStruct((m, BR_W), F32), jax.ShapeDtypeStruct((m, BR_W), F32)],
        scratch_shapes=[pltpu.VMEM((tm, D_MODEL), BF16)],
        compiler_params=_params(("arbitrary", "arbitrary"), 48),
        name="mem_kv",
    )(mem, w_kv, w_kv, ln_g, ln_b)


def _inproj_kernel(x_ref, w_ref, gb_ref, lg_ref, lb_ref, o_ref, rx_ref, *rest, emit_bf16):
    if emit_bf16:
        w16_ref, xb_ref = rest
        _mxu_operand(w_ref, w16_ref)
        w_ref = w16_ref
    else:
        xb_ref, = rest
    j = pl.program_id(1)

    @pl.when(j == 0)
    def _():
        xb_ref[...] = x_ref[...].astype(BF16)

    def z():
        return _dot(xb_ref[...], w_ref[...])

    @pl.when((j == 0) | (j == 3))
    def _():
        o_ref[...] = jax.nn.gelu(z()).astype(o_ref.dtype)

    @pl.when(j == 1)
    def _():
        o_ref[...] = _layer_norm(jax.nn.gelu(z()), lg_ref[...], lb_ref[...]).astype(o_ref.dtype)

    @pl.when(j == 2)
    def _():
        rx_ref[...] = z()

    @pl.when(j == 4)
    def _():
        o_ref[...] = z().astype(o_ref.dtype)

    @pl.when(j >= GATE_BLOCK0)
    def _():
        o_ref[...] = jax.nn.sigmoid(z() + gb_ref[...]).astype(o_ref.dtype)


def _inproj(x, w_in, gate_b, gln_g, gln_b, *, tm, z_dtype, emit_bf16=False):
    m = x.shape[0]
    assert not emit_bf16 or m == tm
    out_specs = [pl.BlockSpec((tm, BR_W), lambda i, j: (i, jnp.where(j < RX_BLOCK, j, j - 1))),
                 pl.BlockSpec((tm, BR_W), lambda i, j: (i, 0))]
    out_shape = [jax.ShapeDtypeStruct((m, Z_BLOCKS * BR_W), z_dtype),
                 jax.ShapeDtypeStruct((m, BR_W), F32)]
    if emit_bf16:
        out_specs.append(pl.BlockSpec((D_MODEL, BR_W), lambda i, j: (0, j)))
        out_shape.append(jax.ShapeDtypeStruct((D_MODEL, IN_BLOCKS * BR_W), BF16))
    return pl.pallas_call(
        functools.partial(_inproj_kernel, emit_bf16=emit_bf16),
        grid=(m // tm, IN_BLOCKS),
        in_specs=[
            pl.BlockSpec((tm, D_MODEL), lambda i, j: (i, 0)),
            pl.BlockSpec((D_MODEL, BR_W), lambda i, j: (0, j)),
            pl.BlockSpec((1, BR_W), lambda i, j: (0, jnp.maximum(j - GATE_BLOCK0, 0))),
            pl.BlockSpec((1, BR_W), lambda i, j: (0, 0)),
            pl.BlockSpec((1, BR_W), lambda i, j: (0, 0)),
        ],
        out_specs=out_specs,
        out_shape=out_shape,
        scratch_shapes=[pltpu.VMEM((tm, D_MODEL), BF16)],
        compiler_params=_params(("arbitrary", "arbitrary"), 48),
        name="in_proj",
    )(x, w_in, gate_b, gln_g, gln_b)


def _softplus(x):
    return jnp.maximum(x, 0.0) + jnp.log1p(jnp.exp(-jnp.abs(x)))


def _lru_coeffs(xc, wa_ref, ba_ref, wx_ref, bx_ref, lam_ref):
    xcb = xc.astype(BF16)
    r_parts, i_parts = [], []
    for k in range(LRU_BLOCKS):
        xk = xcb[:, k * LRU_BLOCK:(k + 1) * LRU_BLOCK]
        r_parts.append(_dot(xk, wa_ref[k]))
        i_parts.append(_dot(xk, wx_ref[k]))
    r = jax.nn.sigmoid(jnp.concatenate(r_parts, axis=1) + ba_ref[...])
    i = jax.nn.sigmoid(jnp.concatenate(i_parts, axis=1) + bx_ref[...])
    log_a = (-LRU_C) * r * _softplus(-lam_ref[...])
    a = jnp.exp(log_a)
    b = jnp.sqrt(-jnp.tanh(log_a) * (a * a + 1.0)) * (i * xc)
    return a, b


def _mixer_kernel(gu_ref, v_ref, rx_ref, grg_ref, q_ref, k_ref, vm_ref,
                  ws_ref, bs_ref, cw_ref, cb_ref, wa_ref, ba_ref, wx_ref, bx_ref, lam_ref,
                  y_ref, conv_ref, hlast_ref,
                  xpad_ref, a_ref, b_ref, h_ref, hc_ref, *, tm):
    t = pl.program_id(1)

    tri = (lax.broadcasted_iota(jnp.int32, (CHUNK, CHUNK), 0)
           >= lax.broadcasted_iota(jnp.int32, (CHUNK, CHUNK), 1))
    for g in range(GMLP_GROUPS):
        wg = jnp.where(tri, ws_ref[g], 0.0).astype(BF16)
        cols = slice(g * GROUP_W, (g + 1) * GROUP_W)
        for c in range(tm // CHUNK):
            rows = slice(c * CHUNK, (c + 1) * CHUNK)
            s = _dot(wg, v_ref[rows, cols].astype(BF16)) + bs_ref[:, cols]
            y_ref[0, rows, cols] = (gu_ref[rows, cols] * s).astype(BF16)

    @pl.when(t == 0)
    def _():
        xpad_ref[0:SUBLANES, :] = jnp.zeros((SUBLANES, BR_W), F32)
        hc_ref[...] = jnp.zeros((1, BR_W), F32)

    @pl.when(t > 0)
    def _():
        xpad_ref[0:SUBLANES, :] = xpad_ref[tm:tm + SUBLANES, :]

    xpad_ref[SUBLANES:SUBLANES + tm, :] = rx_ref[...]
    xc = cb_ref[...]
    for k in range(CONV_W):
        off = SUBLANES - (CONV_W - 1) + k
        xc = xc + xpad_ref[off:off + tm, :] * cw_ref[k:k + 1, :]

    a, b = _lru_coeffs(xc, wa_ref, ba_ref, wx_ref, bx_ref, lam_ref)
    a_ref[...] = a
    b_ref[...] = b

    row = lax.broadcasted_iota(jnp.int32, (SUBLANES, BR_W), 0)
    keep = [row >= d for d in (1, 2, 4)]

    def scan_block(blk, h):
        base = pl.multiple_of(blk * SUBLANES, SUBLANES)
        ca = a_ref[pl.ds(base, SUBLANES), :]
        cb = b_ref[pl.ds(base, SUBLANES), :]
        for d, kp in zip((1, 2, 4), keep):
            a_sh = jnp.where(kp, pltpu.roll(ca, d, 0), 1.0)
            b_sh = jnp.where(kp, pltpu.roll(cb, d, 0), 0.0)
            cb = ca * b_sh + cb
            ca = ca * a_sh
        hh = cb + ca * h
        h_ref[pl.ds(base, SUBLANES), :] = hh
        return hh[SUBLANES - 1:SUBLANES, :]

    h_end = lax.fori_loop(0, tm // SUBLANES, scan_block, hc_ref[...])
    hc_ref[...] = h_end
    y_ref[1] = (grg_ref[...] * h_ref[...]).astype(BF16)

    @pl.when(t == pl.num_programs(1) - 1)
    def _():
        conv_ref[0] = rx_ref[tm - (CONV_W - 1):tm, :]
        hlast_ref[0] = h_end

    scale = XA_HEAD_DIM ** -0.5
    for hd in range(XA_HEADS):
        cols = slice(hd * XA_HEAD_DIM, (hd + 1) * XA_HEAD_DIM)
        qh = q_ref[:, cols].astype(BF16)
        kh = k_ref[0, :, cols].astype(BF16)
        vh = vm_ref[0, :, cols].astype(BF16)
        s = lax.dot_general(qh, kh, (((1,), (1,)), ((), ())), preferred_element_type=F32) * scale
        e = jnp.exp(s - jnp.max(s, axis=-1, keepdims=True))
        p = e / jnp.sum(e, axis=-1, keepdims=True)
        y_ref[2, :, cols] = _dot(p.astype(BF16), vh).astype(BF16)


def _mixer_prompt(z, rx, k_mem, v_mem, ws, bs_mat, conv_w, conv_b, wa, ba, wx, bx, lam, *, bsz, seq, tm=256):
    nt = seq // tm
    zspec = lambda c: pl.BlockSpec((tm, BR_W), lambda b, t, c=c: (b * nt + t, c))
    full = lambda shape: pl.BlockSpec(shape, lambda b, t: (0,) * len(shape))
    return pl.pallas_call(
        functools.partial(_mixer_kernel, tm=tm),
        grid=(bsz, nt),
        in_specs=[zspec(Z_GU), zspec(Z_V), zspec(0), zspec(Z_GRG), zspec(Z_Q),
                  pl.BlockSpec((1, N_MEM, BR_W), lambda b, t: (b, 0, 0)),
                  pl.BlockSpec((1, N_MEM, BR_W), lambda b, t: (b, 0, 0)),
                  full((GMLP_GROUPS, CHUNK, CHUNK)), full((CHUNK, BR_W)),
                  full((CONV_W, BR_W)), full((1, BR_W)),
                  full((LRU_BLOCKS, LRU_BLOCK, LRU_BLOCK)), full((1, BR_W)),
                  full((LRU_BLOCKS, LRU_BLOCK, LRU_BLOCK)), full((1, BR_W)),
                  full((1, BR_W))],
        out_specs=[pl.BlockSpec((3, tm, BR_W), lambda b, t: (0, b * nt + t, 0)),
                   pl.BlockSpec((1, CONV_W - 1, BR_W), lambda b, t: (b, 0, 0)),
                   pl.BlockSpec((1, 1, BR_W), lambda b, t: (b, 0, 0))],
        out_shape=[jax.ShapeDtypeStruct((3, bsz * seq, BR_W), BF16),
                   jax.ShapeDtypeStruct((bsz, CONV_W - 1, BR_W), F32),
                   jax.ShapeDtypeStruct((bsz, 1, BR_W), F32)],
        scratch_shapes=[pltpu.VMEM((tm + SUBLANES, BR_W), F32),
                        pltpu.VMEM((tm, BR_W), F32), pltpu.VMEM((tm, BR_W), F32),
                        pltpu.VMEM((tm, BR_W), F32), pltpu.VMEM((1, BR_W), F32)],
        compiler_params=_params(("arbitrary", "arbitrary"), 40),
        name="mixer_prompt",
    )(z, z, rx, z, z, k_mem, v_mem, ws, bs_mat, conv_w, conv_b, wa, ba, wx, bx, lam)


def _mixer_sample_kernel(gu_ref, v_ref, rx_ref, grg_ref, cs_ref, h0_ref,
                         wv_ref, bv_ref, cw_ref, cb_ref, wa_ref, ba_ref, wx_ref, bx_ref, lam_ref,
                         y_ref, conv_ref, h_ref):
    y_ref[0] = (gu_ref[...] * (wv_ref[...] * v_ref[...] + bv_ref[...])).astype(BF16)

    rx = rx_ref[...]
    xc = cb_ref[...] + rx * cw_ref[CONV_W - 1:CONV_W, :]
    for k in range(CONV_W - 1):
        xc = xc + cs_ref[k] * cw_ref[k:k + 1, :]
    for k in range(CONV_W - 2):
        conv_ref[k] = cs_ref[k + 1]
    conv_ref[CONV_W - 2] = rx

    a, b = _lru_coeffs(xc, wa_ref, ba_ref, wx_ref, bx_ref, lam_ref)
    h = a * h0_ref[...] + b
    h_ref[...] = h
    y_ref[1] = (grg_ref[...] * h).astype(BF16)


def _mixer_sample(z, rx, conv_state, h0, wvec, bvec, conv_w, conv_b, wa, ba, wx, bx, lam):
    n = z.shape[0]
    zspec = lambda c: pl.BlockSpec((n, BR_W), lambda i, c=c: (0, c))
    full = lambda shape: pl.BlockSpec(shape, lambda i: (0,) * len(shape))
    return pl.pallas_call(
        _mixer_sample_kernel,
        grid=(1,),
        in_specs=[zspec(Z_GU), zspec(Z_V), zspec(0), zspec(Z_GRG),
                  full((CONV_W - 1, n, BR_W)), full((n, BR_W)),
                  full((1, BR_W)), full((1, BR_W)), full((CONV_W, BR_W)), full((1, BR_W)),
                  full((LRU_BLOCKS, LRU_BLOCK, LRU_BLOCK)), full((1, BR_W)),
                  full((LRU_BLOCKS, LRU_BLOCK, LRU_BLOCK)), full((1, BR_W)),
                  full((1, BR_W))],
        out_specs=[full((2, n, BR_W)), full((CONV_W - 1, n, BR_W)), full((n, BR_W))],
        out_shape=[jax.ShapeDtypeStruct((2, n, BR_W), BF16),
                   jax.ShapeDtypeStruct((CONV_W - 1, n, BR_W), F32),
                   jax.ShapeDtypeStruct((n, BR_W), F32)],
        compiler_params=_params(("arbitrary",), 32),
        name="mixer_sample",
    )(z, z, rx, z, conv_state, h0, wvec, bvec, conv_w, conv_b, wa, ba, wx, bx, lam)


def _xattn_sample_kernel(q_ref, k_ref, v_ref, o_ref, *, tb):
    scale = XA_HEAD_DIM ** -0.5

    def one(b, carry):
        q = q_ref[b]
        s = jnp.sum(k_ref[b] * q[None], axis=-1, keepdims=True) * scale
        e = jnp.exp(s - jnp.max(s, axis=0, keepdims=True))
        p = e / jnp.sum(e, axis=0, keepdims=True)
        o_ref[b] = jnp.sum(p * v_ref[b], axis=0).astype(BF16)
        return carry

    lax.fori_loop(0, tb, one, 0)


def _xattn_sample(q, k_cache, v_cache, *, tb=2):
    n = q.shape[0]
    kv_spec = pl.BlockSpec((None, tb, N_MEM, XA_HEADS, XA_HEAD_DIM), lambda i: (0, i, 0, 0, 0))
    return pl.pallas_call(
        functools.partial(_xattn_sample_kernel, tb=tb),
        grid=(n // tb,),
        in_specs=[pl.BlockSpec((tb, XA_HEADS, XA_HEAD_DIM), lambda i: (i, 0, 0)), kv_spec, kv_spec],
        out_specs=pl.BlockSpec((tb, XA_HEADS, XA_HEAD_DIM), lambda i: (i, 0, 0)),
        out_shape=jax.ShapeDtypeStruct((n, XA_HEADS, XA_HEAD_DIM), BF16),
        compiler_params=_params(("arbitrary",), 40),
        name="xattn_sample",
    )(q, k_cache, v_cache)


N_BRANCH = 3


def _merge_kernel(y_ref, wb_ref, g0_ref, g1_ref, g2_ref, o_ref, wb16_ref=None):
    acc = None
    for k, gate_ref in enumerate((g0_ref, g1_ref, g2_ref)):
        wk = _mxu_operand(wb_ref.at[k], None if wb16_ref is None else wb16_ref.at[k])
        term = gate_ref[...].astype(F32) * _dot(y_ref[k], wk)
        acc = term if acc is None else acc + term
    o_ref[...] = acc.astype(BF16)


def _merge(y, w_branch, z, *, tm, emit_bf16=False):
    m = y.shape[1]
    assert not emit_bf16 or m == tm
    gate_spec = lambda k: pl.BlockSpec((tm, BR_W), lambda i, n, k=k: (i, Z_GATE0 + 2 * k + n))
    w_spec = pl.BlockSpec((N_BRANCH, BR_W, BR_W), lambda i, n: (0, 0, n))
    out_specs = [pl.BlockSpec((tm, BR_W), lambda i, n: (i, n))]
    out_shape = [jax.ShapeDtypeStruct((m, D_MODEL), BF16)]
    if emit_bf16:
        out_specs.append(w_spec)
        out_shape.append(jax.ShapeDtypeStruct(w_branch.shape, BF16))
    outs = pl.pallas_call(
        _merge_kernel,
        grid=(m // tm, D_MODEL // BR_W),
        in_specs=[
            pl.BlockSpec((N_BRANCH, tm, BR_W), lambda i, n: (0, i, 0)),
            w_spec,
            gate_spec(0), gate_spec(1), gate_spec(2),
        ],
        out_specs=out_specs,
        out_shape=out_shape,
        compiler_params=_params(("arbitrary", "arbitrary"), 56),
        name="merge",
    )(y, w_branch, z, z, z)
    return outs if emit_bf16 else outs[0]


def _outproj_kernel(m_ref, wo_ref, x_ref, g_ref, b_ref, o_ref, *, row_splits):
    rows_per = m_ref.shape[0] // row_splits
    for r in range(row_splits):
        rows = slice(r * rows_per, (r + 1) * rows_per)
        y = ALPHA * x_ref[rows, :] + _dot(m_ref[rows, :], wo_ref[...])
        o_ref[rows, :] = _layer_norm(y, g_ref[...], b_ref[...])


def _outproj(merged, w_out, x, ln_g, ln_b, *, tm):
    m = x.shape[0]
    return pl.pallas_call(
        functools.partial(_outproj_kernel, row_splits=2 if tm >= 512 else 1),
        grid=(m // tm,),
        in_specs=[
            pl.BlockSpec((tm, D_MODEL), lambda i: (i, 0)),
            pl.BlockSpec((D_MODEL, D_MODEL), lambda i: (0, 0)),
            pl.BlockSpec((tm, D_MODEL), lambda i: (i, 0)),
            pl.BlockSpec((1, D_MODEL), lambda i: (0, 0)),
            pl.BlockSpec((1, D_MODEL), lambda i: (0, 0)),
        ],
        out_specs=pl.BlockSpec((tm, D_MODEL), lambda i: (i, 0)),
        out_shape=jax.ShapeDtypeStruct((m, D_MODEL), F32),
        compiler_params=_params(("arbitrary",), 48),
        name="out_proj",
    )(merged, w_out, x, ln_g, ln_b)


def kernel(x_prompt, x_sample, mem_prompt, cache_mem_k, cache_mem_v, state_conv, state_lru_h, ffn1_w_gu, ffn1_w_down, ln1_g, ln1_b, w_in, gate_b, gmlp_ln_g, gmlp_ln_b, gmlp_w_s, gmlp_b_s, conv_w, conv_b, lru_w_a, lru_b_a, lru_w_x, lru_b_x, lru_lambda, mem_ln_g, mem_ln_b, w_mem_kv, w_branch, w_out, ln2_g, ln2_b, ffn2_w_gu, ffn2_w_down, ln3_g, ln3_b):
    bsz, seq, _ = x_prompt.shape
    n_s = x_sample.shape[0]
    l = 0

    wout = w_out[l].astype(BF16)
    wa, wx = lru_w_a[l].astype(BF16), lru_w_x[l].astype(BF16)

    row = lambda p: p[l].reshape(1, -1)
    gb = gate_b[l].reshape(1, -1)
    bs_mat = jnp.repeat(gmlp_b_s[l].T, GROUP_W, axis=1)
    wvec = jnp.repeat(gmlp_w_s[l][:, 0, 0], GROUP_W).reshape(1, -1)
    bvec = jnp.repeat(gmlp_b_s[l][:, 0], GROUP_W).reshape(1, -1)
    lru = (wa, row(lru_b_a), wx, row(lru_b_x), row(lru_lambda))

    xp = x_prompt.reshape(bsz * seq, D_MODEL)
    xs = x_sample.reshape(n_s, D_MODEL)

    x1s, w1g, w1u, w1d = _ffn(xs, ffn1_w_gu[l], ffn1_w_gu[l], ffn1_w_down[l], row(ln1_g), row(ln1_b),
                              tm=n_s, emit_bf16=True)
    zs, rxs, win = _inproj(x1s, w_in[l], gb, row(gmlp_ln_g), row(gmlp_ln_b), tm=n_s, z_dtype=F32,
                           emit_bf16=True)
    y01, conv_s, h_s = _mixer_sample(
        zs, rxs, jnp.swapaxes(state_conv[l], 0, 1), state_lru_h[l], wvec, bvec, conv_w[l], row(conv_b), *lru)
    q_s = zs[:, Z_Q * BR_W:(Z_Q + 1) * BR_W].reshape(n_s, XA_HEADS, XA_HEAD_DIM)
    yxa = _xattn_sample(q_s, cache_mem_k, cache_mem_v)
    ys = jnp.concatenate([y01, yxa.reshape(1, n_s, BR_W)], axis=0)
    ms, wbr = _merge(ys, w_branch[l], zs, tm=n_s, emit_bf16=True)
    x2s = _outproj(ms, wout, x1s, row(ln2_g), row(ln2_b), tm=n_s)
    x3s, w2g, w2u, w2d = _ffn(x2s, ffn2_w_gu[l], ffn2_w_gu[l], ffn2_w_down[l], row(ln3_g), row(ln3_b),
                              tm=n_s, emit_bf16=True)

    k_p, v_p = _memkv(mem_prompt.reshape(bsz * N_MEM, D_MODEL), w_mem_kv[l], row(mem_ln_g), row(mem_ln_b))
    x1p = _ffn(xp, w1g, w1u, w1d, row(ln1_g), row(ln1_b), tm=512)
    zp, rxp = _inproj(x1p, win, gb, row(gmlp_ln_g), row(gmlp_ln_b), tm=1024, z_dtype=BF16)
    yp, conv_p, hlast_p = _mixer_prompt(
        zp, rxp, k_p.reshape(bsz, N_MEM, BR_W), v_p.reshape(bsz, N_MEM, BR_W), gmlp_w_s[l], bs_mat,
        conv_w[l], row(conv_b), *lru, bsz=bsz, seq=seq)
    mp = _merge(yp, wbr, zp, tm=1024)
    x2p = _outproj(mp, wout, x1p, row(ln2_g), row(ln2_b), tm=512)
    x3p = _ffn(x2p, w2g, w2u, w2d, row(ln3_g), row(ln3_b), tm=512)

    kv_shape = (1, bsz, N_MEM, XA_HEADS, XA_HEAD_DIM)
    return (x3p.reshape(bsz, seq, D_MODEL),
            x3s.reshape(n_s, 1, D_MODEL),
            k_p.reshape(kv_shape),
            v_p.reshape(kv_shape),
            conv_p[None],
            hlast_p.reshape(1, bsz, BR_W),
            jnp.swapaxes(conv_s, 0, 1)[None],
            h_s[None],
            zs[:, Z_V * BR_W:(Z_V + 1) * BR_W].reshape(1, n_s, 1, BR_W))
```

```python
import functools

import jax
import jax.numpy as jnp
from jax import lax
from jax.experimental import pallas as pl
from jax.experimental.pallas import tpu as pltpu

F32 = jnp.float32
BF16 = jnp.bfloat16

D_MODEL = 2048
BR_W = 1024
D_FF = 5632
N_MEM = 256
XA_HEADS = 4
XA_HEAD_DIM = 256
GMLP_GROUPS = 4
GROUP_W = BR_W // GMLP_GROUPS
CHUNK = 128
LRU_BLOCKS = 8
LRU_BLOCK = 128
CONV_W = 4
LRU_C = 8.0
LN_EPS = 1e-5
ALPHA = 2.0 ** 0.25
IN_BLOCKS = 11
RX_BLOCK = 2
GATE_BLOCK0 = 5
Z_BLOCKS = IN_BLOCKS - 1
Z_GU, Z_V, Z_GRG, Z_Q, Z_GATE0 = 0, 1, 2, 3, 4
SUBLANES = 8
FFN_ROW_GROUP = 512
MiB = 1024 * 1024


def _params(semantics, vmem_mib):
    return pltpu.CompilerParams(dimension_semantics=semantics, vmem_limit_bytes=vmem_mib * MiB)


def _layer_norm(x, g, b):
    mu = jnp.mean(x, axis=-1, keepdims=True)
    xc = x - mu
    var = jnp.mean(xc * xc, axis=-1, keepdims=True)
    return xc * lax.rsqrt(var + LN_EPS) * g + b


def _dot(a, b):
    return jnp.dot(a, b, preferred_element_type=F32)


def _sigmoid(x):
    return 0.5 * jnp.tanh(0.5 * x) + 0.5


def _mxu_operand(w_ref, w16_ref=None):
    w = w_ref[...].astype(BF16)
    if w16_ref is not None:
        w16_ref[...] = w
    return w


def _ffn_kernel(x_ref, wg_ref, wu_ref, wd_ref, g_ref, b_ref, o_ref, *rest, emit_bf16, row_splits):
    if emit_bf16:
        wg16_ref, wu16_ref, wd16_ref, xb_ref = rest
    else:
        wg16_ref = wu16_ref = wd16_ref = None
        xb_ref, = rest
    j = pl.program_id(1)

    @pl.when(j == 0)
    def _():
        xb_ref[...] = x_ref[...].astype(BF16)
        o_ref[...] = jnp.zeros_like(o_ref)

    wg = _mxu_operand(wg_ref, wg16_ref)
    wu = _mxu_operand(wu_ref, wu16_ref)
    wd = _mxu_operand(wd_ref, wd16_ref)
    rows_per = xb_ref.shape[0] // row_splits
    for r in range(row_splits):
        rows = slice(r * rows_per, (r + 1) * rows_per)
        xb = xb_ref[rows, :]
        g = _dot(xb, wg)
        u = _dot(xb, wu)
        h = (g * _sigmoid(g) * u).astype(BF16)
        o_ref[rows, :] += _dot(h, wd)

    @pl.when(j == pl.num_programs(1) - 1)
    def _():
        y = ALPHA * x_ref[...] + 0.5 * o_ref[...]
        o_ref[...] = _layer_norm(y, g_ref[...], b_ref[...])


def _ffn(x, w_g, w_u, w_down, ln_g, ln_b, *, tm, emit_bf16=False, tf=512):
    m = x.shape[0]
    nf = D_FF // tf
    u_off = nf if w_u.shape[1] == 2 * D_FF else 0
    assert not emit_bf16 or m == tm
    out_specs = [pl.BlockSpec((tm, D_MODEL), lambda i, j: (i, 0))]
    out_shape = [jax.ShapeDtypeStruct((m, D_MODEL), F32)]
    if emit_bf16:
        out_specs += [pl.BlockSpec((D_MODEL, tf), lambda i, j: (0, j)),
                      pl.BlockSpec((D_MODEL, tf), lambda i, j: (0, j)),
                      pl.BlockSpec((tf, D_MODEL), lambda i, j: (j, 0))]
        out_shape += [jax.ShapeDtypeStruct((D_MODEL, D_FF), BF16),
                      jax.ShapeDtypeStruct((D_MODEL, D_FF), BF16),
                      jax.ShapeDtypeStruct((D_FF, D_MODEL), BF16)]
    outs = pl.pallas_call(
        functools.partial(_ffn_kernel, emit_bf16=emit_bf16, row_splits=max(1, tm // FFN_ROW_GROUP)),
        grid=(m // tm, nf),
        in_specs=[
            pl.BlockSpec((tm, D_MODEL), lambda i, j: (i, 0)),
            pl.BlockSpec((D_MODEL, tf), lambda i, j: (0, j)),
            pl.BlockSpec((D_MODEL, tf), lambda i, j: (0, j + u_off)),
            pl.BlockSpec((tf, D_MODEL), lambda i, j: (j, 0)),
            pl.BlockSpec((1, D_MODEL), lambda i, j: (0, 0)),
            pl.BlockSpec((1, D_MODEL), lambda i, j: (0, 0)),
        ],
        out_specs=out_specs,
        out_shape=out_shape,
        scratch_shapes=[pltpu.VMEM((tm, D_MODEL), BF16)],
        compiler_params=_params(("arbitrary", "arbitrary"), 62),
        name="ffn_ln",
    )(x, w_g, w_u, w_down, ln_g, ln_b)
    return outs if emit_bf16 else outs[0]


def _memkv_kernel(x_ref, wk_ref, wv_ref, g_ref, b_ref, k_ref, v_ref, xb_ref):
    @pl.when(pl.program_id(1) == 0)
    def _():
        xb_ref[...] = _layer_norm(x_ref[...], g_ref[...], b_ref[...]).astype(BF16)

    xb = xb_ref[...]
    k_ref[...] = _dot(xb, _mxu_operand(wk_ref))
    v_ref[...] = _dot(xb, _mxu_operand(wv_ref))


def _memkv(mem, w_kv, ln_g, ln_b, *, tm=1024, tn=512):
    m = mem.shape[0]
    nn = BR_W // tn
    return pl.pallas_call(
        _memkv_kernel,
        grid=(m // tm, nn),
        in_specs=[
            pl.BlockSpec((tm, D_MODEL), lambda i, j: (i, 0)),
            pl.BlockSpec((D_MODEL, tn), lambda i, j: (0, j)),
            pl.BlockSpec((D_MODEL, tn), lambda i, j: (0, j + nn)),
            pl.BlockSpec((1, D_MODEL), lambda i, j: (0, 0)),
            pl.BlockSpec((1, D_MODEL), lambda i, j: (0, 0)),
        ],
        out_specs=[pl.BlockSpec((tm, tn), lambda i, j: (i, j)),
                   pl.BlockSpec((tm, tn), lambda i, j: (i, j))],
        out_shape=[jax.ShapeDtypeStruct((m, BR_W), F32), jax.ShapeDtypeStruct((m, BR_W), F32)],
        scratch_shapes=[pltpu.VMEM((tm, D_MODEL), BF16)],
        compiler_params=_params(("arbitrary", "arbitrary"), 48),
        name="mem_kv",
    )(mem, w_kv, w_kv, ln_g, ln_b)


def _inproj_kernel(x_ref, w_ref, gb_ref, lg_ref, lb_ref, o_ref, rx_ref, *rest, emit_bf16):
    if emit_bf16:
        w16_ref, xb_ref = rest
        _mxu_operand(w_ref, w16_ref)
        w_ref = w16_ref
    else:
        xb_ref, = rest
    j = pl.program_id(1)

    @pl.when(j == 0)
    def _():
        xb_ref[...] = x_ref[...].astype(BF16)

    def z():
        return _dot(xb_ref[...], w_ref[...])

    @pl.when((j == 0) | (j == 3))
    def _():
        o_ref[...] = jax.nn.gelu(z()).astype(o_ref.dtype)

    @pl.when(j == 1)
    def _():
        o_ref[...] = _layer_norm(jax.nn.gelu(z()), lg_ref[...], lb_ref[...]).astype(o_ref.dtype)

    @pl.when(j == 2)
    def _():
        rx_ref[...] = z()

    @pl.when(j == 4)
    def _():
        o_ref[...] = z().astype(o_ref.dtype)

    @pl.when(j >= GATE_BLOCK0)
    def _():
        o_ref[...] = _sigmoid(z() + gb_ref[...]).astype(o_ref.dtype)


def _inproj(x, w_in, gate_b, gln_g, gln_b, *, tm, z_dtype, emit_bf16=False):
    m = x.shape[0]
    assert not emit_bf16 or m == tm
    out_specs = [pl.BlockSpec((tm, BR_W), lambda i, j: (i, jnp.where(j < RX_BLOCK, j, j - 1))),
                 pl.BlockSpec((tm, BR_W), lambda i, j: (i, 0))]
    out_shape = [jax.ShapeDtypeStruct((m, Z_BLOCKS * BR_W), z_dtype),
                 jax.ShapeDtypeStruct((m, BR_W), F32)]
    if emit_bf16:
        out_specs.append(pl.BlockSpec((D_MODEL, BR_W), lambda i, j: (0, j)))
        out_shape.append(jax.ShapeDtypeStruct((D_MODEL, IN_BLOCKS * BR_W), BF16))
    return pl.pallas_call(
        functools.partial(_inproj_kernel, emit_bf16=emit_bf16),
        grid=(m // tm, IN_BLOCKS),
        in_specs=[
            pl.BlockSpec((tm, D_MODEL), lambda i, j: (i, 0)),
            pl.BlockSpec((D_MODEL, BR_W), lambda i, j: (0, j)),
            pl.BlockSpec((1, BR_W), lambda i, j: (0, jnp.maximum(j - GATE_BLOCK0, 0))),
            pl.BlockSpec((1, BR_W), lambda i, j: (0, 0)),
            pl.BlockSpec((1, BR_W), lambda i, j: (0, 0)),
        ],
        out_specs=out_specs,
        out_shape=out_shape,
        scratch_shapes=[pltpu.VMEM((tm, D_MODEL), BF16)],
        compiler_params=_params(("arbitrary", "arbitrary"), 48),
        name="in_proj",
    )(x, w_in, gate_b, gln_g, gln_b)


def _softplus(x):
    return jnp.maximum(x, 0.0) + jnp.log1p(jnp.exp(-jnp.abs(x)))


def _lru_coeffs(xc, wa_ref, ba_ref, wx_ref, bx_ref, lam_ref):
    xcb = xc.astype(BF16)
    r_parts, i_parts = [], []
    for k in range(LRU_BLOCKS):
        xk = xcb[:, k * LRU_BLOCK:(k + 1) * LRU_BLOCK]
        r_parts.append(_dot(xk, wa_ref[k]))
        i_parts.append(_dot(xk, wx_ref[k]))
    r = _sigmoid(jnp.concatenate(r_parts, axis=1) + ba_ref[...])
    i = _sigmoid(jnp.concatenate(i_parts, axis=1) + bx_ref[...])
    log_a = (-LRU_C) * r * _softplus(-lam_ref[...])
    a = jnp.exp(log_a)
    b = jnp.sqrt(-jnp.tanh(log_a) * (a * a + 1.0)) * (i * xc)
    return a, b


def _mixer_kernel(gu_ref, v_ref, rx_ref, grg_ref, q_ref, k_ref, vm_ref,
                  ws_ref, bs_ref, cw_ref, cb_ref, wa_ref, ba_ref, wx_ref, bx_ref, lam_ref,
                  y_ref, conv_ref, hlast_ref,
                  xpad_ref, a_ref, b_ref, h_ref, hc_ref, *, tm):
    t = pl.program_id(1)

    tri = (lax.broadcasted_iota(jnp.int32, (CHUNK, CHUNK), 0)
           >= lax.broadcasted_iota(jnp.int32, (CHUNK, CHUNK), 1))
    for g in range(GMLP_GROUPS):
        wg = jnp.where(tri, ws_ref[g], 0.0).astype(BF16)
        cols = slice(g * GROUP_W, (g + 1) * GROUP_W)
        for c in range(tm // CHUNK):
            rows = slice(c * CHUNK, (c + 1) * CHUNK)
            s = _dot(wg, v_ref[rows, cols].astype(BF16)) + bs_ref[:, cols]
            y_ref[0, rows, cols] = (gu_ref[rows, cols] * s).astype(BF16)

    @pl.when(t == 0)
    def _():
        xpad_ref[0:SUBLANES, :] = jnp.zeros((SUBLANES, BR_W), F32)
        hc_ref[...] = jnp.zeros((1, BR_W), F32)

    @pl.when(t > 0)
    def _():
        xpad_ref[0:SUBLANES, :] = xpad_ref[tm:tm + SUBLANES, :]

    xpad_ref[SUBLANES:SUBLANES + tm, :] = rx_ref[...]
    xc = cb_ref[...]
    for k in range(CONV_W):
        off = SUBLANES - (CONV_W - 1) + k
        xc = xc + xpad_ref[off:off + tm, :] * cw_ref[k:k + 1, :]

    a, b = _lru_coeffs(xc, wa_ref, ba_ref, wx_ref, bx_ref, lam_ref)
    a_ref[...] = a
    b_ref[...] = b

    row = lax.broadcasted_iota(jnp.int32, (SUBLANES, BR_W), 0)
    keep = [row >= d for d in (1, 2, 4)]

    def scan_block(blk, h):
        base = pl.multiple_of(blk * SUBLANES, SUBLANES)
        ca = a_ref[pl.ds(base, SUBLANES), :]
        cb = b_ref[pl.ds(base, SUBLANES), :]
        for d, kp in zip((1, 2, 4), keep):
            a_sh = jnp.where(kp, pltpu.roll(ca, d, 0), 1.0)
            b_sh = jnp.where(kp, pltpu.roll(cb, d, 0), 0.0)
            cb = ca * b_sh + cb
            ca = ca * a_sh
        hh = cb + ca * h
        h_ref[pl.ds(base, SUBLANES), :] = hh
        return hh[SUBLANES - 1:SUBLANES, :]

    h_end = lax.fori_loop(0, tm // SUBLANES, scan_block, hc_ref[...])
    hc_ref[...] = h_end
    y_ref[1] = (grg_ref[...] * h_ref[...]).astype(BF16)

    @pl.when(t == pl.num_programs(1) - 1)
    def _():
        conv_ref[0] = rx_ref[tm - (CONV_W - 1):tm, :]
        hlast_ref[0] = h_end

    scale = XA_HEAD_DIM ** -0.5
    for hd in range(XA_HEADS):
        cols = slice(hd * XA_HEAD_DIM, (hd + 1) * XA_HEAD_DIM)
        qh = q_ref[:, cols].astype(BF16)
        kh = k_ref[0, :, cols].astype(BF16)
        vh = vm_ref[0, :, cols].astype(BF16)
        s = lax.dot_general(qh, kh, (((1,), (1,)), ((), ())), preferred_element_type=F32) * scale
        e = jnp.exp(s - jnp.max(s, axis=-1, keepdims=True))
        p = e / jnp.sum(e, axis=-1, keepdims=True)
        y_ref[2, :, cols] = _dot(p.astype(BF16), vh).astype(BF16)


def _mixer_prompt(z, rx, k_mem, v_mem, ws, bs_mat, conv_w, conv_b, wa, ba, wx, bx, lam, *, bsz, seq, tm=256):
    nt = seq // tm
    zspec = lambda c: pl.BlockSpec((tm, BR_W), lambda b, t, c=c: (b * nt + t, c))
    full = lambda shape: pl.BlockSpec(shape, lambda b, t: (0,) * len(shape))
    return pl.pallas_call(
        functools.partial(_mixer_kernel, tm=tm),
        grid=(bsz, nt),
        in_specs=[zspec(Z_GU), zspec(Z_V), zspec(0), zspec(Z_GRG), zspec(Z_Q),
                  pl.BlockSpec((1, N_MEM, BR_W), lambda b, t: (b, 0, 0)),
                  pl.BlockSpec((1, N_MEM, BR_W), lambda b, t: (b, 0, 0)),
                  full((GMLP_GROUPS, CHUNK, CHUNK)), full((CHUNK, BR_W)),
                  full((CONV_W, BR_W)), full((1, BR_W)),
                  full((LRU_BLOCKS, LRU_BLOCK, LRU_BLOCK)), full((1, BR_W)),
                  full((LRU_BLOCKS, LRU_BLOCK, LRU_BLOCK)), full((1, BR_W)),
                  full((1, BR_W))],
        out_specs=[pl.BlockSpec((3, tm, BR_W), lambda b, t: (0, b * nt + t, 0)),
                   pl.BlockSpec((1, CONV_W - 1, BR_W), lambda b, t: (b, 0, 0)),
                   pl.BlockSpec((1, 1, BR_W), lambda b, t: (b, 0, 0))],
        out_shape=[jax.ShapeDtypeStruct((3, bsz * seq, BR_W), BF16),
                   jax.ShapeDtypeStruct((bsz, CONV_W - 1, BR_W), F32),
                   jax.ShapeDtypeStruct((bsz, 1, BR_W), F32)],
        scratch_shapes=[pltpu.VMEM((tm + SUBLANES, BR_W), F32),
                        pltpu.VMEM((tm, BR_W), F32), pltpu.VMEM((tm, BR_W), F32),
                        pltpu.VMEM((tm, BR_W), F32), pltpu.VMEM((1, BR_W), F32)],
        compiler_params=_params(("arbitrary", "arbitrary"), 40),
        name="mixer_prompt",
    )(z, z, rx, z, z, k_mem, v_mem, ws, bs_mat, conv_w, conv_b, wa, ba, wx, bx, lam)


def _mixer_sample_kernel(gu_ref, v_ref, rx_ref, grg_ref, cs_ref, h0_ref,
                         wv_ref, bv_ref, cw_ref, cb_ref, wa_ref, ba_ref, wx_ref, bx_ref, lam_ref,
                         y_ref, conv_ref, h_ref):
    y_ref[0] = (gu_ref[...] * (wv_ref[...] * v_ref[...] + bv_ref[...])).astype(BF16)

    rx = rx_ref[...]
    xc = cb_ref[...] + rx * cw_ref[CONV_W - 1:CONV_W, :]
    for k in range(CONV_W - 1):
        xc = xc + cs_ref[k] * cw_ref[k:k + 1, :]
    for k in range(CONV_W - 2):
        conv_ref[k] = cs_ref[k + 1]
    conv_ref[CONV_W - 2] = rx

    a, b = _lru_coeffs(xc, wa_ref, ba_ref, wx_ref, bx_ref, lam_ref)
    h = a * h0_ref[...] + b
    h_ref[...] = h
    y_ref[1] = (grg_ref[...] * h).astype(BF16)


def _mixer_sample(z, rx, conv_state, h0, wvec, bvec, conv_w, conv_b, wa, ba, wx, bx, lam):
    n = z.shape[0]
    zspec = lambda c: pl.BlockSpec((n, BR_W), lambda i, c=c: (0, c))
    full = lambda shape: pl.BlockSpec(shape, lambda i: (0,) * len(shape))
    return pl.pallas_call(
        _mixer_sample_kernel,
        grid=(1,),
        in_specs=[zspec(Z_GU), zspec(Z_V), zspec(0), zspec(Z_GRG),
                  full((CONV_W - 1, n, BR_W)), full((n, BR_W)),
                  full((1, BR_W)), full((1, BR_W)), full((CONV_W, BR_W)), full((1, BR_W)),
                  full((LRU_BLOCKS, LRU_BLOCK, LRU_BLOCK)), full((1, BR_W)),
                  full((LRU_BLOCKS, LRU_BLOCK, LRU_BLOCK)), full((1, BR_W)),
                  full((1, BR_W))],
        out_specs=[full((2, n, BR_W)), full((CONV_W - 1, n, BR_W)), full((n, BR_W))],
        out_shape=[jax.ShapeDtypeStruct((2, n, BR_W), BF16),
                   jax.ShapeDtypeStruct((CONV_W - 1, n, BR_W), F32),
                   jax.ShapeDtypeStruct((n, BR_W), F32)],
        compiler_params=_params(("arbitrary",), 32),
        name="mixer_sample",
    )(z, z, rx, z, conv_state, h0, wvec, bvec, conv_w, conv_b, wa, ba, wx, bx, lam)


def _xattn_sample_kernel(q_ref, k_ref, v_ref, o_ref, *, tb):
    scale = XA_HEAD_DIM ** -0.5

    def one(b, carry):
        q = q_ref[b]
        s = jnp.sum(k_ref[b] * q[None], axis=-1, keepdims=True) * scale
        e = jnp.exp(s - jnp.max(s, axis=0, keepdims=True))
        p = e / jnp.sum(e, axis=0, keepdims=True)
        o_ref[b] = jnp.sum(p * v_ref[b], axis=0).astype(BF16)
        return carry

    lax.fori_loop(0, tb, one, 0)


def _xattn_sample(q, k_cache, v_cache, *, tb=2):
    n = q.shape[0]
    kv_spec = pl.BlockSpec((None, tb, N_MEM, XA_HEADS, XA_HEAD_DIM), lambda i: (0, i, 0, 0, 0))
    return pl.pallas_call(
        functools.partial(_xattn_sample_kernel, tb=tb),
        grid=(n // tb,),
        in_specs=[pl.BlockSpec((tb, XA_HEADS, XA_HEAD_DIM), lambda i: (i, 0, 0)), kv_spec, kv_spec],
        out_specs=pl.BlockSpec((tb, XA_HEADS, XA_HEAD_DIM), lambda i: (i, 0, 0)),
        out_shape=jax.ShapeDtypeStruct((n, XA_HEADS, XA_HEAD_DIM), BF16),
        compiler_params=_params(("arbitrary",), 40),
        name="xattn_sample",
    )(q, k_cache, v_cache)


N_BRANCH = 3


def _merge_kernel(y_ref, wb_ref, g0_ref, g1_ref, g2_ref, o_ref, wb16_ref=None):
    acc = None
    for k, gate_ref in enumerate((g0_ref, g1_ref, g2_ref)):
        wk = _mxu_operand(wb_ref.at[k], None if wb16_ref is None else wb16_ref.at[k])
        term = gate_ref[...].astype(F32) * _dot(y_ref[k], wk)
        acc = term if acc is None else acc + term
    o_ref[...] = acc.astype(BF16)


def _merge(y, w_branch, z, *, tm, emit_bf16=False):
    m = y.shape[1]
    assert not emit_bf16 or m == tm
    gate_spec = lambda k: pl.BlockSpec((tm, BR_W), lambda i, n, k=k: (i, Z_GATE0 + 2 * k + n))
    w_spec = pl.BlockSpec((N_BRANCH, BR_W, BR_W), lambda i, n: (0, 0, n))
    out_specs = [pl.BlockSpec((tm, BR_W), lambda i, n: (i, n))]
    out_shape = [jax.ShapeDtypeStruct((m, D_MODEL), BF16)]
    if emit_bf16:
        out_specs.append(w_spec)
        out_shape.append(jax.ShapeDtypeStruct(w_branch.shape, BF16))
    outs = pl.pallas_call(
        _merge_kernel,
        grid=(m // tm, D_MODEL // BR_W),
        in_specs=[
            pl.BlockSpec((N_BRANCH, tm, BR_W), lambda i, n: (0, i, 0)),
            w_spec,
            gate_spec(0), gate_spec(1), gate_spec(2),
        ],
        out_specs=out_specs,
        out_shape=out_shape,
        compiler_params=_params(("arbitrary", "arbitrary"), 56),
        name="merge",
    )(y, w_branch, z, z, z)
    return outs if emit_bf16 else outs[0]


def _outproj_kernel(m_ref, wo_ref, x_ref, g_ref, b_ref, o_ref, *, row_splits):
    rows_per = m_ref.shape[0] // row_splits
    for r in range(row_splits):
        rows = slice(r * rows_per, (r + 1) * rows_per)
        y = ALPHA * x_ref[rows, :] + _dot(m_ref[rows, :], wo_ref[...])
        o_ref[rows, :] = _layer_norm(y, g_ref[...], b_ref[...])


def _outproj(merged, w_out, x, ln_g, ln_b, *, tm):
    m = x.shape[0]
    return pl.pallas_call(
        functools.partial(_outproj_kernel, row_splits=2 if tm >= 512 else 1),
        grid=(m // tm,),
        in_specs=[
            pl.BlockSpec((tm, D_MODEL), lambda i: (i, 0)),
            pl.BlockSpec((D_MODEL, D_MODEL), lambda i: (0, 0)),
            pl.BlockSpec((tm, D_MODEL), lambda i: (i, 0)),
            pl.BlockSpec((1, D_MODEL), lambda i: (0, 0)),
            pl.BlockSpec((1, D_MODEL), lambda i: (0, 0)),
        ],
        out_specs=pl.BlockSpec((tm, D_MODEL), lambda i: (i, 0)),
        out_shape=jax.ShapeDtypeStruct((m, D_MODEL), F32),
        compiler_params=_params(("arbitrary",), 48),
        name="out_proj",
    )(merged, w_out, x, ln_g, ln_b)


def kernel(x_prompt, x_sample, mem_prompt, cache_mem_k, cache_mem_v, state_conv, state_lru_h, ffn1_w_gu, ffn1_w_down, ln1_g, ln1_b, w_in, gate_b, gmlp_ln_g, gmlp_ln_b, gmlp_w_s, gmlp_b_s, conv_w, conv_b, lru_w_a, lru_b_a, lru_w_x, lru_b_x, lru_lambda, mem_ln_g, mem_ln_b, w_mem_kv, w_branch, w_out, ln2_g, ln2_b, ffn2_w_gu, ffn2_w_down, ln3_g, ln3_b):
    bsz, seq, _ = x_prompt.shape
    n_s = x_sample.shape[0]
    l = 0

    wout = w_out[l].astype(BF16)
    wa, wx = lru_w_a[l].astype(BF16), lru_w_x[l].astype(BF16)

    row = lambda p: p[l].reshape(1, -1)
    gb = gate_b[l].reshape(1, -1)
    bs_mat = jnp.repeat(gmlp_b_s[l].T, GROUP_W, axis=1)
    wvec = jnp.repeat(gmlp_w_s[l][:, 0, 0], GROUP_W).reshape(1, -1)
    bvec = jnp.repeat(gmlp_b_s[l][:, 0], GROUP_W).reshape(1, -1)
    lru = (wa, row(lru_b_a), wx, row(lru_b_x), row(lru_lambda))

    xp = x_prompt.reshape(bsz * seq, D_MODEL)
    xs = x_sample.reshape(n_s, D_MODEL)

    x1s, w1g, w1u, w1d = _ffn(xs, ffn1_w_gu[l], ffn1_w_gu[l], ffn1_w_down[l], row(ln1_g), row(ln1_b),
                              tm=n_s, emit_bf16=True)
    zs, rxs, win = _inproj(x1s, w_in[l], gb, row(gmlp_ln_g), row(gmlp_ln_b), tm=n_s, z_dtype=F32,
                           emit_bf16=True)
    y01, conv_s, h_s = _mixer_sample(
        zs, rxs, jnp.swapaxes(state_conv[l], 0, 1), state_lru_h[l], wvec, bvec, conv_w[l], row(conv_b), *lru)
    q_s = zs[:, Z_Q * BR_W:(Z_Q + 1) * BR_W].reshape(n_s, XA_HEADS, XA_HEAD_DIM)
    yxa = _xattn_sample(q_s, cache_mem_k, cache_mem_v)
    ys = jnp.concatenate([y01, yxa.reshape(1, n_s, BR_W)], axis=0)
    ms, wbr = _merge(ys, w_branch[l], zs, tm=n_s, emit_bf16=True)
    x2s = _outproj(ms, wout, x1s, row(ln2_g), row(ln2_b), tm=n_s)
    x3s, w2g, w2u, w2d = _ffn(x2s, ffn2_w_gu[l], ffn2_w_gu[l], ffn2_w_down[l], row(ln3_g), row(ln3_b),
                              tm=n_s, emit_bf16=True)

    k_p, v_p = _memkv(mem_prompt.reshape(bsz * N_MEM, D_MODEL), w_mem_kv[l], row(mem_ln_g), row(mem_ln_b))
    x1p = _ffn(xp, w1g, w1u, w1d, row(ln1_g), row(ln1_b), tm=1024)
    zp, rxp = _inproj(x1p, win, gb, row(gmlp_ln_g), row(gmlp_ln_b), tm=1024, z_dtype=BF16)
    yp, conv_p, hlast_p = _mixer_prompt(
        zp, rxp, k_p.reshape(bsz, N_MEM, BR_W), v_p.reshape(bsz, N_MEM, BR_W), gmlp_w_s[l], bs_mat,
        conv_w[l], row(conv_b), *lru, bsz=bsz, seq=seq)
    mp = _merge(yp, wbr, zp, tm=1024)
    x2p = _outproj(mp, wout, x1p, row(ln2_g), row(ln2_b), tm=512)
    x3p = _ffn(x2p, w2g, w2u, w2d, row(ln3_g), row(ln3_b), tm=1024)

    kv_shape = (1, bsz, N_MEM, XA_HEADS, XA_HEAD_DIM)
    return (x3p.reshape(bsz, seq, D_MODEL),
            x3s.reshape(n_s, 1, D_MODEL),
            k_p.reshape(kv_shape),
            v_p.reshape(kv_shape),
            conv_p[None],
            hlast_p.reshape(1, bsz, BR_W),
            jnp.swapaxes(conv_s, 0, 1)[None],
            h_s[None],
            zs[:, Z_V * BR_W:(Z_V + 1) * BR_W].reshape(1, n_s, 1, BR_W))
```

```python
import functools

import jax
import jax.numpy as jnp
from jax import lax
from jax.experimental import pallas as pl
from jax.experimental.pallas import tpu as pltpu

F32 = jnp.float32
BF16 = jnp.bfloat16

D_MODEL = 2048
BR_W = 1024
D_FF = 5632
N_MEM = 256
XA_HEADS = 4
XA_HEAD_DIM = 256
GMLP_GROUPS = 4
GROUP_W = BR_W // GMLP_GROUPS
CHUNK = 128
LRU_BLOCKS = 8
LRU_BLOCK = 128
CONV_W = 4
LRU_C = 8.0
LN_EPS = 1e-5
ALPHA = 2.0 ** 0.25
IN_BLOCKS = 11
RX_BLOCK = 2
GATE_BLOCK0 = 5
Z_BLOCKS = IN_BLOCKS - 1
Z_GU, Z_V, Z_GRG, Z_Q, Z_GATE0 = 0, 1, 2, 3, 4
SUBLANES = 8
FFN_ROW_GROUP = 512
MiB = 1024 * 1024


def _params(semantics, vmem_mib):
    return pltpu.CompilerParams(dimension_semantics=semantics, vmem_limit_bytes=vmem_mib * MiB)


def _layer_norm(x, g, b):
    mu = jnp.mean(x, axis=-1, keepdims=True)
    xc = x - mu
    var = jnp.mean(xc * xc, axis=-1, keepdims=True)
    return xc * lax.rsqrt(var + LN_EPS) * g + b


def _dot(a, b):
    return jnp.dot(a, b, preferred_element_type=F32)


def _sigmoid(x):
    return 0.5 * jnp.tanh(0.5 * x) + 0.5


def _mxu_operand(w_ref, w16_ref=None):
    w = w_ref[...].astype(BF16)
    if w16_ref is not None:
        w16_ref[...] = w
    return w


def _ffn_kernel(x_ref, wg_ref, wu_ref, wd_ref, g_ref, b_ref, o_ref, *rest, emit_bf16, row_splits):
    if emit_bf16:
        wg16_ref, wu16_ref, wd16_ref, xb_ref = rest
    else:
        wg16_ref = wu16_ref = wd16_ref = None
        xb_ref, = rest
    j = pl.program_id(1)
    last = pl.num_programs(1) - 1
    rows_per = xb_ref.shape[0] // row_splits

    def step(first, final):
        wg = _mxu_operand(wg_ref, wg16_ref)
        wu = _mxu_operand(wu_ref, wu16_ref)
        wd = _mxu_operand(wd_ref, wd16_ref)
        for r in range(row_splits):
            rows = slice(r * rows_per, (r + 1) * rows_per)
            if first:
                xb = x_ref[rows, :].astype(BF16)
                xb_ref[rows, :] = xb
            else:
                xb = xb_ref[rows, :]
            g = _dot(xb, wg)
            u = _dot(xb, wu)
            h = (g * _sigmoid(g) * u).astype(BF16)
            acc = _dot(h, wd)
            if not first:
                acc = o_ref[rows, :] + acc
            if final:
                acc = _layer_norm(ALPHA * x_ref[rows, :] + 0.5 * acc, g_ref[...], b_ref[...])
            o_ref[rows, :] = acc

    pl.when(j == 0)(functools.partial(step, True, False))
    pl.when((j > 0) & (j < last))(functools.partial(step, False, False))
    pl.when(j == last)(functools.partial(step, False, True))


def _ffn(x, w_g, w_u, w_down, ln_g, ln_b, *, tm, emit_bf16=False, tf=512):
    m = x.shape[0]
    nf = D_FF // tf
    u_off = nf if w_u.shape[1] == 2 * D_FF else 0
    assert not emit_bf16 or m == tm
    out_specs = [pl.BlockSpec((tm, D_MODEL), lambda i, j: (i, 0))]
    out_shape = [jax.ShapeDtypeStruct((m, D_MODEL), F32)]
    if emit_bf16:
        out_specs += [pl.BlockSpec((D_MODEL, tf), lambda i, j: (0, j)),
                      pl.BlockSpec((D_MODEL, tf), lambda i, j: (0, j)),
                      pl.BlockSpec((tf, D_MODEL), lambda i, j: (j, 0))]
        out_shape += [jax.ShapeDtypeStruct((D_MODEL, D_FF), BF16),
                      jax.ShapeDtypeStruct((D_MODEL, D_FF), BF16),
                      jax.ShapeDtypeStruct((D_FF, D_MODEL), BF16)]
    outs = pl.pallas_call(
        functools.partial(_ffn_kernel, emit_bf16=emit_bf16, row_splits=max(1, tm // FFN_ROW_GROUP)),
        grid=(m // tm, nf),
        in_specs=[
            pl.BlockSpec((tm, D_MODEL), lambda i, j: (i, 0)),
            pl.BlockSpec((D_MODEL, tf), lambda i, j: (0, j)),
            pl.BlockSpec((D_MODEL, tf), lambda i, j: (0, j + u_off)),
            pl.BlockSpec((tf, D_MODEL), lambda i, j: (j, 0)),
            pl.BlockSpec((1, D_MODEL), lambda i, j: (0, 0)),
            pl.BlockSpec((1, D_MODEL), lambda i, j: (0, 0)),
        ],
        out_specs=out_specs,
        out_shape=out_shape,
        scratch_shapes=[pltpu.VMEM((tm, D_MODEL), BF16)],
        compiler_params=_params(("arbitrary", "arbitrary"), 62),
        name="ffn_ln",
    )(x, w_g, w_u, w_down, ln_g, ln_b)
    return outs if emit_bf16 else outs[0]


def _memkv_kernel(x_ref, wk_ref, wv_ref, g_ref, b_ref, k_ref, v_ref, xb_ref):
    @pl.when(pl.program_id(1) == 0)
    def _():
        xb_ref[...] = _layer_norm(x_ref[...], g_ref[...], b_ref[...]).astype(BF16)

    xb = xb_ref[...]
    k_ref[...] = _dot(xb, _mxu_operand(wk_ref))
    v_ref[...] = _dot(xb, _mxu_operand(wv_ref))


def _memkv(mem, w_kv, ln_g, ln_b, *, tm=1024, tn=512):
    m = mem.shape[0]
    nn = BR_W // tn
    return pl.pallas_call(
        _memkv_kernel,
        grid=(m // tm, nn),
        in_specs=[
            pl.BlockSpec((tm, D_MODEL), lambda i, j: (i, 0)),
            pl.BlockSpec((D_MODEL, tn), lambda i, j: (0, j)),
            pl.BlockSpec((D_MODEL, tn), lambda i, j: (0, j + nn)),
            pl.BlockSpec((1, D_MODEL), lambda i, j: (0, 0)),
            pl.BlockSpec((1, D_MODEL), lambda i, j: (0, 0)),
        ],
        out_specs=[pl.BlockSpec((tm, tn), lambda i, j: (i, j)),
                   pl.BlockSpec((tm, tn), lambda i, j: (i, j))],
        out_shape=[jax.ShapeDtypeStruct((m, BR_W), F32), jax.ShapeDtypeStruct((m, BR_W), F32)],
        scratch_shapes=[pltpu.VMEM((tm, D_MODEL), BF16)],
        compiler_params=_params(("arbitrary", "arbitrary"), 48),
        name="mem_kv",
    )(mem, w_kv, w_kv, ln_g, ln_b)


def _inproj_kernel(x_ref, w_ref, gb_ref, lg_ref, lb_ref, o_ref, rx_ref, *rest, emit_bf16):
    if emit_bf16:
        w16_ref, xb_ref = rest
        _mxu_operand(w_ref, w16_ref)
        w_ref = w16_ref
    else:
        xb_ref, = rest
    j = pl.program_id(1)

    @pl.when(j == 0)
    def _():
        xb_ref[...] = x_ref[...].astype(BF16)

    def emit(dst_ref, act):
        dst_ref[...] = act(_dot(xb_ref[...], w_ref[...])).astype(dst_ref.dtype)

    @pl.when((j == 0) | (j == 3))
    def _():
        emit(o_ref, jax.nn.gelu)

    @pl.when(j == 1)
    def _():
        emit(o_ref, lambda z: _layer_norm(jax.nn.gelu(z), lg_ref[...], lb_ref[...]))

    @pl.when(j == 2)
    def _():
        emit(rx_ref, lambda z: z)

    @pl.when(j == 4)
    def _():
        emit(o_ref, lambda z: z)

    @pl.when(j >= GATE_BLOCK0)
    def _():
        emit(o_ref, lambda z: _sigmoid(z + gb_ref[...]))


def _inproj(x, w_in, gate_b, gln_g, gln_b, *, tm, z_dtype, emit_bf16=False):
    m = x.shape[0]
    assert not emit_bf16 or m == tm
    out_specs = [pl.BlockSpec((tm, BR_W), lambda i, j: (i, jnp.where(j < RX_BLOCK, j, j - 1))),
                 pl.BlockSpec((tm, BR_W), lambda i, j: (i, 0))]
    out_shape = [jax.ShapeDtypeStruct((m, Z_BLOCKS * BR_W), z_dtype),
                 jax.ShapeDtypeStruct((m, BR_W), F32)]
    if emit_bf16:
        out_specs.append(pl.BlockSpec((D_MODEL, BR_W), lambda i, j: (0, j)))
        out_shape.append(jax.ShapeDtypeStruct((D_MODEL, IN_BLOCKS * BR_W), BF16))
    return pl.pallas_call(
        functools.partial(_inproj_kernel, emit_bf16=emit_bf16),
        grid=(m // tm, IN_BLOCKS),
        in_specs=[
            pl.BlockSpec((tm, D_MODEL), lambda i, j: (i, 0)),
            pl.BlockSpec((D_MODEL, BR_W), lambda i, j: (0, j)),
            pl.BlockSpec((1, BR_W), lambda i, j: (0, jnp.maximum(j - GATE_BLOCK0, 0))),
            pl.BlockSpec((1, BR_W), lambda i, j: (0, 0)),
            pl.BlockSpec((1, BR_W), lambda i, j: (0, 0)),
        ],
        out_specs=out_specs,
        out_shape=out_shape,
        scratch_shapes=[pltpu.VMEM((tm, D_MODEL), BF16)],
        compiler_params=_params(("arbitrary", "arbitrary"), 48),
        name="in_proj",
    )(x, w_in, gate_b, gln_g, gln_b)


def _softplus(x):
    return jnp.maximum(x, 0.0) + jnp.log1p(jnp.exp(-jnp.abs(x)))


def _lru_coeffs(xc, wa_ref, ba_ref, wx_ref, bx_ref, lam_ref):
    xcb = xc.astype(BF16)
    r_parts, i_parts = [], []
    for k in range(LRU_BLOCKS):
        xk = xcb[:, k * LRU_BLOCK:(k + 1) * LRU_BLOCK]
        r_parts.append(_dot(xk, wa_ref[k]))
        i_parts.append(_dot(xk, wx_ref[k]))
    r = _sigmoid(jnp.concatenate(r_parts, axis=1) + ba_ref[...])
    i = _sigmoid(jnp.concatenate(i_parts, axis=1) + bx_ref[...])
    log_a = (-LRU_C) * r * _softplus(-lam_ref[...])
    a = jnp.exp(log_a)
    m = -jnp.tanh(log_a) * (a * a + 1.0)
    root = jnp.where(m > 0.0, m * lax.rsqrt(m), 0.0)
    return a, root * (i * xc)


def _mixer_kernel(gu_ref, v_ref, rx_ref, grg_ref, q_ref, k_ref, vm_ref,
                  ws_ref, bs_ref, cw_ref, cb_ref, wa_ref, ba_ref, wx_ref, bx_ref, lam_ref,
                  y_ref, conv_ref, hlast_ref,
                  xpad_ref, a_ref, b_ref, h_ref, hc_ref, *, tm):
    t = pl.program_id(1)

    tri = (lax.broadcasted_iota(jnp.int32, (CHUNK, CHUNK), 0)
           >= lax.broadcasted_iota(jnp.int32, (CHUNK, CHUNK), 1))
    for g in range(GMLP_GROUPS):
        wg = jnp.where(tri, ws_ref[g], 0.0).astype(BF16)
        cols = slice(g * GROUP_W, (g + 1) * GROUP_W)
        for c in range(tm // CHUNK):
            rows = slice(c * CHUNK, (c + 1) * CHUNK)
            s = _dot(wg, v_ref[rows, cols].astype(BF16)) + bs_ref[:, cols]
            y_ref[0, rows, cols] = (gu_ref[rows, cols] * s).astype(BF16)

    @pl.when(t == 0)
    def _():
        xpad_ref[0:SUBLANES, :] = jnp.zeros((SUBLANES, BR_W), F32)
        hc_ref[...] = jnp.zeros((1, BR_W), F32)

    @pl.when(t > 0)
    def _():
        xpad_ref[0:SUBLANES, :] = xpad_ref[tm:tm + SUBLANES, :]

    xpad_ref[SUBLANES:SUBLANES + tm, :] = rx_ref[...]
    xc = cb_ref[...]
    for k in range(CONV_W):
        off = SUBLANES - (CONV_W - 1) + k
        xc = xc + xpad_ref[off:off + tm, :] * cw_ref[k:k + 1, :]

    a, b = _lru_coeffs(xc, wa_ref, ba_ref, wx_ref, bx_ref, lam_ref)
    a_ref[...] = a
    b_ref[...] = b

    row = lax.broadcasted_iota(jnp.int32, (SUBLANES, BR_W), 0)
    keep = [row >= d for d in (1, 2, 4)]

    def scan_block(blk, h):
        base = pl.multiple_of(blk * SUBLANES, SUBLANES)
        ca = a_ref[pl.ds(base, SUBLANES), :]
        cb = b_ref[pl.ds(base, SUBLANES), :]
        for d, kp in zip((1, 2, 4), keep):
            a_sh = jnp.where(kp, pltpu.roll(ca, d, 0), 1.0)
            b_sh = jnp.where(kp, pltpu.roll(cb, d, 0), 0.0)
            cb = ca * b_sh + cb
            ca = ca * a_sh
        hh = cb + ca * h
        h_ref[pl.ds(base, SUBLANES), :] = hh
        return hh[SUBLANES - 1:SUBLANES, :]

    h_end = lax.fori_loop(0, tm // SUBLANES, scan_block, hc_ref[...])
    hc_ref[...] = h_end
    y_ref[1] = (grg_ref[...] * h_ref[...]).astype(BF16)

    @pl.when(t == pl.num_programs(1) - 1)
    def _():
        conv_ref[0] = rx_ref[tm - (CONV_W - 1):tm, :]
        hlast_ref[0] = h_end

    scale = XA_HEAD_DIM ** -0.5
    for hd in range(XA_HEADS):
        cols = slice(hd * XA_HEAD_DIM, (hd + 1) * XA_HEAD_DIM)
        qh = q_ref[:, cols].astype(BF16)
        kh = k_ref[0, :, cols].astype(BF16)
        vh = vm_ref[0, :, cols].astype(BF16)
        s = lax.dot_general(qh, kh, (((1,), (1,)), ((), ())), preferred_element_type=F32) * scale
        e = jnp.exp(s - jnp.max(s, axis=-1, keepdims=True))
        p = e * (1.0 / jnp.sum(e, axis=-1, keepdims=True))
        y_ref[2, :, cols] = _dot(p.astype(BF16), vh).astype(BF16)


def _mixer_prompt(z, rx, k_mem, v_mem, ws, bs_mat, conv_w, conv_b, wa, ba, wx, bx, lam, *, bsz, seq, tm=256):
    nt = seq // tm
    zspec = lambda c: pl.BlockSpec((tm, BR_W), lambda b, t, c=c: (b * nt + t, c))
    full = lambda shape: pl.BlockSpec(shape, lambda b, t: (0,) * len(shape))
    return pl.pallas_call(
        functools.partial(_mixer_kernel, tm=tm),
        grid=(bsz, nt),
        in_specs=[zspec(Z_GU), zspec(Z_V), zspec(0), zspec(Z_GRG), zspec(Z_Q),
                  pl.BlockSpec((1, N_MEM, BR_W), lambda b, t: (b, 0, 0)),
                  pl.BlockSpec((1, N_MEM, BR_W), lambda b, t: (b, 0, 0)),
                  full((GMLP_GROUPS, CHUNK, CHUNK)), full((CHUNK, BR_W)),
                  full((CONV_W, BR_W)), full((1, BR_W)),
                  full((LRU_BLOCKS, LRU_BLOCK, LRU_BLOCK)), full((1, BR_W)),
                  full((LRU_BLOCKS, LRU_BLOCK, LRU_BLOCK)), full((1, BR_W)),
                  full((1, BR_W))],
        out_specs=[pl.BlockSpec((3, tm, BR_W), lambda b, t: (0, b * nt + t, 0)),
                   pl.BlockSpec((1, CONV_W - 1, BR_W), lambda b, t: (b, 0, 0)),
                   pl.BlockSpec((1, 1, BR_W), lambda b, t: (b, 0, 0))],
        out_shape=[jax.ShapeDtypeStruct((3, bsz * seq, BR_W), BF16),
                   jax.ShapeDtypeStruct((bsz, CONV_W - 1, BR_W), F32),
                   jax.ShapeDtypeStruct((bsz, 1, BR_W), F32)],
        scratch_shapes=[pltpu.VMEM((tm + SUBLANES, BR_W), F32),
                        pltpu.VMEM((tm, BR_W), F32), pltpu.VMEM((tm, BR_W), F32),
                        pltpu.VMEM((tm, BR_W), F32), pltpu.VMEM((1, BR_W), F32)],
        compiler_params=_params(("arbitrary", "arbitrary"), 40),
        name="mixer_prompt",
    )(z, z, rx, z, z, k_mem, v_mem, ws, bs_mat, conv_w, conv_b, wa, ba, wx, bx, lam)


def _mixer_sample_kernel(gu_ref, v_ref, rx_ref, grg_ref, cs_ref, h0_ref,
                         wv_ref, bv_ref, cw_ref, cb_ref, wa_ref, ba_ref, wx_ref, bx_ref, lam_ref,
                         y_ref, conv_ref, h_ref):
    y_ref[0] = (gu_ref[...] * (wv_ref[...] * v_ref[...] + bv_ref[...])).astype(BF16)

    rx = rx_ref[...]
    xc = cb_ref[...] + rx * cw_ref[CONV_W - 1:CONV_W, :]
    for k in range(CONV_W - 1):
        xc = xc + cs_ref[k] * cw_ref[k:k + 1, :]
    for k in range(CONV_W - 2):
        conv_ref[k] = cs_ref[k + 1]
    conv_ref[CONV_W - 2] = rx

    a, b = _lru_coeffs(xc, wa_ref, ba_ref, wx_ref, bx_ref, lam_ref)
    h = a * h0_ref[...] + b
    h_ref[...] = h
    y_ref[1] = (grg_ref[...] * h).astype(BF16)


def _mixer_sample(z, rx, conv_state, h0, wvec, bvec, conv_w, conv_b, wa, ba, wx, bx, lam):
    n = z.shape[0]
    zspec = lambda c: pl.BlockSpec((n, BR_W), lambda i, c=c: (0, c))
    full = lambda shape: pl.BlockSpec(shape, lambda i: (0,) * len(shape))
    return pl.pallas_call(
        _mixer_sample_kernel,
        grid=(1,),
        in_specs=[zspec(Z_GU), zspec(Z_V), zspec(0), zspec(Z_GRG),
                  full((CONV_W - 1, n, BR_W)), full((n, BR_W)),
                  full((1, BR_W)), full((1, BR_W)), full((CONV_W, BR_W)), full((1, BR_W)),
                  full((LRU_BLOCKS, LRU_BLOCK, LRU_BLOCK)), full((1, BR_W)),
                  full((LRU_BLOCKS, LRU_BLOCK, LRU_BLOCK)), full((1, BR_W)),
                  full((1, BR_W))],
        out_specs=[full((2, n, BR_W)), full((CONV_W - 1, n, BR_W)), full((n, BR_W))],
        out_shape=[jax.ShapeDtypeStruct((2, n, BR_W), BF16),
                   jax.ShapeDtypeStruct((CONV_W - 1, n, BR_W), F32),
                   jax.ShapeDtypeStruct((n, BR_W), F32)],
        compiler_params=_params(("arbitrary",), 32),
        name="mixer_sample",
    )(z, z, rx, z, conv_state, h0, wvec, bvec, conv_w, conv_b, wa, ba, wx, bx, lam)


def _xattn_sample_kernel(q_ref, k_ref, v_ref, o_ref, *, tb):
    scale = XA_HEAD_DIM ** -0.5

    def one(b, carry):
        q = q_ref[b]
        s = jnp.sum(k_ref[b] * q[None], axis=-1, keepdims=True) * scale
        e = jnp.exp(s - jnp.max(s, axis=0, keepdims=True))
        p = e / jnp.sum(e, axis=0, keepdims=True)
        o_ref[b] = jnp.sum(p * v_ref[b], axis=0).astype(BF16)
        return carry

    lax.fori_loop(0, tb, one, 0)


def _xattn_sample(q, k_cache, v_cache, *, tb=2):
    n = q.shape[0]
    kv_spec = pl.BlockSpec((None, tb, N_MEM, XA_HEADS, XA_HEAD_DIM), lambda i: (0, i, 0, 0, 0))
    return pl.pallas_call(
        functools.partial(_xattn_sample_kernel, tb=tb),
        grid=(n // tb,),
        in_specs=[pl.BlockSpec((tb, XA_HEADS, XA_HEAD_DIM), lambda i: (i, 0, 0)), kv_spec, kv_spec],
        out_specs=pl.BlockSpec((tb, XA_HEADS, XA_HEAD_DIM), lambda i: (i, 0, 0)),
        out_shape=jax.ShapeDtypeStruct((n, XA_HEADS, XA_HEAD_DIM), BF16),
        compiler_params=_params(("arbitrary",), 40),
        name="xattn_sample",
    )(q, k_cache, v_cache)


N_BRANCH = 3


def _merge_kernel(y_ref, wb_ref, g0_ref, g1_ref, g2_ref, o_ref, wb16_ref=None):
    acc = None
    for k, gate_ref in enumerate((g0_ref, g1_ref, g2_ref)):
        wk = _mxu_operand(wb_ref.at[k], None if wb16_ref is None else wb16_ref.at[k])
        term = gate_ref[...].astype(F32) * _dot(y_ref[k], wk)
        acc = term if acc is None else acc + term
    o_ref[...] = acc.astype(BF16)


def _merge(y, w_branch, z, *, tm, emit_bf16=False):
    m = y.shape[1]
    assert not emit_bf16 or m == tm
    gate_spec = lambda k: pl.BlockSpec((tm, BR_W), lambda i, n, k=k: (i, Z_GATE0 + 2 * k + n))
    w_spec = pl.BlockSpec((N_BRANCH, BR_W, BR_W), lambda i, n: (0, 0, n))
    out_specs = [pl.BlockSpec((tm, BR_W), lambda i, n: (i, n))]
    out_shape = [jax.ShapeDtypeStruct((m, D_MODEL), BF16)]
    if emit_bf16:
        out_specs.append(w_spec)
        out_shape.append(jax.ShapeDtypeStruct(w_branch.shape, BF16))
    outs = pl.pallas_call(
        _merge_kernel,
        grid=(m // tm, D_MODEL // BR_W),
        in_specs=[
            pl.BlockSpec((N_BRANCH, tm, BR_W), lambda i, n: (0, i, 0)),
            w_spec,
            gate_spec(0), gate_spec(1), gate_spec(2),
        ],
        out_specs=out_specs,
        out_shape=out_shape,
        compiler_params=_params(("arbitrary", "arbitrary"), 56),
        name="merge",
    )(y, w_branch, z, z, z)
    return outs if emit_bf16 else outs[0]


def _outproj_kernel(m_ref, wo_ref, x_ref, g_ref, b_ref, o_ref, *, row_splits):
    rows_per = m_ref.shape[0] // row_splits
    for r in range(row_splits):
        rows = slice(r * rows_per, (r + 1) * rows_per)
        y = ALPHA * x_ref[rows, :] + _dot(m_ref[rows, :], wo_ref[...])
        o_ref[rows, :] = _layer_norm(y, g_ref[...], b_ref[...])


def _outproj(merged, w_out, x, ln_g, ln_b, *, tm):
    m = x.shape[0]
    return pl.pallas_call(
        functools.partial(_outproj_kernel, row_splits=2 if tm >= 512 else 1),
        grid=(m // tm,),
        in_specs=[
            pl.BlockSpec((tm, D_MODEL), lambda i: (i, 0)),
            pl.BlockSpec((D_MODEL, D_MODEL), lambda i: (0, 0)),
            pl.BlockSpec((tm, D_MODEL), lambda i: (i, 0)),
            pl.BlockSpec((1, D_MODEL), lambda i: (0, 0)),
            pl.BlockSpec((1, D_MODEL), lambda i: (0, 0)),
        ],
        out_specs=pl.BlockSpec((tm, D_MODEL), lambda i: (i, 0)),
        out_shape=jax.ShapeDtypeStruct((m, D_MODEL), F32),
        compiler_params=_params(("arbitrary",), 48),
        name="out_proj",
    )(merged, w_out, x, ln_g, ln_b)


def kernel(x_prompt, x_sample, mem_prompt, cache_mem_k, cache_mem_v, state_conv, state_lru_h, ffn1_w_gu, ffn1_w_down, ln1_g, ln1_b, w_in, gate_b, gmlp_ln_g, gmlp_ln_b, gmlp_w_s, gmlp_b_s, conv_w, conv_b, lru_w_a, lru_b_a, lru_w_x, lru_b_x, lru_lambda, mem_ln_g, mem_ln_b, w_mem_kv, w_branch, w_out, ln2_g, ln2_b, ffn2_w_gu, ffn2_w_down, ln3_g, ln3_b):
    bsz, seq, _ = x_prompt.shape
    n_s = x_sample.shape[0]
    l = 0

    wout = w_out[l].astype(BF16)
    wa, wx = lru_w_a[l].astype(BF16), lru_w_x[l].astype(BF16)

    row = lambda p: p[l].reshape(1, -1)
    gb = gate_b[l].reshape(1, -1)
    bs_mat = jnp.repeat(gmlp_b_s[l].T, GROUP_W, axis=1)
    wvec = jnp.repeat(gmlp_w_s[l][:, 0, 0], GROUP_W).reshape(1, -1)
    bvec = jnp.repeat(gmlp_b_s[l][:, 0], GROUP_W).reshape(1, -1)
    lru = (wa, row(lru_b_a), wx, row(lru_b_x), row(lru_lambda))

    xp = x_prompt.reshape(bsz * seq, D_MODEL)
    xs = x_sample.reshape(n_s, D_MODEL)

    x1s, w1g, w1u, w1d = _ffn(xs, ffn1_w_gu[l], ffn1_w_gu[l], ffn1_w_down[l], row(ln1_g), row(ln1_b),
                              tm=n_s, emit_bf16=True)
    zs, rxs, win = _inproj(x1s, w_in[l], gb, row(gmlp_ln_g), row(gmlp_ln_b), tm=n_s, z_dtype=F32,
                           emit_bf16=True)
    y01, conv_s, h_s = _mixer_sample(
        zs, rxs, jnp.swapaxes(state_conv[l], 0, 1), state_lru_h[l], wvec, bvec, conv_w[l], row(conv_b), *lru)
    q_s = zs[:, Z_Q * BR_W:(Z_Q + 1) * BR_W].reshape(n_s, XA_HEADS, XA_HEAD_DIM)
    yxa = _xattn_sample(q_s, cache_mem_k, cache_mem_v)
    ys = jnp.concatenate([y01, yxa.reshape(1, n_s, BR_W)], axis=0)
    ms, wbr = _merge(ys, w_branch[l], zs, tm=n_s, emit_bf16=True)
    x2s = _outproj(ms, wout, x1s, row(ln2_g), row(ln2_b), tm=n_s)
    x3s, w2g, w2u, w2d = _ffn(x2s, ffn2_w_gu[l], ffn2_w_gu[l], ffn2_w_down[l], row(ln3_g), row(ln3_b),
                              tm=n_s, emit_bf16=True)

    k_p, v_p = _memkv(mem_prompt.reshape(bsz * N_MEM, D_MODEL), w_mem_kv[l], row(mem_ln_g), row(mem_ln_b))
    x1p = _ffn(xp, w1g, w1u, w1d, row(ln1_g), row(ln1_b), tm=1024)
    zp, rxp = _inproj(x1p, win, gb, row(gmlp_ln_g), row(gmlp_ln_b), tm=1024, z_dtype=BF16)
    yp, conv_p, hlast_p = _mixer_prompt(
        zp, rxp, k_p.reshape(bsz, N_MEM, BR_W), v_p.reshape(bsz, N_MEM, BR_W), gmlp_w_s[l], bs_mat,
        conv_w[l], row(conv_b), *lru, bsz=bsz, seq=seq)
    mp = _merge(yp, wbr, zp, tm=1024)
    x2p = _outproj(mp, wout, x1p, row(ln2_g), row(ln2_b), tm=512)
    x3p = _ffn(x2p, w2g, w2u, w2d, row(ln3_g), row(ln3_b), tm=1024)

    kv_shape = (1, bsz, N_MEM, XA_HEADS, XA_HEAD_DIM)
    return (x3p.reshape(bsz, seq, D_MODEL),
            x3s.reshape(n_s, 1, D_MODEL),
            k_p.reshape(kv_shape),
            v_p.reshape(kv_shape),
            conv_p[None],
            hlast_p.reshape(1, bsz, BR_W),
            jnp.swapaxes(conv_s, 0, 1)[None],
            h_s[None],
            zs[:, Z_V * BR_W:(Z_V + 1) * BR_W].reshape(1, n_s, 1, BR_W))
```

```python
import functools

import jax
import jax.numpy as jnp
from jax import lax
from jax.experimental import pallas as pl
from jax.experimental.pallas import tpu as pltpu

F32 = jnp.float32
BF16 = jnp.bfloat16

D_MODEL = 2048
BR_W = 1024
D_FF = 5632
N_MEM = 256
XA_HEADS = 4
XA_HEAD_DIM = 256
GMLP_GROUPS = 4
GROUP_W = BR_W // GMLP_GROUPS
CHUNK = 128
LRU_BLOCKS = 8
LRU_BLOCK = 128
CONV_W = 4
LRU_C = 8.0
LN_EPS = 1e-5
ALPHA = 2.0 ** 0.25
IN_BLOCKS = 11
RX_BLOCK = 2
GATE_BLOCK0 = 5
Z_BLOCKS = IN_BLOCKS - 1
Z_GU, Z_V, Z_GRG, Z_Q, Z_GATE0 = 0, 1, 2, 3, 4
SUBLANES = 8
FFN_ROW_GROUP = 512
MiB = 1024 * 1024


def _params(semantics, vmem_mib):
    return pltpu.CompilerParams(dimension_semantics=semantics, vmem_limit_bytes=vmem_mib * MiB)


def _layer_norm(x, g, b):
    mu = jnp.mean(x, axis=-1, keepdims=True)
    xc = x - mu
    var = jnp.mean(xc * xc, axis=-1, keepdims=True)
    return xc * lax.rsqrt(var + LN_EPS) * g + b


def _dot(a, b):
    return jnp.dot(a, b, preferred_element_type=F32)


def _sigmoid(x):
    return 0.5 * jnp.tanh(0.5 * x) + 0.5


def _mxu_operand(w_ref, w16_ref=None):
    w = w_ref[...].astype(BF16)
    if w16_ref is not None:
        w16_ref[...] = w
    return w


def _ffn_kernel(x_ref, wg_ref, wu_ref, wd_ref, g_ref, b_ref, o_ref, *rest, emit_bf16, row_splits):
    if emit_bf16:
        wg16_ref, wu16_ref, wd16_ref, xb_ref = rest
    else:
        wg16_ref = wu16_ref = wd16_ref = None
        xb_ref, = rest
    j = pl.program_id(1)
    last = pl.num_programs(1) - 1
    rows_per = xb_ref.shape[0] // row_splits

    def step(first, final):
        wg = _mxu_operand(wg_ref, wg16_ref)
        wu = _mxu_operand(wu_ref, wu16_ref)
        wd = _mxu_operand(wd_ref, wd16_ref)
        for r in range(row_splits):
            rows = slice(r * rows_per, (r + 1) * rows_per)
            if first:
                xb = x_ref[rows, :].astype(BF16)
                xb_ref[rows, :] = xb
            else:
                xb = xb_ref[rows, :]
            g = _dot(xb, wg)
            u = _dot(xb, wu)
            h = (g * _sigmoid(g) * u).astype(BF16)
            acc = _dot(h, wd)
            if not first:
                acc = o_ref[rows, :] + acc
            if final:
                acc = _layer_norm(ALPHA * x_ref[rows, :] + 0.5 * acc, g_ref[...], b_ref[...])
            o_ref[rows, :] = acc

    pl.when(j == 0)(functools.partial(step, True, False))
    pl.when((j > 0) & (j < last))(functools.partial(step, False, False))
    pl.when(j == last)(functools.partial(step, False, True))


def _ffn(x, w_g, w_u, w_down, ln_g, ln_b, *, tm, emit_bf16=False, tf=512):
    m = x.shape[0]
    nf = D_FF // tf
    u_off = nf if w_u.shape[1] == 2 * D_FF else 0
    assert not emit_bf16 or m == tm
    out_specs = [pl.BlockSpec((tm, D_MODEL), lambda i, j: (i, 0))]
    out_shape = [jax.ShapeDtypeStruct((m, D_MODEL), F32)]
    if emit_bf16:
        out_specs += [pl.BlockSpec((D_MODEL, tf), lambda i, j: (0, j)),
                      pl.BlockSpec((D_MODEL, tf), lambda i, j: (0, j)),
                      pl.BlockSpec((tf, D_MODEL), lambda i, j: (j, 0))]
        out_shape += [jax.ShapeDtypeStruct((D_MODEL, D_FF), BF16),
                      jax.ShapeDtypeStruct((D_MODEL, D_FF), BF16),
                      jax.ShapeDtypeStruct((D_FF, D_MODEL), BF16)]
    outs = pl.pallas_call(
        functools.partial(_ffn_kernel, emit_bf16=emit_bf16, row_splits=max(1, tm // FFN_ROW_GROUP)),
        grid=(m // tm, nf),
        in_specs=[
            pl.BlockSpec((tm, D_MODEL), lambda i, j: (i, 0)),
            pl.BlockSpec((D_MODEL, tf), lambda i, j: (0, j)),
            pl.BlockSpec((D_MODEL, tf), lambda i, j: (0, j + u_off)),
            pl.BlockSpec((tf, D_MODEL), lambda i, j: (j, 0)),
            pl.BlockSpec((1, D_MODEL), lambda i, j: (0, 0)),
            pl.BlockSpec((1, D_MODEL), lambda i, j: (0, 0)),
        ],
        out_specs=out_specs,
        out_shape=out_shape,
        scratch_shapes=[pltpu.VMEM((tm, D_MODEL), BF16)],
        compiler_params=_params(("arbitrary", "arbitrary"), 62),
        name="ffn_ln",
    )(x, w_g, w_u, w_down, ln_g, ln_b)
    return outs if emit_bf16 else outs[0]


def _memkv_kernel(x_ref, wk_ref, wv_ref, g_ref, b_ref, k_ref, v_ref, xb_ref):
    @pl.when(pl.program_id(1) == 0)
    def _():
        xb_ref[...] = _layer_norm(x_ref[...], g_ref[...], b_ref[...]).astype(BF16)

    xb = xb_ref[...]
    k_ref[...] = _dot(xb, _mxu_operand(wk_ref))
    v_ref[...] = _dot(xb, _mxu_operand(wv_ref))


def _memkv(mem, w_kv, ln_g, ln_b, *, tm=1024, tn=512):
    m = mem.shape[0]
    nn = BR_W // tn
    return pl.pallas_call(
        _memkv_kernel,
        grid=(m // tm, nn),
        in_specs=[
            pl.BlockSpec((tm, D_MODEL), lambda i, j: (i, 0)),
            pl.BlockSpec((D_MODEL, tn), lambda i, j: (0, j)),
            pl.BlockSpec((D_MODEL, tn), lambda i, j: (0, j + nn)),
            pl.BlockSpec((1, D_MODEL), lambda i, j: (0, 0)),
            pl.BlockSpec((1, D_MODEL), lambda i, j: (0, 0)),
        ],
        out_specs=[pl.BlockSpec((tm, tn), lambda i, j: (i, j)),
                   pl.BlockSpec((tm, tn), lambda i, j: (i, j))],
        out_shape=[jax.ShapeDtypeStruct((m, BR_W), F32), jax.ShapeDtypeStruct((m, BR_W), F32)],
        scratch_shapes=[pltpu.VMEM((tm, D_MODEL), BF16)],
        compiler_params=_params(("arbitrary", "arbitrary"), 48),
        name="mem_kv",
    )(mem, w_kv, w_kv, ln_g, ln_b)


def _inproj_kernel(x_ref, w_ref, gb_ref, lg_ref, lb_ref, o_ref, rx_ref, *rest, emit_bf16):
    if emit_bf16:
        w16_ref, xb_ref = rest
        _mxu_operand(w_ref, w16_ref)
        w_ref = w16_ref
    else:
        xb_ref, = rest
    j = pl.program_id(1)

    @pl.when(j == 0)
    def _():
        xb_ref[...] = x_ref[...].astype(BF16)

    def emit(dst_ref, act):
        dst_ref[...] = act(_dot(xb_ref[...], w_ref[...])).astype(dst_ref.dtype)

    @pl.when((j == 0) | (j == 3))
    def _():
        emit(o_ref, jax.nn.gelu)

    @pl.when(j == 1)
    def _():
        emit(o_ref, lambda z: _layer_norm(jax.nn.gelu(z), lg_ref[...], lb_ref[...]))

    @pl.when(j == 2)
    def _():
        emit(rx_ref, lambda z: z)

    @pl.when(j == 4)
    def _():
        emit(o_ref, lambda z: z)

    @pl.when(j >= GATE_BLOCK0)
    def _():
        emit(o_ref, lambda z: _sigmoid(z + gb_ref[...]))


def _inproj(x, w_in, gate_b, gln_g, gln_b, *, tm, z_dtype, emit_bf16=False):
    m = x.shape[0]
    assert not emit_bf16 or m == tm
    out_specs = [pl.BlockSpec((tm, BR_W), lambda i, j: (i, jnp.where(j < RX_BLOCK, j, j - 1))),
                 pl.BlockSpec((tm, BR_W), lambda i, j: (i, 0))]
    out_shape = [jax.ShapeDtypeStruct((m, Z_BLOCKS * BR_W), z_dtype),
                 jax.ShapeDtypeStruct((m, BR_W), F32)]
    if emit_bf16:
        out_specs.append(pl.BlockSpec((D_MODEL, BR_W), lambda i, j: (0, j)))
        out_shape.append(jax.ShapeDtypeStruct((D_MODEL, IN_BLOCKS * BR_W), BF16))
    return pl.pallas_call(
        functools.partial(_inproj_kernel, emit_bf16=emit_bf16),
        grid=(m // tm, IN_BLOCKS),
        in_specs=[
            pl.BlockSpec((tm, D_MODEL), lambda i, j: (i, 0)),
            pl.BlockSpec((D_MODEL, BR_W), lambda i, j: (0, j)),
            pl.BlockSpec((1, BR_W), lambda i, j: (0, jnp.maximum(j - GATE_BLOCK0, 0))),
            pl.BlockSpec((1, BR_W), lambda i, j: (0, 0)),
            pl.BlockSpec((1, BR_W), lambda i, j: (0, 0)),
        ],
        out_specs=out_specs,
        out_shape=out_shape,
        scratch_shapes=[pltpu.VMEM((tm, D_MODEL), BF16)],
        compiler_params=_params(("arbitrary", "arbitrary"), 48),
        name="in_proj",
    )(x, w_in, gate_b, gln_g, gln_b)


def _softplus(x):
    return jnp.maximum(x, 0.0) + jnp.log1p(jnp.exp(-jnp.abs(x)))


def _lru_coeffs(xc, wa_ref, ba_ref, wx_ref, bx_ref, lam_ref):
    xcb = xc.astype(BF16)
    r_parts, i_parts = [], []
    for k in range(LRU_BLOCKS):
        xk = xcb[:, k * LRU_BLOCK:(k + 1) * LRU_BLOCK]
        r_parts.append(_dot(xk, wa_ref[k]))
        i_parts.append(_dot(xk, wx_ref[k]))
    r = _sigmoid(jnp.concatenate(r_parts, axis=1) + ba_ref[...])
    i = _sigmoid(jnp.concatenate(i_parts, axis=1) + bx_ref[...])
    log_a = (-LRU_C) * r * _softplus(-lam_ref[...])
    a = jnp.exp(log_a)
    m = -jnp.tanh(log_a) * (a * a + 1.0)
    root = jnp.where(m > 0.0, m * lax.rsqrt(m), 0.0)
    return a, root * (i * xc)


def _mixer_kernel(gu_ref, v_ref, rx_ref, grg_ref, q_ref, k_ref, vm_ref,
                  ws_ref, bs_ref, cw_ref, cb_ref, wa_ref, ba_ref, wx_ref, bx_ref, lam_ref,
                  y_ref, conv_ref, hlast_ref,
                  xpad_ref, a_ref, b_ref, h_ref, hc_ref, *, tm):
    t = pl.program_id(1)

    tri = (lax.broadcasted_iota(jnp.int32, (CHUNK, CHUNK), 0)
           >= lax.broadcasted_iota(jnp.int32, (CHUNK, CHUNK), 1))
    for g in range(GMLP_GROUPS):
        wg = jnp.where(tri, ws_ref[g], 0.0).astype(BF16)
        cols = slice(g * GROUP_W, (g + 1) * GROUP_W)
        for c in range(tm // CHUNK):
            rows = slice(c * CHUNK, (c + 1) * CHUNK)
            s = _dot(wg, v_ref[rows, cols].astype(BF16)) + bs_ref[:, cols]
            y_ref[0, rows, cols] = (gu_ref[rows, cols] * s).astype(BF16)

    @pl.when(t == 0)
    def _():
        xpad_ref[0:SUBLANES, :] = jnp.zeros((SUBLANES, BR_W), F32)
        hc_ref[...] = jnp.zeros((1, BR_W), F32)

    @pl.when(t > 0)
    def _():
        xpad_ref[0:SUBLANES, :] = xpad_ref[tm:tm + SUBLANES, :]

    xpad_ref[SUBLANES:SUBLANES + tm, :] = rx_ref[...]
    xc = cb_ref[...]
    for k in range(CONV_W):
        off = SUBLANES - (CONV_W - 1) + k
        xc = xc + xpad_ref[off:off + tm, :] * cw_ref[k:k + 1, :]

    a, b = _lru_coeffs(xc, wa_ref, ba_ref, wx_ref, bx_ref, lam_ref)
    a_ref[...] = a
    b_ref[...] = b

    row = lax.broadcasted_iota(jnp.int32, (SUBLANES, BR_W), 0)
    keep = [row >= d for d in (1, 2, 4)]

    def scan_block(blk, h):
        base = pl.multiple_of(blk * SUBLANES, SUBLANES)
        ca = a_ref[pl.ds(base, SUBLANES), :]
        cb = b_ref[pl.ds(base, SUBLANES), :]
        for d, kp in zip((1, 2, 4), keep):
            a_sh = jnp.where(kp, pltpu.roll(ca, d, 0), 1.0)
            b_sh = jnp.where(kp, pltpu.roll(cb, d, 0), 0.0)
            cb = ca * b_sh + cb
            ca = ca * a_sh
        hh = cb + ca * h
        h_ref[pl.ds(base, SUBLANES), :] = hh
        return hh[SUBLANES - 1:SUBLANES, :]

    h_end = lax.fori_loop(0, tm // SUBLANES, scan_block, hc_ref[...])
    hc_ref[...] = h_end
    y_ref[1] = (grg_ref[...] * h_ref[...]).astype(BF16)

    @pl.when(t == pl.num_programs(1) - 1)
    def _():
        conv_ref[0] = rx_ref[tm - (CONV_W - 1):tm, :]
        hlast_ref[0] = h_end

    scale = XA_HEAD_DIM ** -0.5
    for hd in range(XA_HEADS):
        cols = slice(hd * XA_HEAD_DIM, (hd + 1) * XA_HEAD_DIM)
        qh = q_ref[:, cols].astype(BF16)
        kh = k_ref[0, :, cols].astype(BF16)
        vh = vm_ref[0, :, cols].astype(BF16)
        s = lax.dot_general(qh, kh, (((1,), (1,)), ((), ())), preferred_element_type=F32) * scale
        e = jnp.exp(s - jnp.max(s, axis=-1, keepdims=True))
        p = e * (1.0 / jnp.sum(e, axis=-1, keepdims=True))
        y_ref[2, :, cols] = _dot(p.astype(BF16), vh).astype(BF16)


def _mixer_prompt(z, rx, k_mem, v_mem, ws, bs_mat, conv_w, conv_b, wa, ba, wx, bx, lam, *, bsz, seq, tm=256):
    nt = seq // tm
    zspec = lambda c: pl.BlockSpec((tm, BR_W), lambda b, t, c=c: (b * nt + t, c))
    full = lambda shape: pl.BlockSpec(shape, lambda b, t: (0,) * len(shape))
    return pl.pallas_call(
        functools.partial(_mixer_kernel, tm=tm),
        grid=(bsz, nt),
        in_specs=[zspec(Z_GU), zspec(Z_V), zspec(0), zspec(Z_GRG), zspec(Z_Q),
                  pl.BlockSpec((1, N_MEM, BR_W), lambda b, t: (b, 0, 0)),
                  pl.BlockSpec((1, N_MEM, BR_W), lambda b, t: (b, 0, 0)),
                  full((GMLP_GROUPS, CHUNK, CHUNK)), full((CHUNK, BR_W)),
                  full((CONV_W, BR_W)), full((1, BR_W)),
                  full((LRU_BLOCKS, LRU_BLOCK, LRU_BLOCK)), full((1, BR_W)),
                  full((LRU_BLOCKS, LRU_BLOCK, LRU_BLOCK)), full((1, BR_W)),
                  full((1, BR_W))],
        out_specs=[pl.BlockSpec((3, tm, BR_W), lambda b, t: (0, b * nt + t, 0)),
                   pl.BlockSpec((1, CONV_W - 1, BR_W), lambda b, t: (b, 0, 0)),
                   pl.BlockSpec((1, 1, BR_W), lambda b, t: (b, 0, 0))],
        out_shape=[jax.ShapeDtypeStruct((3, bsz * seq, BR_W), BF16),
                   jax.ShapeDtypeStruct((bsz, CONV_W - 1, BR_W), F32),
                   jax.ShapeDtypeStruct((bsz, 1, BR_W), F32)],
        scratch_shapes=[pltpu.VMEM((tm + SUBLANES, BR_W), F32),
                        pltpu.VMEM((tm, BR_W), F32), pltpu.VMEM((tm, BR_W), F32),
                        pltpu.VMEM((tm, BR_W), F32), pltpu.VMEM((1, BR_W), F32)],
        compiler_params=_params(("arbitrary", "arbitrary"), 40),
        name="mixer_prompt",
    )(z, z, rx, z, z, k_mem, v_mem, ws, bs_mat, conv_w, conv_b, wa, ba, wx, bx, lam)


def _mixer_sample_kernel(gu_ref, v_ref, rx_ref, grg_ref, cs_ref, h0_ref,
                         wv_ref, bv_ref, cw_ref, cb_ref, wa_ref, ba_ref, wx_ref, bx_ref, lam_ref,
                         y_ref, conv_ref, h_ref):
    y_ref[0] = (gu_ref[...] * (wv_ref[...] * v_ref[...] + bv_ref[...])).astype(BF16)

    rx = rx_ref[...]
    xc = cb_ref[...] + rx * cw_ref[CONV_W - 1:CONV_W, :]
    for k in range(CONV_W - 1):
        xc = xc + cs_ref[k] * cw_ref[k:k + 1, :]
    for k in range(CONV_W - 2):
        conv_ref[k] = cs_ref[k + 1]
    conv_ref[CONV_W - 2] = rx

    a, b = _lru_coeffs(xc, wa_ref, ba_ref, wx_ref, bx_ref, lam_ref)
    h = a * h0_ref[...] + b
    h_ref[...] = h
    y_ref[1] = (grg_ref[...] * h).astype(BF16)


def _mixer_sample(z, rx, conv_state, h0, wvec, bvec, conv_w, conv_b, wa, ba, wx, bx, lam):
    n = z.shape[0]
    zspec = lambda c: pl.BlockSpec((n, BR_W), lambda i, c=c: (0, c))
    full = lambda shape: pl.BlockSpec(shape, lambda i: (0,) * len(shape))
    return pl.pallas_call(
        _mixer_sample_kernel,
        grid=(1,),
        in_specs=[zspec(Z_GU), zspec(Z_V), zspec(0), zspec(Z_GRG),
                  full((CONV_W - 1, n, BR_W)), full((n, BR_W)),
                  full((1, BR_W)), full((1, BR_W)), full((CONV_W, BR_W)), full((1, BR_W)),
                  full((LRU_BLOCKS, LRU_BLOCK, LRU_BLOCK)), full((1, BR_W)),
                  full((LRU_BLOCKS, LRU_BLOCK, LRU_BLOCK)), full((1, BR_W)),
                  full((1, BR_W))],
        out_specs=[full((2, n, BR_W)), full((CONV_W - 1, n, BR_W)), full((n, BR_W))],
        out_shape=[jax.ShapeDtypeStruct((2, n, BR_W), BF16),
                   jax.ShapeDtypeStruct((CONV_W - 1, n, BR_W), F32),
                   jax.ShapeDtypeStruct((n, BR_W), F32)],
        compiler_params=_params(("arbitrary",), 32),
        name="mixer_sample",
    )(z, z, rx, z, conv_state, h0, wvec, bvec, conv_w, conv_b, wa, ba, wx, bx, lam)


HEAD_SPLIT = XA_HEAD_DIM // 128
HEAD_ROWS = HEAD_SPLIT * XA_HEADS


def _head_rows(x):
    lead = x.shape[:-2]
    x = x.reshape(*lead, XA_HEADS, HEAD_SPLIT, 128)
    return jnp.swapaxes(x, -2, -3).reshape(*lead, HEAD_ROWS, 128)


def _xattn_sample_kernel(q_ref, k_ref, v_ref, o_ref, *, tb):
    scale = XA_HEAD_DIM ** -0.5
    n_rows = N_MEM * HEAD_ROWS
    lane = lax.broadcasted_iota(jnp.int32, (HEAD_ROWS, n_rows), 1)
    sub = lax.broadcasted_iota(jnp.int32, (HEAD_ROWS, n_rows), 0)
    own = (lane % HEAD_ROWS) == sub
    low_piece = (lax.broadcasted_iota(jnp.int32, (1, n_rows), 1) % HEAD_ROWS) < XA_HEADS

    def lane_group_reduce(x, op):
        shift = HEAD_ROWS
        while shift < 128:
            x = op(x, pltpu.roll(x, shift, 1))
            shift *= 2
        return x

    def per_kind(x, op):
        acc = x[:, 0:128]
        for c in range(1, n_rows // 128):
            acc = op(acc, x[:, c * 128:(c + 1) * 128])
        acc = lane_group_reduce(acc, op)
        return jnp.concatenate([acc] * (n_rows // 128), axis=1)

    parts = []
    for b in range(tb):
        qb = q_ref[b].astype(BF16)
        kb = k_ref[b].reshape(n_rows, 128).astype(BF16)
        c = lax.dot_general(qb, kb, (((1,), (1,)), ((), ())), preferred_element_type=F32)
        parts.append(jnp.sum(jnp.where(own, c, 0.0), axis=0, keepdims=True))
    part = jnp.concatenate(parts, axis=0)
    other = jnp.where(low_piece, pltpu.roll(part, n_rows - XA_HEADS, 1), pltpu.roll(part, XA_HEADS, 1))
    s = (part + other) * scale
    e = jnp.exp(s - per_kind(s, jnp.maximum))
    p = e * (1.0 / per_kind(e, jnp.add))
    for b in range(tb):
        vb = v_ref[b].reshape(n_rows, 128).astype(BF16)
        pm = jnp.where(own, jnp.broadcast_to(p[b:b + 1, :], (HEAD_ROWS, n_rows)), 0.0).astype(BF16)
        o_ref[b] = _dot(pm, vb).astype(BF16)


def _xattn_sample(q, k_cache, v_cache, *, tb=8):
    n = q.shape[0]
    kv_spec = pl.BlockSpec((tb, N_MEM, HEAD_ROWS, 128), lambda i: (i, 0, 0, 0))
    q_spec = pl.BlockSpec((tb, HEAD_ROWS, 128), lambda i: (i, 0, 0))
    return pl.pallas_call(
        functools.partial(_xattn_sample_kernel, tb=tb),
        grid=(n // tb,),
        in_specs=[q_spec, kv_spec, kv_spec],
        out_specs=q_spec,
        out_shape=jax.ShapeDtypeStruct((n, HEAD_ROWS, 128), BF16),
        compiler_params=_params(("arbitrary",), 48),
        name="xattn_sample",
    )(q, k_cache, v_cache)


N_BRANCH = 3


def _merge_kernel(y_ref, wb_ref, g0_ref, g1_ref, g2_ref, o_ref, wb16_ref=None):
    acc = None
    for k, gate_ref in enumerate((g0_ref, g1_ref, g2_ref)):
        wk = _mxu_operand(wb_ref.at[k], None if wb16_ref is None else wb16_ref.at[k])
        term = gate_ref[...].astype(F32) * _dot(y_ref[k], wk)
        acc = term if acc is None else acc + term
    o_ref[...] = acc.astype(BF16)


def _merge(y, w_branch, z, *, tm, emit_bf16=False):
    m = y.shape[1]
    assert not emit_bf16 or m == tm
    gate_spec = lambda k: pl.BlockSpec((tm, BR_W), lambda i, n, k=k: (i, Z_GATE0 + 2 * k + n))
    w_spec = pl.BlockSpec((N_BRANCH, BR_W, BR_W), lambda i, n: (0, 0, n))
    out_specs = [pl.BlockSpec((tm, BR_W), lambda i, n: (i, n))]
    out_shape = [jax.ShapeDtypeStruct((m, D_MODEL), BF16)]
    if emit_bf16:
        out_specs.append(w_spec)
        out_shape.append(jax.ShapeDtypeStruct(w_branch.shape, BF16))
    outs = pl.pallas_call(
        _merge_kernel,
        grid=(m // tm, D_MODEL // BR_W),
        in_specs=[
            pl.BlockSpec((N_BRANCH, tm, BR_W), lambda i, n: (0, i, 0)),
            w_spec,
            gate_spec(0), gate_spec(1), gate_spec(2),
        ],
        out_specs=out_specs,
        out_shape=out_shape,
        compiler_params=_params(("arbitrary", "arbitrary"), 56),
        name="merge",
    )(y, w_branch, z, z, z)
    return outs if emit_bf16 else outs[0]


def _outproj_kernel(m_ref, wo_ref, x_ref, g_ref, b_ref, o_ref, *, row_splits):
    rows_per = m_ref.shape[0] // row_splits
    for r in range(row_splits):
        rows = slice(r * rows_per, (r + 1) * rows_per)
        y = ALPHA * x_ref[rows, :] + _dot(m_ref[rows, :], wo_ref[...])
        o_ref[rows, :] = _layer_norm(y, g_ref[...], b_ref[...])


def _outproj(merged, w_out, x, ln_g, ln_b, *, tm):
    m = x.shape[0]
    return pl.pallas_call(
        functools.partial(_outproj_kernel, row_splits=2 if tm >= 512 else 1),
        grid=(m // tm,),
        in_specs=[
            pl.BlockSpec((tm, D_MODEL), lambda i: (i, 0)),
            pl.BlockSpec((D_MODEL, D_MODEL), lambda i: (0, 0)),
            pl.BlockSpec((tm, D_MODEL), lambda i: (i, 0)),
            pl.BlockSpec((1, D_MODEL), lambda i: (0, 0)),
            pl.BlockSpec((1, D_MODEL), lambda i: (0, 0)),
        ],
        out_specs=pl.BlockSpec((tm, D_MODEL), lambda i: (i, 0)),
        out_shape=jax.ShapeDtypeStruct((m, D_MODEL), F32),
        compiler_params=_params(("arbitrary",), 48),
        name="out_proj",
    )(merged, w_out, x, ln_g, ln_b)


def kernel(x_prompt, x_sample, mem_prompt, cache_mem_k, cache_mem_v, state_conv, state_lru_h, ffn1_w_gu, ffn1_w_down, ln1_g, ln1_b, w_in, gate_b, gmlp_ln_g, gmlp_ln_b, gmlp_w_s, gmlp_b_s, conv_w, conv_b, lru_w_a, lru_b_a, lru_w_x, lru_b_x, lru_lambda, mem_ln_g, mem_ln_b, w_mem_kv, w_branch, w_out, ln2_g, ln2_b, ffn2_w_gu, ffn2_w_down, ln3_g, ln3_b):
    bsz, seq, _ = x_prompt.shape
    n_s = x_sample.shape[0]
    l = 0

    wout = w_out[l].astype(BF16)
    wa, wx = lru_w_a[l].astype(BF16), lru_w_x[l].astype(BF16)

    row = lambda p: p[l].reshape(1, -1)
    gb = gate_b[l].reshape(1, -1)
    bs_mat = jnp.repeat(gmlp_b_s[l].T, GROUP_W, axis=1)
    wvec = jnp.repeat(gmlp_w_s[l][:, 0, 0], GROUP_W).reshape(1, -1)
    bvec = jnp.repeat(gmlp_b_s[l][:, 0], GROUP_W).reshape(1, -1)
    lru = (wa, row(lru_b_a), wx, row(lru_b_x), row(lru_lambda))

    xp = x_prompt.reshape(bsz * seq, D_MODEL)
    xs = x_sample.reshape(n_s, D_MODEL)

    x1s, w1g, w1u, w1d = _ffn(xs, ffn1_w_gu[l], ffn1_w_gu[l], ffn1_w_down[l], row(ln1_g), row(ln1_b),
                              tm=n_s, emit_bf16=True)
    zs, rxs, win = _inproj(x1s, w_in[l], gb, row(gmlp_ln_g), row(gmlp_ln_b), tm=n_s, z_dtype=F32,
                           emit_bf16=True)
    y01, conv_s, h_s = _mixer_sample(
        zs, rxs, jnp.swapaxes(state_conv[l], 0, 1), state_lru_h[l], wvec, bvec, conv_w[l], row(conv_b), *lru)
    q_s = zs[:, Z_Q * BR_W:(Z_Q + 1) * BR_W].reshape(n_s, XA_HEADS, XA_HEAD_DIM)
    yxa = _xattn_sample(_head_rows(q_s), _head_rows(cache_mem_k[l]), _head_rows(cache_mem_v[l]))
    yxa = jnp.swapaxes(yxa.reshape(n_s, HEAD_SPLIT, XA_HEADS, 128), 1, 2)
    ys = jnp.concatenate([y01, yxa.reshape(1, n_s, BR_W)], axis=0)
    ms, wbr = _merge(ys, w_branch[l], zs, tm=n_s, emit_bf16=True)
    x2s = _outproj(ms, wout, x1s, row(ln2_g), row(ln2_b), tm=n_s)
    x3s, w2g, w2u, w2d = _ffn(x2s, ffn2_w_gu[l], ffn2_w_gu[l], ffn2_w_down[l], row(ln3_g), row(ln3_b),
                              tm=n_s, emit_bf16=True)

    k_p, v_p = _memkv(mem_prompt.reshape(bsz * N_MEM, D_MODEL), w_mem_kv[l], row(mem_ln_g), row(mem_ln_b))
    x1p = _ffn(xp, w1g, w1u, w1d, row(ln1_g), row(ln1_b), tm=1024)
    zp, rxp = _inproj(x1p, win, gb, row(gmlp_ln_g), row(gmlp_ln_b), tm=1024, z_dtype=BF16)
    yp, conv_p, hlast_p = _mixer_prompt(
        zp, rxp, k_p.reshape(bsz, N_MEM, BR_W), v_p.reshape(bsz, N_MEM, BR_W), gmlp_w_s[l], bs_mat,
        conv_w[l], row(conv_b), *lru, bsz=bsz, seq=seq)
    mp = _merge(yp, wbr, zp, tm=1024)
    x2p = _outproj(mp, wout, x1p, row(ln2_g), row(ln2_b), tm=512)
    x3p = _ffn(x2p, w2g, w2u, w2d, row(ln3_g), row(ln3_b), tm=1024)

    kv_shape = (1, bsz, N_MEM, XA_HEADS, XA_HEAD_DIM)
    return (x3p.reshape(bsz, seq, D_MODEL),
            x3s.reshape(n_s, 1, D_MODEL),
            k_p.reshape(kv_shape),
            v_p.reshape(kv_shape),
            conv_p[None],
            hlast_p.reshape(1, bsz, BR_W),
            jnp.swapaxes(conv_s, 0, 1)[None],
            h_s[None],
            zs[:, Z_V * BR_W:(Z_V + 1) * BR_W].reshape(1, n_s, 1, BR_W))
```

```python
import functools

import jax
import jax.numpy as jnp
from jax import lax
from jax.experimental import pallas as pl
from jax.experimental.pallas import tpu as pltpu

F32 = jnp.float32
BF16 = jnp.bfloat16

D_MODEL = 2048
BR_W = 1024
D_FF = 5632
N_MEM = 256
XA_HEADS = 4
XA_HEAD_DIM = 256
GMLP_GROUPS = 4
GROUP_W = BR_W // GMLP_GROUPS
CHUNK = 128
LRU_BLOCKS = 8
LRU_BLOCK = 128
CONV_W = 4
LRU_C = 8.0
LN_EPS = 1e-5
ALPHA = 2.0 ** 0.25
IN_BLOCKS = 11
RX_BLOCK = 2
GATE_BLOCK0 = 5
Z_BLOCKS = IN_BLOCKS - 1
Z_GU, Z_V, Z_GRG, Z_Q, Z_GATE0 = 0, 1, 2, 3, 4
SUBLANES = 8
FFN_ROW_GROUP = 512
OUTPROJ_ROW_GROUP = 128
MiB = 1024 * 1024


def _params(semantics, vmem_mib):
    return pltpu.CompilerParams(dimension_semantics=semantics, vmem_limit_bytes=vmem_mib * MiB)


def _layer_norm(x, g, b):
    mu = jnp.mean(x, axis=-1, keepdims=True)
    xc = x - mu
    var = jnp.mean(xc * xc, axis=-1, keepdims=True)
    return xc * lax.rsqrt(var + LN_EPS) * g + b


def _dot(a, b):
    return jnp.dot(a, b, preferred_element_type=F32)


def _sigmoid(x):
    return 0.5 * jnp.tanh(0.5 * x) + 0.5


def _mxu_operand(w_ref, w16_ref=None):
    w = w_ref[...].astype(BF16)
    if w16_ref is not None:
        w16_ref[...] = w
    return w


def _side_cast_plan(side, grid):
    n_steps = 1
    for g in grid:
        n_steps *= g

    def linear(*idx):
        lin = idx[0]
        for g, i in zip(grid[1:], idx[1:]):
            lin = lin * g + i
        return lin

    specs, shapes = [], []
    for w in side:
        r, c = w.shape
        if c % (n_steps * 128) == 0:
            spec = pl.BlockSpec((r, c // n_steps), lambda *idx: (0, linear(*idx)))
        else:
            assert r % (n_steps * 16) == 0, (w.shape, n_steps)
            spec = pl.BlockSpec((r // n_steps, c), lambda *idx: (linear(*idx), 0))
        specs.append(spec)
        shapes.append(jax.ShapeDtypeStruct(w.shape, BF16))
    return specs, shapes


def _side_cast(in_refs, out_refs):
    for src, dst in zip(in_refs, out_refs):
        dst[...] = src[...].astype(BF16)


def _ffn_kernel(x_ref, wg_ref, wu_ref, wd_ref, g_ref, b_ref, *rest, emit_bf16, row_splits, n_side):
    side_in, rest = rest[:n_side], rest[n_side:]
    o_ref, rest = rest[0], rest[1:]
    if emit_bf16:
        (wg16_ref, wu16_ref, wd16_ref), rest = rest[:3], rest[3:]
    else:
        wg16_ref = wu16_ref = wd16_ref = None
    side_out, (xb_ref,) = rest[:n_side], rest[n_side:]
    _side_cast(side_in, side_out)
    j = pl.program_id(1)
    last = pl.num_programs(1) - 1
    rows_per = xb_ref.shape[0] // row_splits

    def step(first, final):
        wg = _mxu_operand(wg_ref, wg16_ref)
        wu = _mxu_operand(wu_ref, wu16_ref)
        wd = _mxu_operand(wd_ref, wd16_ref)
        for r in range(row_splits):
            rows = slice(r * rows_per, (r + 1) * rows_per)
            if first:
                xb = x_ref[rows, :].astype(BF16)
                xb_ref[rows, :] = xb
            else:
                xb = xb_ref[rows, :]
            g = _dot(xb, wg)
            u = _dot(xb, wu)
            h = (g * _sigmoid(g) * u).astype(BF16)
            acc = _dot(h, wd)
            if not first:
                acc = o_ref[rows, :] + acc
            if final:
                acc = _layer_norm(ALPHA * x_ref[rows, :] + 0.5 * acc, g_ref[...], b_ref[...])
            o_ref[rows, :] = acc

    pl.when(j == 0)(functools.partial(step, True, False))
    pl.when((j > 0) & (j < last))(functools.partial(step, False, False))
    pl.when(j == last)(functools.partial(step, False, True))


def _ffn(x, w_g, w_u, w_down, ln_g, ln_b, *, tm, emit_bf16=False, side=(), tf=512):
    m = x.shape[0]
    nf = D_FF // tf
    grid = (m // tm, nf)
    u_off = nf if w_u.shape[1] == 2 * D_FF else 0
    assert not emit_bf16 or m == tm
    side_specs, side_shapes = _side_cast_plan(side, grid)
    out_specs = [pl.BlockSpec((tm, D_MODEL), lambda i, j: (i, 0))]
    out_shape = [jax.ShapeDtypeStruct((m, D_MODEL), F32)]
    if emit_bf16:
        out_specs += [pl.BlockSpec((D_MODEL, tf), lambda i, j: (0, j)),
                      pl.BlockSpec((D_MODEL, tf), lambda i, j: (0, j)),
                      pl.BlockSpec((tf, D_MODEL), lambda i, j: (j, 0))]
        out_shape += [jax.ShapeDtypeStruct((D_MODEL, D_FF), BF16),
                      jax.ShapeDtypeStruct((D_MODEL, D_FF), BF16),
                      jax.ShapeDtypeStruct((D_FF, D_MODEL), BF16)]
    outs = pl.pallas_call(
        functools.partial(_ffn_kernel, emit_bf16=emit_bf16, row_splits=max(1, tm // FFN_ROW_GROUP),
                          n_side=len(side)),
        grid=grid,
        in_specs=[
            pl.BlockSpec((tm, D_MODEL), lambda i, j: (i, 0)),
            pl.BlockSpec((D_MODEL, tf), lambda i, j: (0, j)),
            pl.BlockSpec((D_MODEL, tf), lambda i, j: (0, j + u_off)),
            pl.BlockSpec((tf, D_MODEL), lambda i, j: (j, 0)),
            pl.BlockSpec((1, D_MODEL), lambda i, j: (0, 0)),
            pl.BlockSpec((1, D_MODEL), lambda i, j: (0, 0)),
        ] + side_specs,
        out_specs=out_specs + side_specs,
        out_shape=out_shape + side_shapes,
        scratch_shapes=[pltpu.VMEM((tm, D_MODEL), BF16)],
        compiler_params=_params(("arbitrary", "arbitrary"), 62),
        name="ffn_ln",
    )(x, w_g, w_u, w_down, ln_g, ln_b, *side)
    return outs if (emit_bf16 or side) else outs[0]


def _memkv_kernel(x_ref, wk_ref, wv_ref, g_ref, b_ref, k_ref, v_ref, xb_ref):
    @pl.when(pl.program_id(1) == 0)
    def _():
        xb_ref[...] = _layer_norm(x_ref[...], g_ref[...], b_ref[...]).astype(BF16)

    xb = xb_ref[...]
    k_ref[...] = _dot(xb, _mxu_operand(wk_ref))
    v_ref[...] = _dot(xb, _mxu_operand(wv_ref))


def _memkv(mem, w_kv, ln_g, ln_b, *, tm=1024, tn=512):
    m = mem.shape[0]
    nn = BR_W // tn
    return pl.pallas_call(
        _memkv_kernel,
        grid=(m // tm, nn),
        in_specs=[
            pl.BlockSpec((tm, D_MODEL), lambda i, j: (i, 0)),
            pl.BlockSpec((D_MODEL, tn), lambda i, j: (0, j)),
            pl.BlockSpec((D_MODEL, tn), lambda i, j: (0, j + nn)),
            pl.BlockSpec((1, D_MODEL), lambda i, j: (0, 0)),
            pl.BlockSpec((1, D_MODEL), lambda i, j: (0, 0)),
        ],
        out_specs=[pl.BlockSpec((tm, tn), lambda i, j: (i, j)),
                   pl.BlockSpec((tm, tn), lambda i, j: (i, j))],
        out_shape=[jax.ShapeDtypeStruct((m, BR_W), F32), jax.ShapeDtypeStruct((m, BR_W), F32)],
        scratch_shapes=[pltpu.VMEM((tm, D_MODEL), BF16)],
        compiler_params=_params(("arbitrary", "arbitrary"), 48),
        name="mem_kv",
    )(mem, w_kv, w_kv, ln_g, ln_b)


def _inproj_kernel(x_ref, w_ref, gb_ref, lg_ref, lb_ref, *rest, n_side):
    side_in, rest = rest[:n_side], rest[n_side:]
    (o_ref, rx_ref), rest = rest[:2], rest[2:]
    side_out, (xb_ref,) = rest[:n_side], rest[n_side:]
    _side_cast(side_in, side_out)
    j = pl.program_id(1)

    @pl.when(j == 0)
    def _():
        xb_ref[...] = x_ref[...].astype(BF16)

    def emit(dst_ref, act):
        dst_ref[...] = act(_dot(xb_ref[...], w_ref[...])).astype(dst_ref.dtype)

    @pl.when((j == 0) | (j == 3))
    def _():
        emit(o_ref, jax.nn.gelu)

    @pl.when(j == 1)
    def _():
        emit(o_ref, lambda z: _layer_norm(jax.nn.gelu(z), lg_ref[...], lb_ref[...]))

    @pl.when(j == 2)
    def _():
        emit(rx_ref, lambda z: z)

    @pl.when(j == 4)
    def _():
        emit(o_ref, lambda z: z)

    @pl.when(j >= GATE_BLOCK0)
    def _():
        emit(o_ref, lambda z: _sigmoid(z + gb_ref[...]))


def _inproj(x, w_in, gate_b, gln_g, gln_b, *, tm, z_dtype, side=()):
    m = x.shape[0]
    grid = (m // tm, IN_BLOCKS)
    side_specs, side_shapes = _side_cast_plan(side, grid)
    out_specs = [pl.BlockSpec((tm, BR_W), lambda i, j: (i, jnp.where(j < RX_BLOCK, j, j - 1))),
                 pl.BlockSpec((tm, BR_W), lambda i, j: (i, 0))]
    out_shape = [jax.ShapeDtypeStruct((m, Z_BLOCKS * BR_W), z_dtype),
                 jax.ShapeDtypeStruct((m, BR_W), F32)]
    return pl.pallas_call(
        functools.partial(_inproj_kernel, n_side=len(side)),
        grid=grid,
        in_specs=[
            pl.BlockSpec((tm, D_MODEL), lambda i, j: (i, 0)),
            pl.BlockSpec((D_MODEL, BR_W), lambda i, j: (0, j)),
            pl.BlockSpec((1, BR_W), lambda i, j: (0, jnp.maximum(j - GATE_BLOCK0, 0))),
            pl.BlockSpec((1, BR_W), lambda i, j: (0, 0)),
            pl.BlockSpec((1, BR_W), lambda i, j: (0, 0)),
        ] + side_specs,
        out_specs=out_specs + side_specs,
        out_shape=out_shape + side_shapes,
        scratch_shapes=[pltpu.VMEM((tm, D_MODEL), BF16)],
        compiler_params=_params(("arbitrary", "arbitrary"), 56),
        name="in_proj",
    )(x, w_in, gate_b, gln_g, gln_b, *side)


def _softplus(x):
    return jnp.maximum(x, 0.0) + jnp.log1p(jnp.exp(-jnp.abs(x)))


def _lru_coeffs(xc, wa_ref, ba_ref, wx_ref, bx_ref, lam_ref):
    xcb = xc.astype(BF16)
    r_parts, i_parts = [], []
    for k in range(LRU_BLOCKS):
        xk = xcb[:, k * LRU_BLOCK:(k + 1) * LRU_BLOCK]
        r_parts.append(_dot(xk, wa_ref[k]))
        i_parts.append(_dot(xk, wx_ref[k]))
    r = _sigmoid(jnp.concatenate(r_parts, axis=1) + ba_ref[...])
    i = _sigmoid(jnp.concatenate(i_parts, axis=1) + bx_ref[...])
    log_a = (-LRU_C) * r * _softplus(-lam_ref[...])
    a = jnp.exp(log_a)
    m = -jnp.tanh(log_a) * (a * a + 1.0)
    root = jnp.where(m > 0.0, m * lax.rsqrt(m), 0.0)
    return a, root * (i * xc)


def _mixer_kernel(gu_ref, v_ref, rx_ref, grg_ref, q_ref, k_ref, vm_ref,
                  ws_ref, bs_ref, cw_ref, cb_ref, wa_ref, ba_ref, wx_ref, bx_ref, lam_ref,
                  *rest, tm, n_side):
    side_in, rest = rest[:n_side], rest[n_side:]
    (y_ref, conv_ref, hlast_ref), rest = rest[:3], rest[3:]
    side_out, (xpad_ref, a_ref, b_ref, h_ref, hc_ref) = rest[:n_side], rest[n_side:]
    _side_cast(side_in, side_out)
    t = pl.program_id(1)

    tri = (lax.broadcasted_iota(jnp.int32, (CHUNK, CHUNK), 0)
           >= lax.broadcasted_iota(jnp.int32, (CHUNK, CHUNK), 1))
    for g in range(GMLP_GROUPS):
        wg = jnp.where(tri, ws_ref[g], 0.0).astype(BF16)
        cols = slice(g * GROUP_W, (g + 1) * GROUP_W)
        for c in range(tm // CHUNK):
            rows = slice(c * CHUNK, (c + 1) * CHUNK)
            s = _dot(wg, v_ref[rows, cols].astype(BF16)) + bs_ref[:, cols]
            y_ref[0, rows, cols] = (gu_ref[rows, cols] * s).astype(BF16)

    @pl.when(t == 0)
    def _():
        xpad_ref[0:SUBLANES, :] = jnp.zeros((SUBLANES, BR_W), F32)
        hc_ref[...] = jnp.zeros((1, BR_W), F32)

    @pl.when(t > 0)
    def _():
        xpad_ref[0:SUBLANES, :] = xpad_ref[tm:tm + SUBLANES, :]

    xpad_ref[SUBLANES:SUBLANES + tm, :] = rx_ref[...]
    xc = cb_ref[...]
    for k in range(CONV_W):
        off = SUBLANES - (CONV_W - 1) + k
        xc = xc + xpad_ref[off:off + tm, :] * cw_ref[k:k + 1, :]

    a, b = _lru_coeffs(xc, wa_ref, ba_ref, wx_ref, bx_ref, lam_ref)
    a_ref[...] = a
    b_ref[...] = b

    row = lax.broadcasted_iota(jnp.int32, (SUBLANES, BR_W), 0)
    keep = [row >= d for d in (1, 2, 4)]

    def scan_block(blk, h):
        base = pl.multiple_of(blk * SUBLANES, SUBLANES)
        ca = a_ref[pl.ds(base, SUBLANES), :]
        cb = b_ref[pl.ds(base, SUBLANES), :]
        for d, kp in zip((1, 2, 4), keep):
            a_sh = jnp.where(kp, pltpu.roll(ca, d, 0), 1.0)
            b_sh = jnp.where(kp, pltpu.roll(cb, d, 0), 0.0)
            cb = ca * b_sh + cb
            ca = ca * a_sh
        hh = cb + ca * h
        h_ref[pl.ds(base, SUBLANES), :] = hh
        return hh[SUBLANES - 1:SUBLANES, :]

    h_end = lax.fori_loop(0, tm // SUBLANES, scan_block, hc_ref[...])
    hc_ref[...] = h_end
    y_ref[1] = (grg_ref[...] * h_ref[...]).astype(BF16)

    @pl.when(t == pl.num_programs(1) - 1)
    def _():
        conv_ref[0] = rx_ref[tm - (CONV_W - 1):tm, :]
        hlast_ref[0] = h_end

    scale = XA_HEAD_DIM ** -0.5
    for hd in range(XA_HEADS):
        cols = slice(hd * XA_HEAD_DIM, (hd + 1) * XA_HEAD_DIM)
        qh = q_ref[:, cols].astype(BF16)
        kh = k_ref[0, :, cols].astype(BF16)
        vh = vm_ref[0, :, cols].astype(BF16)
        s = lax.dot_general(qh, kh, (((1,), (1,)), ((), ())), preferred_element_type=F32) * scale
        e = jnp.exp(s - jnp.max(s, axis=-1, keepdims=True))
        p = e * (1.0 / jnp.sum(e, axis=-1, keepdims=True))
        y_ref[2, :, cols] = _dot(p.astype(BF16), vh).astype(BF16)


def _mixer_prompt(z, rx, k_mem, v_mem, ws, bs_mat, conv_w, conv_b, wa, ba, wx, bx, lam, *, bsz, seq, side=(),
                  tm=256):
    nt = seq // tm
    zspec = lambda c: pl.BlockSpec((tm, BR_W), lambda b, t, c=c: (b * nt + t, c))
    full = lambda shape: pl.BlockSpec(shape, lambda b, t: (0,) * len(shape))
    side_specs, side_shapes = _side_cast_plan(side, (bsz, nt))
    return pl.pallas_call(
        functools.partial(_mixer_kernel, tm=tm, n_side=len(side)),
        grid=(bsz, nt),
        in_specs=[zspec(Z_GU), zspec(Z_V), zspec(0), zspec(Z_GRG), zspec(Z_Q),
                  pl.BlockSpec((1, N_MEM, BR_W), lambda b, t: (b, 0, 0)),
                  pl.BlockSpec((1, N_MEM, BR_W), lambda b, t: (b, 0, 0)),
                  full((GMLP_GROUPS, CHUNK, CHUNK)), full((CHUNK, BR_W)),
                  full((CONV_W, BR_W)), full((1, BR_W)),
                  full((LRU_BLOCKS, LRU_BLOCK, LRU_BLOCK)), full((1, BR_W)),
                  full((LRU_BLOCKS, LRU_BLOCK, LRU_BLOCK)), full((1, BR_W)),
                  full((1, BR_W))] + side_specs,
        out_specs=[pl.BlockSpec((3, tm, BR_W), lambda b, t: (0, b * nt + t, 0)),
                   pl.BlockSpec((1, CONV_W - 1, BR_W), lambda b, t: (b, 0, 0)),
                   pl.BlockSpec((1, 1, BR_W), lambda b, t: (b, 0, 0))] + side_specs,
        out_shape=[jax.ShapeDtypeStruct((3, bsz * seq, BR_W), BF16),
                   jax.ShapeDtypeStruct((bsz, CONV_W - 1, BR_W), F32),
                   jax.ShapeDtypeStruct((bsz, 1, BR_W), F32)] + side_shapes,
        scratch_shapes=[pltpu.VMEM((tm + SUBLANES, BR_W), F32),
                        pltpu.VMEM((tm, BR_W), F32), pltpu.VMEM((tm, BR_W), F32),
                        pltpu.VMEM((tm, BR_W), F32), pltpu.VMEM((1, BR_W), F32)],
        compiler_params=_params(("arbitrary", "arbitrary"), 48),
        name="mixer_prompt",
    )(z, z, rx, z, z, k_mem, v_mem, ws, bs_mat, conv_w, conv_b, wa, ba, wx, bx, lam, *side)


def _mixer_sample_kernel(gu_ref, v_ref, rx_ref, grg_ref, cs_ref, h0_ref,
                         wv_ref, bv_ref, cw_ref, cb_ref, wa_ref, ba_ref, wx_ref, bx_ref, lam_ref,
                         y_ref, conv_ref, h_ref):
    y_ref[0] = (gu_ref[...] * (wv_ref[...] * v_ref[...] + bv_ref[...])).astype(BF16)

    rx = rx_ref[...]
    xc = cb_ref[...] + rx * cw_ref[CONV_W - 1:CONV_W, :]
    for k in range(CONV_W - 1):
        xc = xc + cs_ref[k] * cw_ref[k:k + 1, :]
    for k in range(CONV_W - 2):
        conv_ref[k] = cs_ref[k + 1]
    conv_ref[CONV_W - 2] = rx

    a, b = _lru_coeffs(xc, wa_ref, ba_ref, wx_ref, bx_ref, lam_ref)
    h = a * h0_ref[...] + b
    h_ref[...] = h
    y_ref[1] = (grg_ref[...] * h).astype(BF16)


def _mixer_sample(z, rx, conv_state, h0, wvec, bvec, conv_w, conv_b, wa, ba, wx, bx, lam):
    n = z.shape[0]
    zspec = lambda c: pl.BlockSpec((n, BR_W), lambda i, c=c: (0, c))
    full = lambda shape: pl.BlockSpec(shape, lambda i: (0,) * len(shape))
    return pl.pallas_call(
        _mixer_sample_kernel,
        grid=(1,),
        in_specs=[zspec(Z_GU), zspec(Z_V), zspec(0), zspec(Z_GRG),
                  full((CONV_W - 1, n, BR_W)), full((n, BR_W)),
                  full((1, BR_W)), full((1, BR_W)), full((CONV_W, BR_W)), full((1, BR_W)),
                  full((LRU_BLOCKS, LRU_BLOCK, LRU_BLOCK)), full((1, BR_W)),
                  full((LRU_BLOCKS, LRU_BLOCK, LRU_BLOCK)), full((1, BR_W)),
                  full((1, BR_W))],
        out_specs=[full((2, n, BR_W)), full((CONV_W - 1, n, BR_W)), full((n, BR_W))],
        out_shape=[jax.ShapeDtypeStruct((2, n, BR_W), BF16),
                   jax.ShapeDtypeStruct((CONV_W - 1, n, BR_W), F32),
                   jax.ShapeDtypeStruct((n, BR_W), F32)],
        compiler_params=_params(("arbitrary",), 32),
        name="mixer_sample",
    )(z, z, rx, z, conv_state, h0, wvec, bvec, conv_w, conv_b, wa, ba, wx, bx, lam)


HEAD_SPLIT = XA_HEAD_DIM // 128
HEAD_ROWS = HEAD_SPLIT * XA_HEADS


def _head_rows(x):
    lead = x.shape[:-2]
    x = x.reshape(*lead, XA_HEADS, HEAD_SPLIT, 128)
    return jnp.swapaxes(x, -2, -3).reshape(*lead, HEAD_ROWS, 128)


def _xattn_sample_kernel(q_ref, k_ref, v_ref, o_ref, *, tb):
    scale = XA_HEAD_DIM ** -0.5
    n_rows = N_MEM * HEAD_ROWS
    lane = lax.broadcasted_iota(jnp.int32, (HEAD_ROWS, n_rows), 1)
    sub = lax.broadcasted_iota(jnp.int32, (HEAD_ROWS, n_rows), 0)
    own = (lane % HEAD_ROWS) == sub
    low_piece = (lax.broadcasted_iota(jnp.int32, (1, n_rows), 1) % HEAD_ROWS) < XA_HEADS

    def lane_group_reduce(x, op):
        shift = HEAD_ROWS
        while shift < 128:
            x = op(x, pltpu.roll(x, shift, 1))
            shift *= 2
        return x

    def per_kind(x, op):
        acc = x[:, 0:128]
        for c in range(1, n_rows // 128):
            acc = op(acc, x[:, c * 128:(c + 1) * 128])
        acc = lane_group_reduce(acc, op)
        return jnp.concatenate([acc] * (n_rows // 128), axis=1)

    parts = []
    for b in range(tb):
        qb = q_ref[b].astype(BF16)
        kb = k_ref[b].reshape(n_rows, 128).astype(BF16)
        c = lax.dot_general(qb, kb, (((1,), (1,)), ((), ())), preferred_element_type=F32)
        parts.append(jnp.sum(jnp.where(own, c, 0.0), axis=0, keepdims=True))
    part = jnp.concatenate(parts, axis=0)
    other = jnp.where(low_piece, pltpu.roll(part, n_rows - XA_HEADS, 1), pltpu.roll(part, XA_HEADS, 1))
    s = (part + other) * scale
    e = jnp.exp(s - per_kind(s, jnp.maximum))
    p = e * (1.0 / per_kind(e, jnp.add))
    for b in range(tb):
        vb = v_ref[b].reshape(n_rows, 128).astype(BF16)
        pm = jnp.where(own, jnp.broadcast_to(p[b:b + 1, :], (HEAD_ROWS, n_rows)), 0.0).astype(BF16)
        o_ref[b] = _dot(pm, vb).astype(BF16)


def _xattn_sample(q, k_cache, v_cache, *, tb=8):
    n = q.shape[0]
    kv_spec = pl.BlockSpec((tb, N_MEM, HEAD_ROWS, 128), lambda i: (i, 0, 0, 0))
    q_spec = pl.BlockSpec((tb, HEAD_ROWS, 128), lambda i: (i, 0, 0))
    return pl.pallas_call(
        functools.partial(_xattn_sample_kernel, tb=tb),
        grid=(n // tb,),
        in_specs=[q_spec, kv_spec, kv_spec],
        out_specs=q_spec,
        out_shape=jax.ShapeDtypeStruct((n, HEAD_ROWS, 128), BF16),
        compiler_params=_params(("arbitrary",), 48),
        name="xattn_sample",
    )(q, k_cache, v_cache)


N_BRANCH = 3


def _merge_kernel(y_ref, wb_ref, g0_ref, g1_ref, g2_ref, o_ref):
    acc = None
    for k, gate_ref in enumerate((g0_ref, g1_ref, g2_ref)):
        term = gate_ref[...].astype(F32) * _dot(y_ref[k], wb_ref[k])
        acc = term if acc is None else acc + term
    o_ref[...] = acc.astype(BF16)


def _merge(y, w_branch, z, *, tm):
    m = y.shape[1]
    gate_spec = lambda k: pl.BlockSpec((tm, BR_W), lambda i, n, k=k: (i, Z_GATE0 + 2 * k + n))
    return pl.pallas_call(
        _merge_kernel,
        grid=(m // tm, D_MODEL // BR_W),
        in_specs=[
            pl.BlockSpec((N_BRANCH, tm, BR_W), lambda i, n: (0, i, 0)),
            pl.BlockSpec((N_BRANCH, BR_W, BR_W), lambda i, n: (0, 0, n)),
            gate_spec(0), gate_spec(1), gate_spec(2),
        ],
        out_specs=pl.BlockSpec((tm, BR_W), lambda i, n: (i, n)),
        out_shape=jax.ShapeDtypeStruct((m, D_MODEL), BF16),
        compiler_params=_params(("arbitrary", "arbitrary"), 56),
        name="merge",
    )(y, w_branch, z, z, z)


def _outproj_kernel(m_ref, wo_ref, x_ref, g_ref, b_ref, o_ref, *, row_splits):
    rows_per = m_ref.shape[0] // row_splits
    for r in range(row_splits):
        rows = slice(r * rows_per, (r + 1) * rows_per)
        y = ALPHA * x_ref[rows, :] + _dot(m_ref[rows, :], wo_ref[...])
        o_ref[rows, :] = _layer_norm(y, g_ref[...], b_ref[...])


def _outproj(merged, w_out, x, ln_g, ln_b, *, tm):
    m = x.shape[0]
    return pl.pallas_call(
        functools.partial(_outproj_kernel, row_splits=max(1, tm // OUTPROJ_ROW_GROUP)),
        grid=(m // tm,),
        in_specs=[
            pl.BlockSpec((tm, D_MODEL), lambda i: (i, 0)),
            pl.BlockSpec((D_MODEL, D_MODEL), lambda i: (0, 0)),
            pl.BlockSpec((tm, D_MODEL), lambda i: (i, 0)),
            pl.BlockSpec((1, D_MODEL), lambda i: (0, 0)),
            pl.BlockSpec((1, D_MODEL), lambda i: (0, 0)),
        ],
        out_specs=pl.BlockSpec((tm, D_MODEL), lambda i: (i, 0)),
        out_shape=jax.ShapeDtypeStruct((m, D_MODEL), F32),
        compiler_params=_params(("arbitrary",), 48),
        name="out_proj",
    )(merged, w_out, x, ln_g, ln_b)


def kernel(x_prompt, x_sample, mem_prompt, cache_mem_k, cache_mem_v, state_conv, state_lru_h, ffn1_w_gu, ffn1_w_down, ln1_g, ln1_b, w_in, gate_b, gmlp_ln_g, gmlp_ln_b, gmlp_w_s, gmlp_b_s, conv_w, conv_b, lru_w_a, lru_b_a, lru_w_x, lru_b_x, lru_lambda, mem_ln_g, mem_ln_b, w_mem_kv, w_branch, w_out, ln2_g, ln2_b, ffn2_w_gu, ffn2_w_down, ln3_g, ln3_b):
    bsz, seq, _ = x_prompt.shape
    n_s = x_sample.shape[0]
    l = 0

    wa, wx = lru_w_a[l].astype(BF16), lru_w_x[l].astype(BF16)

    row = lambda p: p[l].reshape(1, -1)
    gb = gate_b[l].reshape(1, -1)
    bs_mat = jnp.repeat(gmlp_b_s[l].T, GROUP_W, axis=1)
    wvec = jnp.repeat(gmlp_w_s[l][:, 0, 0], GROUP_W).reshape(1, -1)
    bvec = jnp.repeat(gmlp_b_s[l][:, 0], GROUP_W).reshape(1, -1)
    lru = (wa, row(lru_b_a), wx, row(lru_b_x), row(lru_lambda))

    xp = x_prompt.reshape(bsz * seq, D_MODEL)
    xs = x_sample.reshape(n_s, D_MODEL)

    x1s, w1g, w1u, w1d = _ffn(xs, ffn1_w_gu[l], ffn1_w_gu[l], ffn1_w_down[l], row(ln1_g), row(ln1_b),
                              tm=n_s, emit_bf16=True)
    k_p, v_p = _memkv(mem_prompt.reshape(bsz * N_MEM, D_MODEL), w_mem_kv[l], row(mem_ln_g), row(mem_ln_b))
    x1p, win, w2d = _ffn(xp, w1g, w1u, w1d, row(ln1_g), row(ln1_b), tm=1024, side=(w_in[l], ffn2_w_down[l]))

    zs, rxs = _inproj(x1s, win, gb, row(gmlp_ln_g), row(gmlp_ln_b), tm=n_s, z_dtype=F32)
    y01, conv_s, h_s = _mixer_sample(
        zs, rxs, jnp.swapaxes(state_conv[l], 0, 1), state_lru_h[l], wvec, bvec, conv_w[l], row(conv_b), *lru)
    q_s = zs[:, Z_Q * BR_W:(Z_Q + 1) * BR_W].reshape(n_s, XA_HEADS, XA_HEAD_DIM)
    yxa = _xattn_sample(_head_rows(q_s), _head_rows(cache_mem_k[l]), _head_rows(cache_mem_v[l]))
    yxa = jnp.swapaxes(yxa.reshape(n_s, HEAD_SPLIT, XA_HEADS, 128), 1, 2)
    ys = jnp.concatenate([y01, yxa.reshape(1, n_s, BR_W)], axis=0)

    zp, rxp, w2gu = _inproj(x1p, win, gb, row(gmlp_ln_g), row(gmlp_ln_b), tm=1024, z_dtype=BF16,
                            side=(ffn2_w_gu[l],))
    yp, conv_p, hlast_p, wbr, wout = _mixer_prompt(
        zp, rxp, k_p.reshape(bsz, N_MEM, BR_W), v_p.reshape(bsz, N_MEM, BR_W), gmlp_w_s[l], bs_mat,
        conv_w[l], row(conv_b), *lru, bsz=bsz, seq=seq,
        side=(w_branch[l].reshape(N_BRANCH * BR_W, D_MODEL), w_out[l]))
    wbr = wbr.reshape(N_BRANCH, BR_W, D_MODEL)

    ms = _merge(ys, wbr, zs, tm=n_s)
    x2s = _outproj(ms, wout, x1s, row(ln2_g), row(ln2_b), tm=n_s)
    x3s = _ffn(x2s, w2gu, w2gu, w2d, row(ln3_g), row(ln3_b), tm=n_s)
    mp = _merge(yp, wbr, zp, tm=1024)
    x2p = _outproj(mp, wout, x1p, row(ln2_g), row(ln2_b), tm=512)
    x3p = _ffn(x2p, w2gu, w2gu, w2d, row(ln3_g), row(ln3_b), tm=1024)

    kv_shape = (1, bsz, N_MEM, XA_HEADS, XA_HEAD_DIM)
    return (x3p.reshape(bsz, seq, D_MODEL),
            x3s.reshape(n_s, 1, D_MODEL),
            k_p.reshape(kv_shape),
            v_p.reshape(kv_shape),
            conv_p[None],
            hlast_p.reshape(1, bsz, BR_W),
            jnp.swapaxes(conv_s, 0, 1)[None],
            h_s[None],
            zs[:, Z_V * BR_W:(Z_V + 1) * BR_W].reshape(1, n_s, 1, BR_W))
```

```python
import functools

import jax
import jax.numpy as jnp
from jax import lax
from jax.experimental import pallas as pl
from jax.experimental.pallas import tpu as pltpu

F32 = jnp.float32
BF16 = jnp.bfloat16

D_MODEL = 2048
BR_W = 1024
D_FF = 5632
N_MEM = 256
XA_HEADS = 4
XA_HEAD_DIM = 256
GMLP_GROUPS = 4
GROUP_W = BR_W // GMLP_GROUPS
CHUNK = 128
LRU_BLOCKS = 8
LRU_BLOCK = 128
CONV_W = 4
LRU_C = 8.0
LN_EPS = 1e-5
ALPHA = 2.0 ** 0.25
IN_BLOCKS = 11
GATE_BLOCK0 = 5
Z_SLOTS = 12
Z_GU, Z_V, Z_GRG, Z_Q, Z_GATE0 = 0, 1, 2, 4, 5
SUBLANES = 8
FFN_ROW_GROUP = 512
OUTPROJ_ROW_GROUP = 128
MiB = 1024 * 1024


def _params(semantics, vmem_mib):
    return pltpu.CompilerParams(dimension_semantics=semantics, vmem_limit_bytes=vmem_mib * MiB)


def _layer_norm(x, g, b):
    mu = jnp.mean(x, axis=-1, keepdims=True)
    xc = x - mu
    var = jnp.mean(xc * xc, axis=-1, keepdims=True)
    return xc * lax.rsqrt(var + LN_EPS) * g + b


def _dot(a, b):
    return jnp.dot(a, b, preferred_element_type=F32)


def _sigmoid(x):
    return 0.5 * jnp.tanh(0.5 * x) + 0.5


def _mxu_operand(w_ref, w16_ref=None):
    w = w_ref[...].astype(BF16)
    if w16_ref is not None:
        w16_ref[...] = w
    return w


def _side_cast_plan(side, grid):
    n_steps = 1
    for g in grid:
        n_steps *= g

    def linear(*idx):
        lin = idx[0]
        for g, i in zip(grid[1:], idx[1:]):
            lin = lin * g + i
        return lin

    specs, shapes = [], []
    for w in side:
        r, c = w.shape
        if c % (n_steps * 128) == 0:
            spec = pl.BlockSpec((r, c // n_steps), lambda *idx: (0, linear(*idx)))
        else:
            assert r % (n_steps * 16) == 0, (w.shape, n_steps)
            spec = pl.BlockSpec((r // n_steps, c), lambda *idx: (linear(*idx), 0))
        specs.append(spec)
        shapes.append(jax.ShapeDtypeStruct(w.shape, BF16))
    return specs, shapes


def _side_cast(in_refs, out_refs):
    for src, dst in zip(in_refs, out_refs):
        dst[...] = src[...].astype(BF16)


def _ffn_kernel(x_ref, wg_ref, wu_ref, wd_ref, g_ref, b_ref, *rest, emit_bf16, row_splits, n_side):
    side_in, rest = rest[:n_side], rest[n_side:]
    o_ref, rest = rest[0], rest[1:]
    if emit_bf16:
        (wg16_ref, wu16_ref, wd16_ref), rest = rest[:3], rest[3:]
    else:
        wg16_ref = wu16_ref = wd16_ref = None
    side_out, (xb_ref,) = rest[:n_side], rest[n_side:]
    _side_cast(side_in, side_out)
    j = pl.program_id(1)
    last = pl.num_programs(1) - 1
    rows_per = xb_ref.shape[0] // row_splits

    def step(first, final):
        wg = _mxu_operand(wg_ref, wg16_ref)
        wu = _mxu_operand(wu_ref, wu16_ref)
        wd = _mxu_operand(wd_ref, wd16_ref)
        for r in range(row_splits):
            rows = slice(r * rows_per, (r + 1) * rows_per)
            if first:
                xb = x_ref[rows, :].astype(BF16)
                xb_ref[rows, :] = xb
            else:
                xb = xb_ref[rows, :]
            g = _dot(xb, wg)
            u = _dot(xb, wu)
            h = (g * _sigmoid(g) * u).astype(BF16)
            acc = _dot(h, wd)
            if not first:
                acc = o_ref[rows, :] + acc
            if final:
                acc = _layer_norm(ALPHA * x_ref[rows, :] + 0.5 * acc, g_ref[...], b_ref[...])
            o_ref[rows, :] = acc

    pl.when(j == 0)(functools.partial(step, True, False))
    pl.when((j > 0) & (j < last))(functools.partial(step, False, False))
    pl.when(j == last)(functools.partial(step, False, True))


def _ffn(x, w_g, w_u, w_down, ln_g, ln_b, *, tm, emit_bf16=False, side=(), tf=512):
    m = x.shape[0]
    nf = D_FF // tf
    grid = (m // tm, nf)
    u_off = nf if w_u.shape[1] == 2 * D_FF else 0
    assert not emit_bf16 or m == tm
    side_specs, side_shapes = _side_cast_plan(side, grid)
    out_specs = [pl.BlockSpec((tm, D_MODEL), lambda i, j: (i, 0))]
    out_shape = [jax.ShapeDtypeStruct((m, D_MODEL), F32)]
    if emit_bf16:
        out_specs += [pl.BlockSpec((D_MODEL, tf), lambda i, j: (0, j)),
                      pl.BlockSpec((D_MODEL, tf), lambda i, j: (0, j)),
                      pl.BlockSpec((tf, D_MODEL), lambda i, j: (j, 0))]
        out_shape += [jax.ShapeDtypeStruct((D_MODEL, D_FF), BF16),
                      jax.ShapeDtypeStruct((D_MODEL, D_FF), BF16),
                      jax.ShapeDtypeStruct((D_FF, D_MODEL), BF16)]
    outs = pl.pallas_call(
        functools.partial(_ffn_kernel, emit_bf16=emit_bf16, row_splits=max(1, tm // FFN_ROW_GROUP),
                          n_side=len(side)),
        grid=grid,
        in_specs=[
            pl.BlockSpec((tm, D_MODEL), lambda i, j: (i, 0)),
            pl.BlockSpec((D_MODEL, tf), lambda i, j: (0, j)),
            pl.BlockSpec((D_MODEL, tf), lambda i, j: (0, j + u_off)),
            pl.BlockSpec((tf, D_MODEL), lambda i, j: (j, 0)),
            pl.BlockSpec((1, D_MODEL), lambda i, j: (0, 0)),
            pl.BlockSpec((1, D_MODEL), lambda i, j: (0, 0)),
        ] + side_specs,
        out_specs=out_specs + side_specs,
        out_shape=out_shape + side_shapes,
        scratch_shapes=[pltpu.VMEM((tm, D_MODEL), BF16)],
        compiler_params=_params(("arbitrary", "arbitrary"), 62),
        name="ffn_ln",
    )(x, w_g, w_u, w_down, ln_g, ln_b, *side)
    return outs if (emit_bf16 or side) else outs[0]


def _memkv_kernel(x_ref, wk_ref, wv_ref, g_ref, b_ref, k_ref, v_ref, xb_ref):
    @pl.when(pl.program_id(1) == 0)
    def _():
        xb_ref[...] = _layer_norm(x_ref[...], g_ref[...], b_ref[...]).astype(BF16)

    xb = xb_ref[...]
    k_ref[...] = _dot(xb, _mxu_operand(wk_ref))
    v_ref[...] = _dot(xb, _mxu_operand(wv_ref))


def _memkv(mem, w_kv, ln_g, ln_b, *, tm=1024, tn=512):
    m = mem.shape[0]
    nn = BR_W // tn
    return pl.pallas_call(
        _memkv_kernel,
        grid=(m // tm, nn),
        in_specs=[
            pl.BlockSpec((tm, D_MODEL), lambda i, j: (i, 0)),
            pl.BlockSpec((D_MODEL, tn), lambda i, j: (0, j)),
            pl.BlockSpec((D_MODEL, tn), lambda i, j: (0, j + nn)),
            pl.BlockSpec((1, D_MODEL), lambda i, j: (0, 0)),
            pl.BlockSpec((1, D_MODEL), lambda i, j: (0, 0)),
        ],
        out_specs=[pl.BlockSpec((tm, tn), lambda i, j: (i, j)),
                   pl.BlockSpec((tm, tn), lambda i, j: (i, j))],
        out_shape=[jax.ShapeDtypeStruct((m, BR_W), F32), jax.ShapeDtypeStruct((m, BR_W), F32)],
        scratch_shapes=[pltpu.VMEM((tm, D_MODEL), BF16)],
        compiler_params=_params(("arbitrary", "arbitrary"), 48),
        name="mem_kv",
    )(mem, w_kv, w_kv, ln_g, ln_b)


def _inproj_kernel(x_ref, wa_ref, wb_ref, gba_ref, gbb_ref, lg_ref, lb_ref, *rest, n_side):
    side_in, rest = rest[:n_side], rest[n_side:]
    (o_ref, rx_ref), rest = rest[:2], rest[2:]
    side_out, (xb_ref,) = rest[:n_side], rest[n_side:]
    _side_cast(side_in, side_out)
    s = pl.program_id(1)
    lo, hi = slice(0, BR_W), slice(BR_W, 2 * BR_W)

    @pl.when(s == 0)
    def _():
        xb_ref[...] = x_ref[...].astype(BF16)

    def put(cols, val):
        o_ref[:, cols] = val.astype(o_ref.dtype)

    za = lambda: _dot(xb_ref[...], wa_ref[...])
    zb = lambda: _dot(xb_ref[...], wb_ref[...])
    gate_a = lambda: _sigmoid(za() + gba_ref[...])
    gate_b = lambda: _sigmoid(zb() + gbb_ref[...])

    @pl.when(s == 0)
    def _():
        put(hi, _layer_norm(jax.nn.gelu(zb()), lg_ref[...], lb_ref[...]))
        put(lo, jax.nn.gelu(za()))

    @pl.when(s == 1)
    def _():
        put(lo, jax.nn.gelu(zb()))
        put(hi, jnp.zeros((o_ref.shape[0], BR_W), F32))
        rx_ref[...] = za()

    @pl.when(s == 2)
    def _():
        put(hi, gate_b())
        put(lo, za())

    @pl.when((s == 3) | (s == 4))
    def _():
        put(lo, gate_a())
        put(hi, gate_b())

    @pl.when(s == 5)
    def _():
        put(lo, gate_a())
        put(hi, jnp.zeros((o_ref.shape[0], BR_W), F32))


def _inproj(x, w_in, gate_b, gln_g, gln_b, *, tm, z_dtype, side=()):
    m = x.shape[0]
    n_steps = (IN_BLOCKS + 1) // 2
    grid = (m // tm, n_steps)
    n_gate = IN_BLOCKS - GATE_BLOCK0
    side_specs, side_shapes = _side_cast_plan(side, grid)
    out_specs = [pl.BlockSpec((tm, 2 * BR_W), lambda i, s: (i, s)),
                 pl.BlockSpec((tm, BR_W), lambda i, s: (i, 0))]
    out_shape = [jax.ShapeDtypeStruct((m, Z_SLOTS * BR_W), z_dtype),
                 jax.ShapeDtypeStruct((m, BR_W), F32)]
    return pl.pallas_call(
        functools.partial(_inproj_kernel, n_side=len(side)),
        grid=grid,
        in_specs=[
            pl.BlockSpec((tm, D_MODEL), lambda i, s: (i, 0)),
            pl.BlockSpec((D_MODEL, BR_W), lambda i, s: (0, 2 * s)),
            pl.BlockSpec((D_MODEL, BR_W), lambda i, s: (0, jnp.minimum(2 * s + 1, IN_BLOCKS - 2))),
            pl.BlockSpec((1, BR_W), lambda i, s: (0, jnp.clip(2 * s - GATE_BLOCK0, 0, n_gate - 1))),
            pl.BlockSpec((1, BR_W), lambda i, s: (0, jnp.clip(2 * s + 1 - GATE_BLOCK0, 0, n_gate - 1))),
            pl.BlockSpec((1, BR_W), lambda i, s: (0, 0)),
            pl.BlockSpec((1, BR_W), lambda i, s: (0, 0)),
        ] + side_specs,
        out_specs=out_specs + side_specs,
        out_shape=out_shape + side_shapes,
        scratch_shapes=[pltpu.VMEM((tm, D_MODEL), BF16)],
        compiler_params=_params(("arbitrary", "arbitrary"), 62),
        name="in_proj",
    )(x, w_in, w_in, gate_b, gate_b, gln_g, gln_b, *side)


def _softplus(x):
    return jnp.maximum(x, 0.0) + jnp.log1p(jnp.exp(-jnp.abs(x)))


def _lru_coeffs(xc, wa_ref, ba_ref, wx_ref, bx_ref, lam_ref):
    xcb = xc.astype(BF16)
    r_parts, i_parts = [], []
    for k in range(LRU_BLOCKS):
        xk = xcb[:, k * LRU_BLOCK:(k + 1) * LRU_BLOCK]
        r_parts.append(_dot(xk, wa_ref[k]))
        i_parts.append(_dot(xk, wx_ref[k]))
    r = _sigmoid(jnp.concatenate(r_parts, axis=1) + ba_ref[...])
    i = _sigmoid(jnp.concatenate(i_parts, axis=1) + bx_ref[...])
    log_a = (-LRU_C) * r * _softplus(-lam_ref[...])
    a = jnp.exp(log_a)
    m = -jnp.tanh(log_a) * (a * a + 1.0)
    root = jnp.where(m > 0.0, m * lax.rsqrt(m), 0.0)
    return a, root * (i * xc)


def _mixer_kernel(gu_ref, v_ref, rx_ref, grg_ref, q_ref, k_ref, vm_ref,
                  ws_ref, bs_ref, cw_ref, cb_ref, wa_ref, ba_ref, wx_ref, bx_ref, lam_ref,
                  *rest, tm, n_side):
    side_in, rest = rest[:n_side], rest[n_side:]
    (y_ref, conv_ref, hlast_ref), rest = rest[:3], rest[3:]
    side_out, (xpad_ref, a_ref, b_ref, h_ref, hc_ref) = rest[:n_side], rest[n_side:]
    _side_cast(side_in, side_out)
    t = pl.program_id(1)

    tri = (lax.broadcasted_iota(jnp.int32, (CHUNK, CHUNK), 0)
           >= lax.broadcasted_iota(jnp.int32, (CHUNK, CHUNK), 1))
    for g in range(GMLP_GROUPS):
        wg = jnp.where(tri, ws_ref[g], 0.0).astype(BF16)
        cols = slice(g * GROUP_W, (g + 1) * GROUP_W)
        for c in range(tm // CHUNK):
            rows = slice(c * CHUNK, (c + 1) * CHUNK)
            s = _dot(wg, v_ref[rows, cols].astype(BF16)) + bs_ref[:, cols]
            y_ref[0, rows, cols] = (gu_ref[rows, cols] * s).astype(BF16)

    @pl.when(t == 0)
    def _():
        xpad_ref[0:SUBLANES, :] = jnp.zeros((SUBLANES, BR_W), F32)
        hc_ref[...] = jnp.zeros((1, BR_W), F32)

    @pl.when(t > 0)
    def _():
        xpad_ref[0:SUBLANES, :] = xpad_ref[tm:tm + SUBLANES, :]

    xpad_ref[SUBLANES:SUBLANES + tm, :] = rx_ref[...]
    xc = cb_ref[...]
    for k in range(CONV_W):
        off = SUBLANES - (CONV_W - 1) + k
        xc = xc + xpad_ref[off:off + tm, :] * cw_ref[k:k + 1, :]

    a, b = _lru_coeffs(xc, wa_ref, ba_ref, wx_ref, bx_ref, lam_ref)
    a_ref[...] = a
    b_ref[...] = b

    row = lax.broadcasted_iota(jnp.int32, (SUBLANES, BR_W), 0)
    keep = [row >= d for d in (1, 2, 4)]

    def scan_block(blk, h):
        base = pl.multiple_of(blk * SUBLANES, SUBLANES)
        ca = a_ref[pl.ds(base, SUBLANES), :]
        cb = b_ref[pl.ds(base, SUBLANES), :]
        for d, kp in zip((1, 2, 4), keep):
            a_sh = jnp.where(kp, pltpu.roll(ca, d, 0), 1.0)
            b_sh = jnp.where(kp, pltpu.roll(cb, d, 0), 0.0)
            cb = ca * b_sh + cb
            ca = ca * a_sh
        hh = cb + ca * h
        h_ref[pl.ds(base, SUBLANES), :] = hh
        return hh[SUBLANES - 1:SUBLANES, :]

    h_end = lax.fori_loop(0, tm // SUBLANES, scan_block, hc_ref[...])
    hc_ref[...] = h_end
    y_ref[1] = (grg_ref[...] * h_ref[...]).astype(BF16)

    @pl.when(t == pl.num_programs(1) - 1)
    def _():
        conv_ref[0] = rx_ref[tm - (CONV_W - 1):tm, :]
        hlast_ref[0] = h_end

    scale = XA_HEAD_DIM ** -0.5
    for hd in range(XA_HEADS):
        cols = slice(hd * XA_HEAD_DIM, (hd + 1) * XA_HEAD_DIM)
        qh = q_ref[:, cols].astype(BF16)
        kh = k_ref[0, :, cols].astype(BF16)
        vh = vm_ref[0, :, cols].astype(BF16)
        s = lax.dot_general(qh, kh, (((1,), (1,)), ((), ())), preferred_element_type=F32) * scale
        e = jnp.exp(s - jnp.max(s, axis=-1, keepdims=True))
        p = e * (1.0 / jnp.sum(e, axis=-1, keepdims=True))
        y_ref[2, :, cols] = _dot(p.astype(BF16), vh).astype(BF16)


def _mixer_prompt(z, rx, k_mem, v_mem, ws, bs_mat, conv_w, conv_b, wa, ba, wx, bx, lam, *, bsz, seq, side=(),
                  tm=256):
    nt = seq // tm
    zspec = lambda c: pl.BlockSpec((tm, BR_W), lambda b, t, c=c: (b * nt + t, c))
    full = lambda shape: pl.BlockSpec(shape, lambda b, t: (0,) * len(shape))
    side_specs, side_shapes = _side_cast_plan(side, (bsz, nt))
    return pl.pallas_call(
        functools.partial(_mixer_kernel, tm=tm, n_side=len(side)),
        grid=(bsz, nt),
        in_specs=[zspec(Z_GU), zspec(Z_V), zspec(0), zspec(Z_GRG), zspec(Z_Q),
                  pl.BlockSpec((1, N_MEM, BR_W), lambda b, t: (b, 0, 0)),
                  pl.BlockSpec((1, N_MEM, BR_W), lambda b, t: (b, 0, 0)),
                  full((GMLP_GROUPS, CHUNK, CHUNK)), full((CHUNK, BR_W)),
                  full((CONV_W, BR_W)), full((1, BR_W)),
                  full((LRU_BLOCKS, LRU_BLOCK, LRU_BLOCK)), full((1, BR_W)),
                  full((LRU_BLOCKS, LRU_BLOCK, LRU_BLOCK)), full((1, BR_W)),
                  full((1, BR_W))] + side_specs,
        out_specs=[pl.BlockSpec((3, tm, BR_W), lambda b, t: (0, b * nt + t, 0)),
                   pl.BlockSpec((1, CONV_W - 1, BR_W), lambda b, t: (b, 0, 0)),
                   pl.BlockSpec((1, 1, BR_W), lambda b, t: (b, 0, 0))] + side_specs,
        out_shape=[jax.ShapeDtypeStruct((3, bsz * seq, BR_W), BF16),
                   jax.ShapeDtypeStruct((bsz, CONV_W - 1, BR_W), F32),
                   jax.ShapeDtypeStruct((bsz, 1, BR_W), F32)] + side_shapes,
        scratch_shapes=[pltpu.VMEM((tm + SUBLANES, BR_W), F32),
                        pltpu.VMEM((tm, BR_W), F32), pltpu.VMEM((tm, BR_W), F32),
                        pltpu.VMEM((tm, BR_W), F32), pltpu.VMEM((1, BR_W), F32)],
        compiler_params=_params(("arbitrary", "arbitrary"), 48),
        name="mixer_prompt",
    )(z, z, rx, z, z, k_mem, v_mem, ws, bs_mat, conv_w, conv_b, wa, ba, wx, bx, lam, *side)


def _mixer_sample_kernel(gu_ref, v_ref, rx_ref, grg_ref, cs_ref, h0_ref,
                         wv_ref, bv_ref, cw_ref, cb_ref, wa_ref, ba_ref, wx_ref, bx_ref, lam_ref,
                         y_ref, conv_ref, h_ref):
    y_ref[0] = (gu_ref[...] * (wv_ref[...] * v_ref[...] + bv_ref[...])).astype(BF16)

    rx = rx_ref[...]
    xc = cb_ref[...] + rx * cw_ref[CONV_W - 1:CONV_W, :]
    for k in range(CONV_W - 1):
        xc = xc + cs_ref[k] * cw_ref[k:k + 1, :]
    for k in range(CONV_W - 2):
        conv_ref[k] = cs_ref[k + 1]
    conv_ref[CONV_W - 2] = rx

    a, b = _lru_coeffs(xc, wa_ref, ba_ref, wx_ref, bx_ref, lam_ref)
    h = a * h0_ref[...] + b
    h_ref[...] = h
    y_ref[1] = (grg_ref[...] * h).astype(BF16)


def _mixer_sample(z, rx, conv_state, h0, wvec, bvec, conv_w, conv_b, wa, ba, wx, bx, lam):
    n = z.shape[0]
    zspec = lambda c: pl.BlockSpec((n, BR_W), lambda i, c=c: (0, c))
    full = lambda shape: pl.BlockSpec(shape, lambda i: (0,) * len(shape))
    return pl.pallas_call(
        _mixer_sample_kernel,
        grid=(1,),
        in_specs=[zspec(Z_GU), zspec(Z_V), zspec(0), zspec(Z_GRG),
                  full((CONV_W - 1, n, BR_W)), full((n, BR_W)),
                  full((1, BR_W)), full((1, BR_W)), full((CONV_W, BR_W)), full((1, BR_W)),
                  full((LRU_BLOCKS, LRU_BLOCK, LRU_BLOCK)), full((1, BR_W)),
                  full((LRU_BLOCKS, LRU_BLOCK, LRU_BLOCK)), full((1, BR_W)),
                  full((1, BR_W))],
        out_specs=[full((2, n, BR_W)), full((CONV_W - 1, n, BR_W)), full((n, BR_W))],
        out_shape=[jax.ShapeDtypeStruct((2, n, BR_W), BF16),
                   jax.ShapeDtypeStruct((CONV_W - 1, n, BR_W), F32),
                   jax.ShapeDtypeStruct((n, BR_W), F32)],
        compiler_params=_params(("arbitrary",), 32),
        name="mixer_sample",
    )(z, z, rx, z, conv_state, h0, wvec, bvec, conv_w, conv_b, wa, ba, wx, bx, lam)


HEAD_SPLIT = XA_HEAD_DIM // 128
HEAD_ROWS = HEAD_SPLIT * XA_HEADS


def _head_rows(x):
    lead = x.shape[:-2]
    x = x.reshape(*lead, XA_HEADS, HEAD_SPLIT, 128)
    return jnp.swapaxes(x, -2, -3).reshape(*lead, HEAD_ROWS, 128)


def _xattn_sample_kernel(q_ref, k_ref, v_ref, o_ref, *, tb):
    scale = XA_HEAD_DIM ** -0.5
    n_rows = N_MEM * HEAD_ROWS
    lane = lax.broadcasted_iota(jnp.int32, (HEAD_ROWS, n_rows), 1)
    sub = lax.broadcasted_iota(jnp.int32, (HEAD_ROWS, n_rows), 0)
    own = (lane % HEAD_ROWS) == sub
    low_piece = (lax.broadcasted_iota(jnp.int32, (1, n_rows), 1) % HEAD_ROWS) < XA_HEADS

    def lane_group_reduce(x, op):
        shift = HEAD_ROWS
        while shift < 128:
            x = op(x, pltpu.roll(x, shift, 1))
            shift *= 2
        return x

    def per_kind(x, op):
        acc = x[:, 0:128]
        for c in range(1, n_rows // 128):
            acc = op(acc, x[:, c * 128:(c + 1) * 128])
        acc = lane_group_reduce(acc, op)
        return jnp.concatenate([acc] * (n_rows // 128), axis=1)

    parts = []
    for b in range(tb):
        qb = q_ref[b].astype(BF16)
        kb = k_ref[b].reshape(n_rows, 128).astype(BF16)
        c = lax.dot_general(qb, kb, (((1,), (1,)), ((), ())), preferred_element_type=F32)
        parts.append(jnp.sum(jnp.where(own, c, 0.0), axis=0, keepdims=True))
    part = jnp.concatenate(parts, axis=0)
    other = jnp.where(low_piece, pltpu.roll(part, n_rows - XA_HEADS, 1), pltpu.roll(part, XA_HEADS, 1))
    s = (part + other) * scale
    e = jnp.exp(s - per_kind(s, jnp.maximum))
    p = e * (1.0 / per_kind(e, jnp.add))
    for b in range(tb):
        vb = v_ref[b].reshape(n_rows, 128).astype(BF16)
        pm = jnp.where(own, jnp.broadcast_to(p[b:b + 1, :], (HEAD_ROWS, n_rows)), 0.0).astype(BF16)
        o_ref[b] = _dot(pm, vb).astype(BF16)


def _xattn_sample(q, k_cache, v_cache, *, tb=8):
    n = q.shape[0]
    kv_spec = pl.BlockSpec((tb, N_MEM, HEAD_ROWS, 128), lambda i: (i, 0, 0, 0))
    q_spec = pl.BlockSpec((tb, HEAD_ROWS, 128), lambda i: (i, 0, 0))
    return pl.pallas_call(
        functools.partial(_xattn_sample_kernel, tb=tb),
        grid=(n // tb,),
        in_specs=[q_spec, kv_spec, kv_spec],
        out_specs=q_spec,
        out_shape=jax.ShapeDtypeStruct((n, HEAD_ROWS, 128), BF16),
        compiler_params=_params(("arbitrary",), 48),
        name="xattn_sample",
    )(q, k_cache, v_cache)


N_BRANCH = 3


def _merge_kernel(y_ref, wb_ref, g0_ref, g1_ref, g2_ref, o_ref):
    acc = None
    for k, gate_ref in enumerate((g0_ref, g1_ref, g2_ref)):
        term = gate_ref[...].astype(F32) * _dot(y_ref[k], wb_ref[k])
        acc = term if acc is None else acc + term
    o_ref[...] = acc.astype(BF16)


def _merge(y, w_branch, z, *, tm):
    m = y.shape[1]
    gate_spec = lambda k: pl.BlockSpec((tm, BR_W), lambda i, n, k=k: (i, Z_GATE0 + 2 * k + n))
    return pl.pallas_call(
        _merge_kernel,
        grid=(m // tm, D_MODEL // BR_W),
        in_specs=[
            pl.BlockSpec((N_BRANCH, tm, BR_W), lambda i, n: (0, i, 0)),
            pl.BlockSpec((N_BRANCH, BR_W, BR_W), lambda i, n: (0, 0, n)),
            gate_spec(0), gate_spec(1), gate_spec(2),
        ],
        out_specs=pl.BlockSpec((tm, BR_W), lambda i, n: (i, n)),
        out_shape=jax.ShapeDtypeStruct((m, D_MODEL), BF16),
        compiler_params=_params(("arbitrary", "arbitrary"), 56),
        name="merge",
    )(y, w_branch, z, z, z)


def _outproj_kernel(m_ref, wo_ref, x_ref, g_ref, b_ref, o_ref, *, row_splits):
    rows_per = m_ref.shape[0] // row_splits
    for r in range(row_splits):
        rows = slice(r * rows_per, (r + 1) * rows_per)
        y = ALPHA * x_ref[rows, :] + _dot(m_ref[rows, :], wo_ref[...])
        o_ref[rows, :] = _layer_norm(y, g_ref[...], b_ref[...])


def _outproj(merged, w_out, x, ln_g, ln_b, *, tm):
    m = x.shape[0]
    return pl.pallas_call(
        functools.partial(_outproj_kernel, row_splits=max(1, tm // OUTPROJ_ROW_GROUP)),
        grid=(m // tm,),
        in_specs=[
            pl.BlockSpec((tm, D_MODEL), lambda i: (i, 0)),
            pl.BlockSpec((D_MODEL, D_MODEL), lambda i: (0, 0)),
            pl.BlockSpec((tm, D_MODEL), lambda i: (i, 0)),
            pl.BlockSpec((1, D_MODEL), lambda i: (0, 0)),
            pl.BlockSpec((1, D_MODEL), lambda i: (0, 0)),
        ],
        out_specs=pl.BlockSpec((tm, D_MODEL), lambda i: (i, 0)),
        out_shape=jax.ShapeDtypeStruct((m, D_MODEL), F32),
        compiler_params=_params(("arbitrary",), 48),
        name="out_proj",
    )(merged, w_out, x, ln_g, ln_b)


def kernel(x_prompt, x_sample, mem_prompt, cache_mem_k, cache_mem_v, state_conv, state_lru_h, ffn1_w_gu, ffn1_w_down, ln1_g, ln1_b, w_in, gate_b, gmlp_ln_g, gmlp_ln_b, gmlp_w_s, gmlp_b_s, conv_w, conv_b, lru_w_a, lru_b_a, lru_w_x, lru_b_x, lru_lambda, mem_ln_g, mem_ln_b, w_mem_kv, w_branch, w_out, ln2_g, ln2_b, ffn2_w_gu, ffn2_w_down, ln3_g, ln3_b):
    bsz, seq, _ = x_prompt.shape
    n_s = x_sample.shape[0]
    l = 0

    wa, wx = lru_w_a[l].astype(BF16), lru_w_x[l].astype(BF16)

    row = lambda p: p[l].reshape(1, -1)
    gb = gate_b[l].reshape(1, -1)
    bs_mat = jnp.repeat(gmlp_b_s[l].T, GROUP_W, axis=1)
    wvec = jnp.repeat(gmlp_w_s[l][:, 0, 0], GROUP_W).reshape(1, -1)
    bvec = jnp.repeat(gmlp_b_s[l][:, 0], GROUP_W).reshape(1, -1)
    lru = (wa, row(lru_b_a), wx, row(lru_b_x), row(lru_lambda))

    xp = x_prompt.reshape(bsz * seq, D_MODEL)
    xs = x_sample.reshape(n_s, D_MODEL)

    x1s, w1g, w1u, w1d = _ffn(xs, ffn1_w_gu[l], ffn1_w_gu[l], ffn1_w_down[l], row(ln1_g), row(ln1_b),
                              tm=n_s, emit_bf16=True)
    k_p, v_p = _memkv(mem_prompt.reshape(bsz * N_MEM, D_MODEL), w_mem_kv[l], row(mem_ln_g), row(mem_ln_b))
    x1p, win, w2d = _ffn(xp, w1g, w1u, w1d, row(ln1_g), row(ln1_b), tm=1024, side=(w_in[l], ffn2_w_down[l]))

    zs, rxs = _inproj(x1s, win, gb, row(gmlp_ln_g), row(gmlp_ln_b), tm=n_s, z_dtype=F32)
    y01, conv_s, h_s = _mixer_sample(
        zs, rxs, jnp.swapaxes(state_conv[l], 0, 1), state_lru_h[l], wvec, bvec, conv_w[l], row(conv_b), *lru)
    q_s = zs[:, Z_Q * BR_W:(Z_Q + 1) * BR_W].reshape(n_s, XA_HEADS, XA_HEAD_DIM)
    yxa = _xattn_sample(_head_rows(q_s), _head_rows(cache_mem_k[l]), _head_rows(cache_mem_v[l]))
    yxa = jnp.swapaxes(yxa.reshape(n_s, HEAD_SPLIT, XA_HEADS, 128), 1, 2)
    ys = jnp.concatenate([y01, yxa.reshape(1, n_s, BR_W)], axis=0)

    zp, rxp = _inproj(x1p, win, gb, row(gmlp_ln_g), row(gmlp_ln_b), tm=1024, z_dtype=BF16)
    yp, conv_p, hlast_p, wbr, wout, w2gu = _mixer_prompt(
        zp, rxp, k_p.reshape(bsz, N_MEM, BR_W), v_p.reshape(bsz, N_MEM, BR_W), gmlp_w_s[l], bs_mat,
        conv_w[l], row(conv_b), *lru, bsz=bsz, seq=seq,
        side=(w_branch[l].reshape(N_BRANCH * BR_W, D_MODEL), w_out[l], ffn2_w_gu[l]))
    wbr = wbr.reshape(N_BRANCH, BR_W, D_MODEL)

    ms = _merge(ys, wbr, zs, tm=n_s)
    x2s = _outproj(ms, wout, x1s, row(ln2_g), row(ln2_b), tm=n_s)
    x3s = _ffn(x2s, w2gu, w2gu, w2d, row(ln3_g), row(ln3_b), tm=n_s)
    mp = _merge(yp, wbr, zp, tm=1024)
    x2p = _outproj(mp, wout, x1p, row(ln2_g), row(ln2_b), tm=512)
    x3p = _ffn(x2p, w2gu, w2gu, w2d, row(ln3_g), row(ln3_b), tm=1024)

    kv_shape = (1, bsz, N_MEM, XA_HEADS, XA_HEAD_DIM)
    return (x3p.reshape(bsz, seq, D_MODEL),
            x3s.reshape(n_s, 1, D_MODEL),
            k_p.reshape(kv_shape),
            v_p.reshape(kv_shape),
            conv_p[None],
            hlast_p.reshape(1, bsz, BR_W),
            jnp.swapaxes(conv_s, 0, 1)[None],
            h_s[None],
            zs[:, Z_V * BR_W:(Z_V + 1) * BR_W].reshape(1, n_s, 1, BR_W))
```

```python
import functools

import jax
import jax.numpy as jnp
from jax import lax
from jax.experimental import pallas as pl
from jax.experimental.pallas import tpu as pltpu

F32 = jnp.float32
BF16 = jnp.bfloat16

D_MODEL = 2048
BR_W = 1024
D_FF = 5632
N_MEM = 256
XA_HEADS = 4
XA_HEAD_DIM = 256
GMLP_GROUPS = 4
GROUP_W = BR_W // GMLP_GROUPS
CHUNK = 128
LRU_BLOCKS = 8
LRU_BLOCK = 128
CONV_W = 4
LRU_C = 8.0
LN_EPS = 1e-5
ALPHA = 2.0 ** 0.25
IN_BLOCKS = 11
RX_BLOCK = 2
GATE_BLOCK0 = 5
Z_BLOCKS = IN_BLOCKS - 1
Z_GU, Z_V, Z_GRG, Z_Q, Z_GATE0 = 0, 1, 2, 3, 4
SUBLANES = 8
FFN_ROW_GROUP = 512
OUTPROJ_ROW_GROUP = 128
MiB = 1024 * 1024


def _params(semantics, vmem_mib):
    return pltpu.CompilerParams(dimension_semantics=semantics, vmem_limit_bytes=vmem_mib * MiB)


def _layer_norm(x, g, b):
    mu = jnp.mean(x, axis=-1, keepdims=True)
    xc = x - mu
    var = jnp.mean(xc * xc, axis=-1, keepdims=True)
    return xc * lax.rsqrt(var + LN_EPS) * g + b


def _dot(a, b):
    return jnp.dot(a, b, preferred_element_type=F32)


def _sigmoid(x):
    return 0.5 * jnp.tanh(0.5 * x) + 0.5


def _mxu_operand(w_ref, w16_ref=None):
    w = w_ref[...].astype(BF16)
    if w16_ref is not None:
        w16_ref[...] = w
    return w


def _side_cast_plan(side, grid):
    n_steps = 1
    for g in grid:
        n_steps *= g

    def linear(*idx):
        lin = idx[0]
        for g, i in zip(grid[1:], idx[1:]):
            lin = lin * g + i
        return lin

    specs, shapes = [], []
    for w in side:
        r, c = w.shape
        if c % (n_steps * 128) == 0:
            spec = pl.BlockSpec((r, c // n_steps), lambda *idx: (0, linear(*idx)))
        else:
            assert r % (n_steps * 16) == 0, (w.shape, n_steps)
            spec = pl.BlockSpec((r // n_steps, c), lambda *idx: (linear(*idx), 0))
        specs.append(spec)
        shapes.append(jax.ShapeDtypeStruct(w.shape, BF16))
    return specs, shapes


def _side_cast(in_refs, out_refs):
    for src, dst in zip(in_refs, out_refs):
        dst[...] = src[...].astype(BF16)


def _ffn_kernel(x_ref, wg_ref, wu_ref, wd_ref, g_ref, b_ref, *rest, emit_bf16, row_splits, n_side):
    side_in, rest = rest[:n_side], rest[n_side:]
    o_ref, rest = rest[0], rest[1:]
    if emit_bf16:
        (wg16_ref, wu16_ref, wd16_ref), rest = rest[:3], rest[3:]
    else:
        wg16_ref = wu16_ref = wd16_ref = None
    side_out, (xb_ref,) = rest[:n_side], rest[n_side:]
    _side_cast(side_in, side_out)
    j = pl.program_id(1)
    last = pl.num_programs(1) - 1
    rows_per = xb_ref.shape[0] // row_splits

    def step(first, final):
        wg = _mxu_operand(wg_ref, wg16_ref)
        wu = _mxu_operand(wu_ref, wu16_ref)
        wd = _mxu_operand(wd_ref, wd16_ref)
        for r in range(row_splits):
            rows = slice(r * rows_per, (r + 1) * rows_per)
            if first:
                xb = x_ref[rows, :].astype(BF16)
                xb_ref[rows, :] = xb
            else:
                xb = xb_ref[rows, :]
            g = _dot(xb, wg)
            u = _dot(xb, wu)
            h = (g * _sigmoid(g) * u).astype(BF16)
            acc = _dot(h, wd)
            if not first:
                acc = o_ref[rows, :] + acc
            if final:
                acc = _layer_norm(ALPHA * x_ref[rows, :] + 0.5 * acc, g_ref[...], b_ref[...])
            o_ref[rows, :] = acc

    pl.when(j == 0)(functools.partial(step, True, False))
    pl.when((j > 0) & (j < last))(functools.partial(step, False, False))
    pl.when(j == last)(functools.partial(step, False, True))


def _ffn(x, w_g, w_u, w_down, ln_g, ln_b, *, tm, emit_bf16=False, side=(), tf=512):
    m = x.shape[0]
    nf = D_FF // tf
    grid = (m // tm, nf)
    u_off = nf if w_u.shape[1] == 2 * D_FF else 0
    assert not emit_bf16 or m == tm
    side_specs, side_shapes = _side_cast_plan(side, grid)
    out_specs = [pl.BlockSpec((tm, D_MODEL), lambda i, j: (i, 0))]
    out_shape = [jax.ShapeDtypeStruct((m, D_MODEL), F32)]
    if emit_bf16:
        out_specs += [pl.BlockSpec((D_MODEL, tf), lambda i, j: (0, j)),
                      pl.BlockSpec((D_MODEL, tf), lambda i, j: (0, j)),
                      pl.BlockSpec((tf, D_MODEL), lambda i, j: (j, 0))]
        out_shape += [jax.ShapeDtypeStruct((D_MODEL, D_FF), BF16),
                      jax.ShapeDtypeStruct((D_MODEL, D_FF), BF16),
                      jax.ShapeDtypeStruct((D_FF, D_MODEL), BF16)]
    outs = pl.pallas_call(
        functools.partial(_ffn_kernel, emit_bf16=emit_bf16, row_splits=max(1, tm // FFN_ROW_GROUP),
                          n_side=len(side)),
        grid=grid,
        in_specs=[
            pl.BlockSpec((tm, D_MODEL), lambda i, j: (i, 0)),
            pl.BlockSpec((D_MODEL, tf), lambda i, j: (0, j)),
            pl.BlockSpec((D_MODEL, tf), lambda i, j: (0, j + u_off)),
            pl.BlockSpec((tf, D_MODEL), lambda i, j: (j, 0)),
            pl.BlockSpec((1, D_MODEL), lambda i, j: (0, 0)),
            pl.BlockSpec((1, D_MODEL), lambda i, j: (0, 0)),
        ] + side_specs,
        out_specs=out_specs + side_specs,
        out_shape=out_shape + side_shapes,
        scratch_shapes=[pltpu.VMEM((tm, D_MODEL), BF16)],
        compiler_params=_params(("arbitrary", "arbitrary"), 62),
        name="ffn_ln",
    )(x, w_g, w_u, w_down, ln_g, ln_b, *side)
    return outs if (emit_bf16 or side) else outs[0]


def _memkv_kernel(x_ref, wk_ref, wv_ref, g_ref, b_ref, k_ref, v_ref, xb_ref):
    @pl.when(pl.program_id(1) == 0)
    def _():
        xb_ref[...] = _layer_norm(x_ref[...], g_ref[...], b_ref[...]).astype(BF16)

    xb = xb_ref[...]
    k_ref[...] = _dot(xb, _mxu_operand(wk_ref))
    v_ref[...] = _dot(xb, _mxu_operand(wv_ref))


def _memkv(mem, w_kv, ln_g, ln_b, *, tm=1024, tn=512):
    m = mem.shape[0]
    nn = BR_W // tn
    return pl.pallas_call(
        _memkv_kernel,
        grid=(m // tm, nn),
        in_specs=[
            pl.BlockSpec((tm, D_MODEL), lambda i, j: (i, 0)),
            pl.BlockSpec((D_MODEL, tn), lambda i, j: (0, j)),
            pl.BlockSpec((D_MODEL, tn), lambda i, j: (0, j + nn)),
            pl.BlockSpec((1, D_MODEL), lambda i, j: (0, 0)),
            pl.BlockSpec((1, D_MODEL), lambda i, j: (0, 0)),
        ],
        out_specs=[pl.BlockSpec((tm, tn), lambda i, j: (i, j)),
                   pl.BlockSpec((tm, tn), lambda i, j: (i, j))],
        out_shape=[jax.ShapeDtypeStruct((m, BR_W), F32), jax.ShapeDtypeStruct((m, BR_W), F32)],
        scratch_shapes=[pltpu.VMEM((tm, D_MODEL), BF16)],
        compiler_params=_params(("arbitrary", "arbitrary"), 48),
        name="mem_kv",
    )(mem, w_kv, w_kv, ln_g, ln_b)


def _inproj_kernel(x_ref, w_ref, gb_ref, lg_ref, lb_ref, *rest, n_side):
    side_in, rest = rest[:n_side], rest[n_side:]
    (o_ref, rx_ref), rest = rest[:2], rest[2:]
    side_out, (xb_ref,) = rest[:n_side], rest[n_side:]
    _side_cast(side_in, side_out)
    j = pl.program_id(1)

    @pl.when(j == 0)
    def _():
        xb_ref[...] = x_ref[...].astype(BF16)

    def emit(dst_ref, act):
        dst_ref[...] = act(_dot(xb_ref[...], w_ref[...])).astype(dst_ref.dtype)

    @pl.when((j == 0) | (j == 3))
    def _():
        emit(o_ref, jax.nn.gelu)

    @pl.when(j == 1)
    def _():
        emit(o_ref, lambda z: _layer_norm(jax.nn.gelu(z), lg_ref[...], lb_ref[...]))

    @pl.when(j == 2)
    def _():
        emit(rx_ref, lambda z: z)

    @pl.when(j == 4)
    def _():
        emit(o_ref, lambda z: z)

    @pl.when(j >= GATE_BLOCK0)
    def _():
        emit(o_ref, lambda z: _sigmoid(z + gb_ref[...]))


def _inproj(x, w_in, gate_b, gln_g, gln_b, *, tm, z_dtype, side=()):
    m = x.shape[0]
    grid = (m // tm, IN_BLOCKS)
    side_specs, side_shapes = _side_cast_plan(side, grid)
    out_specs = [pl.BlockSpec((tm, BR_W), lambda i, j: (i, jnp.where(j < RX_BLOCK, j, j - 1))),
                 pl.BlockSpec((tm, BR_W), lambda i, j: (i, 0))]
    out_shape = [jax.ShapeDtypeStruct((m, Z_BLOCKS * BR_W), z_dtype),
                 jax.ShapeDtypeStruct((m, BR_W), F32)]
    return pl.pallas_call(
        functools.partial(_inproj_kernel, n_side=len(side)),
        grid=grid,
        in_specs=[
            pl.BlockSpec((tm, D_MODEL), lambda i, j: (i, 0)),
            pl.BlockSpec((D_MODEL, BR_W), lambda i, j: (0, j)),
            pl.BlockSpec((1, BR_W), lambda i, j: (0, jnp.maximum(j - GATE_BLOCK0, 0))),
            pl.BlockSpec((1, BR_W), lambda i, j: (0, 0)),
            pl.BlockSpec((1, BR_W), lambda i, j: (0, 0)),
        ] + side_specs,
        out_specs=out_specs + side_specs,
        out_shape=out_shape + side_shapes,
        scratch_shapes=[pltpu.VMEM((tm, D_MODEL), BF16)],
        compiler_params=_params(("arbitrary", "arbitrary"), 56),
        name="in_proj",
    )(x, w_in, gate_b, gln_g, gln_b, *side)


def _softplus(x):
    return jnp.maximum(x, 0.0) + jnp.log1p(jnp.exp(-jnp.abs(x)))


def _lru_coeffs(xc, wa_ref, ba_ref, wx_ref, bx_ref, lam_ref):
    xcb = xc.astype(BF16)
    r_parts, i_parts = [], []
    for k in range(LRU_BLOCKS):
        xk = xcb[:, k * LRU_BLOCK:(k + 1) * LRU_BLOCK]
        r_parts.append(_dot(xk, wa_ref[k]))
        i_parts.append(_dot(xk, wx_ref[k]))
    r = _sigmoid(jnp.concatenate(r_parts, axis=1) + ba_ref[...])
    i = _sigmoid(jnp.concatenate(i_parts, axis=1) + bx_ref[...])
    log_a = (-LRU_C) * r * _softplus(-lam_ref[...])
    a = jnp.exp(log_a)
    m = -jnp.tanh(log_a) * (a * a + 1.0)
    root = jnp.where(m > 0.0, m * lax.rsqrt(m), 0.0)
    return a, root * (i * xc)


def _mixer_kernel(gu_ref, v_ref, rx_ref, grg_ref, q_ref, k_ref, vm_ref,
                  ws_ref, bs_ref, cw_ref, cb_ref, wa_ref, ba_ref, wx_ref, bx_ref, lam_ref,
                  *rest, tm, n_side):
    side_in, rest = rest[:n_side], rest[n_side:]
    (y_ref, conv_ref, hlast_ref), rest = rest[:3], rest[3:]
    side_out, (xpad_ref, a_ref, b_ref, h_ref, hc_ref) = rest[:n_side], rest[n_side:]
    _side_cast(side_in, side_out)
    t = pl.program_id(1)

    tri = (lax.broadcasted_iota(jnp.int32, (CHUNK, CHUNK), 0)
           >= lax.broadcasted_iota(jnp.int32, (CHUNK, CHUNK), 1))
    for g in range(GMLP_GROUPS):
        wg = jnp.where(tri, ws_ref[g], 0.0).astype(BF16)
        cols = slice(g * GROUP_W, (g + 1) * GROUP_W)
        for c in range(tm // CHUNK):
            rows = slice(c * CHUNK, (c + 1) * CHUNK)
            s = _dot(wg, v_ref[rows, cols].astype(BF16)) + bs_ref[:, cols]
            y_ref[0, rows, cols] = (gu_ref[rows, cols] * s).astype(BF16)

    @pl.when(t == 0)
    def _():
        xpad_ref[0:SUBLANES, :] = jnp.zeros((SUBLANES, BR_W), F32)
        hc_ref[...] = jnp.zeros((1, BR_W), F32)

    @pl.when(t > 0)
    def _():
        xpad_ref[0:SUBLANES, :] = xpad_ref[tm:tm + SUBLANES, :]

    xpad_ref[SUBLANES:SUBLANES + tm, :] = rx_ref[...]
    xfull = xpad_ref[...]
    acc = xfull * cw_ref[0:1, :]
    for k in range(1, CONV_W):
        acc = pltpu.roll(acc, 1, 0) + xfull * cw_ref[k:k + 1, :]
    xc = acc[SUBLANES:, :] + cb_ref[...]

    a, b = _lru_coeffs(xc, wa_ref, ba_ref, wx_ref, bx_ref, lam_ref)
    a_ref[...] = a
    b_ref[...] = b

    row = lax.broadcasted_iota(jnp.int32, (SUBLANES, BR_W), 0)
    keep = [row >= d for d in (1, 2, 4)]

    def scan_block(blk, h):
        base = pl.multiple_of(blk * SUBLANES, SUBLANES)
        ca = a_ref[pl.ds(base, SUBLANES), :]
        cb = b_ref[pl.ds(base, SUBLANES), :]
        for d, kp in zip((1, 2, 4), keep):
            a_sh = jnp.where(kp, pltpu.roll(ca, d, 0), 1.0)
            b_sh = jnp.where(kp, pltpu.roll(cb, d, 0), 0.0)
            cb = ca * b_sh + cb
            ca = ca * a_sh
        hh = cb + ca * h
        h_ref[pl.ds(base, SUBLANES), :] = hh
        return hh[SUBLANES - 1:SUBLANES, :]

    h_end = lax.fori_loop(0, tm // SUBLANES, scan_block, hc_ref[...])
    hc_ref[...] = h_end
    y_ref[1] = (grg_ref[...] * h_ref[...]).astype(BF16)

    @pl.when(t == pl.num_programs(1) - 1)
    def _():
        conv_ref[0] = rx_ref[tm - (CONV_W - 1):tm, :]
        hlast_ref[0] = h_end

    scale = XA_HEAD_DIM ** -0.5
    for hd in range(XA_HEADS):
        cols = slice(hd * XA_HEAD_DIM, (hd + 1) * XA_HEAD_DIM)
        qh = q_ref[:, cols].astype(BF16)
        kh = k_ref[0, :, cols].astype(BF16)
        vh = vm_ref[0, :, cols].astype(BF16)
        s = lax.dot_general(qh, kh, (((1,), (1,)), ((), ())), preferred_element_type=F32) * scale
        e = jnp.exp(s - jnp.max(s, axis=-1, keepdims=True))
        p = e * (1.0 / jnp.sum(e, axis=-1, keepdims=True))
        y_ref[2, :, cols] = _dot(p.astype(BF16), vh).astype(BF16)


def _mixer_prompt(z, rx, k_mem, v_mem, ws, bs_mat, conv_w, conv_b, wa, ba, wx, bx, lam, *, bsz, seq, side=(),
                  tm=512):
    nt = seq // tm
    zspec = lambda c: pl.BlockSpec((tm, BR_W), lambda b, t, c=c: (b * nt + t, c))
    full = lambda shape: pl.BlockSpec(shape, lambda b, t: (0,) * len(shape))
    side_specs, side_shapes = _side_cast_plan(side, (bsz, nt))
    return pl.pallas_call(
        functools.partial(_mixer_kernel, tm=tm, n_side=len(side)),
        grid=(bsz, nt),
        in_specs=[zspec(Z_GU), zspec(Z_V), zspec(0), zspec(Z_GRG), zspec(Z_Q),
                  pl.BlockSpec((1, N_MEM, BR_W), lambda b, t: (b, 0, 0)),
                  pl.BlockSpec((1, N_MEM, BR_W), lambda b, t: (b, 0, 0)),
                  full((GMLP_GROUPS, CHUNK, CHUNK)), full((CHUNK, BR_W)),
                  full((CONV_W, BR_W)), full((1, BR_W)),
                  full((LRU_BLOCKS, LRU_BLOCK, LRU_BLOCK)), full((1, BR_W)),
                  full((LRU_BLOCKS, LRU_BLOCK, LRU_BLOCK)), full((1, BR_W)),
                  full((1, BR_W))] + side_specs,
        out_specs=[pl.BlockSpec((3, tm, BR_W), lambda b, t: (0, b * nt + t, 0)),
                   pl.BlockSpec((1, CONV_W - 1, BR_W), lambda b, t: (b, 0, 0)),
                   pl.BlockSpec((1, 1, BR_W), lambda b, t: (b, 0, 0))] + side_specs,
        out_shape=[jax.ShapeDtypeStruct((3, bsz * seq, BR_W), BF16),
                   jax.ShapeDtypeStruct((bsz, CONV_W - 1, BR_W), F32),
                   jax.ShapeDtypeStruct((bsz, 1, BR_W), F32)] + side_shapes,
        scratch_shapes=[pltpu.VMEM((tm + SUBLANES, BR_W), F32),
                        pltpu.VMEM((tm, BR_W), F32), pltpu.VMEM((tm, BR_W), F32),
                        pltpu.VMEM((tm, BR_W), F32), pltpu.VMEM((1, BR_W), F32)],
        compiler_params=_params(("arbitrary", "arbitrary"), 48),
        name="mixer_prompt",
    )(z, z, rx, z, z, k_mem, v_mem, ws, bs_mat, conv_w, conv_b, wa, ba, wx, bx, lam, *side)


def _mixer_sample_kernel(gu_ref, v_ref, rx_ref, grg_ref, cs_ref, h0_ref,
                         wv_ref, bv_ref, cw_ref, cb_ref, wa_ref, ba_ref, wx_ref, bx_ref, lam_ref,
                         y_ref, conv_ref, h_ref):
    y_ref[0] = (gu_ref[...] * (wv_ref[...] * v_ref[...] + bv_ref[...])).astype(BF16)

    rx = rx_ref[...]
    xc = cb_ref[...] + rx * cw_ref[CONV_W - 1:CONV_W, :]
    for k in range(CONV_W - 1):
        xc = xc + cs_ref[k] * cw_ref[k:k + 1, :]
    for k in range(CONV_W - 2):
        conv_ref[k] = cs_ref[k + 1]
    conv_ref[CONV_W - 2] = rx

    a, b = _lru_coeffs(xc, wa_ref, ba_ref, wx_ref, bx_ref, lam_ref)
    h = a * h0_ref[...] + b
    h_ref[...] = h
    y_ref[1] = (grg_ref[...] * h).astype(BF16)


def _mixer_sample(z, rx, conv_state, h0, wvec, bvec, conv_w, conv_b, wa, ba, wx, bx, lam):
    n = z.shape[0]
    zspec = lambda c: pl.BlockSpec((n, BR_W), lambda i, c=c: (0, c))
    full = lambda shape: pl.BlockSpec(shape, lambda i: (0,) * len(shape))
    return pl.pallas_call(
        _mixer_sample_kernel,
        grid=(1,),
        in_specs=[zspec(Z_GU), zspec(Z_V), zspec(0), zspec(Z_GRG),
                  full((CONV_W - 1, n, BR_W)), full((n, BR_W)),
                  full((1, BR_W)), full((1, BR_W)), full((CONV_W, BR_W)), full((1, BR_W)),
                  full((LRU_BLOCKS, LRU_BLOCK, LRU_BLOCK)), full((1, BR_W)),
                  full((LRU_BLOCKS, LRU_BLOCK, LRU_BLOCK)), full((1, BR_W)),
                  full((1, BR_W))],
        out_specs=[full((2, n, BR_W)), full((CONV_W - 1, n, BR_W)), full((n, BR_W))],
        out_shape=[jax.ShapeDtypeStruct((2, n, BR_W), BF16),
                   jax.ShapeDtypeStruct((CONV_W - 1, n, BR_W), F32),
                   jax.ShapeDtypeStruct((n, BR_W), F32)],
        compiler_params=_params(("arbitrary",), 32),
        name="mixer_sample",
    )(z, z, rx, z, conv_state, h0, wvec, bvec, conv_w, conv_b, wa, ba, wx, bx, lam)


HEAD_SPLIT = XA_HEAD_DIM // 128
HEAD_ROWS = HEAD_SPLIT * XA_HEADS


def _head_rows(x):
    lead = x.shape[:-2]
    x = x.reshape(*lead, XA_HEADS, HEAD_SPLIT, 128)
    return jnp.swapaxes(x, -2, -3).reshape(*lead, HEAD_ROWS, 128)


def _xattn_sample_kernel(q_ref, k_ref, v_ref, o_ref, *, tb):
    scale = XA_HEAD_DIM ** -0.5
    n_rows = N_MEM * HEAD_ROWS
    lane = lax.broadcasted_iota(jnp.int32, (HEAD_ROWS, n_rows), 1)
    sub = lax.broadcasted_iota(jnp.int32, (HEAD_ROWS, n_rows), 0)
    own = (lane % HEAD_ROWS) == sub
    low_piece = (lax.broadcasted_iota(jnp.int32, (1, n_rows), 1) % HEAD_ROWS) < XA_HEADS

    def lane_group_reduce(x, op):
        shift = HEAD_ROWS
        while shift < 128:
            x = op(x, pltpu.roll(x, shift, 1))
            shift *= 2
        return x

    def per_kind(x, op):
        acc = x[:, 0:128]
        for c in range(1, n_rows // 128):
            acc = op(acc, x[:, c * 128:(c + 1) * 128])
        acc = lane_group_reduce(acc, op)
        return jnp.concatenate([acc] * (n_rows // 128), axis=1)

    parts = []
    for b in range(tb):
        qb = q_ref[b].astype(BF16)
        kb = k_ref[b].reshape(n_rows, 128).astype(BF16)
        c = lax.dot_general(qb, kb, (((1,), (1,)), ((), ())), preferred_element_type=F32)
        parts.append(jnp.sum(jnp.where(own, c, 0.0), axis=0, keepdims=True))
    part = jnp.concatenate(parts, axis=0)
    other = jnp.where(low_piece, pltpu.roll(part, n_rows - XA_HEADS, 1), pltpu.roll(part, XA_HEADS, 1))
    s = (part + other) * scale
    e = jnp.exp(s - per_kind(s, jnp.maximum))
    p = e * (1.0 / per_kind(e, jnp.add))
    for b in range(tb):
        vb = v_ref[b].reshape(n_rows, 128).astype(BF16)
        pm = jnp.where(own, jnp.broadcast_to(p[b:b + 1, :], (HEAD_ROWS, n_rows)), 0.0).astype(BF16)
        o_ref[b] = _dot(pm, vb).astype(BF16)


def _xattn_sample(q, k_cache, v_cache, *, tb=8):
    n = q.shape[0]
    kv_spec = pl.BlockSpec((tb, N_MEM, HEAD_ROWS, 128), lambda i: (i, 0, 0, 0))
    q_spec = pl.BlockSpec((tb, HEAD_ROWS, 128), lambda i: (i, 0, 0))
    return pl.pallas_call(
        functools.partial(_xattn_sample_kernel, tb=tb),
        grid=(n // tb,),
        in_specs=[q_spec, kv_spec, kv_spec],
        out_specs=q_spec,
        out_shape=jax.ShapeDtypeStruct((n, HEAD_ROWS, 128), BF16),
        compiler_params=_params(("arbitrary",), 48),
        name="xattn_sample",
    )(q, k_cache, v_cache)


N_BRANCH = 3


def _merge_kernel(y_ref, wb_ref, g0_ref, g1_ref, g2_ref, o_ref):
    acc = None
    for k, gate_ref in enumerate((g0_ref, g1_ref, g2_ref)):
        term = gate_ref[...].astype(F32) * _dot(y_ref[k], wb_ref[k])
        acc = term if acc is None else acc + term
    o_ref[...] = acc.astype(BF16)


def _merge(y, w_branch, z, *, tm):
    m = y.shape[1]
    gate_spec = lambda k: pl.BlockSpec((tm, BR_W), lambda i, n, k=k: (i, Z_GATE0 + 2 * k + n))
    return pl.pallas_call(
        _merge_kernel,
        grid=(m // tm, D_MODEL // BR_W),
        in_specs=[
            pl.BlockSpec((N_BRANCH, tm, BR_W), lambda i, n: (0, i, 0)),
            pl.BlockSpec((N_BRANCH, BR_W, BR_W), lambda i, n: (0, 0, n)),
            gate_spec(0), gate_spec(1), gate_spec(2),
        ],
        out_specs=pl.BlockSpec((tm, BR_W), lambda i, n: (i, n)),
        out_shape=jax.ShapeDtypeStruct((m, D_MODEL), BF16),
        compiler_params=_params(("arbitrary", "arbitrary"), 56),
        name="merge",
    )(y, w_branch, z, z, z)


def _outproj_kernel(m_ref, wo_ref, x_ref, g_ref, b_ref, o_ref, *, row_splits):
    rows_per = m_ref.shape[0] // row_splits
    for r in range(row_splits):
        rows = slice(r * rows_per, (r + 1) * rows_per)
        y = ALPHA * x_ref[rows, :] + _dot(m_ref[rows, :], wo_ref[...])
        o_ref[rows, :] = _layer_norm(y, g_ref[...], b_ref[...])


def _outproj(merged, w_out, x, ln_g, ln_b, *, tm):
    m = x.shape[0]
    return pl.pallas_call(
        functools.partial(_outproj_kernel, row_splits=max(1, tm // OUTPROJ_ROW_GROUP)),
        grid=(m // tm,),
        in_specs=[
            pl.BlockSpec((tm, D_MODEL), lambda i: (i, 0)),
            pl.BlockSpec((D_MODEL, D_MODEL), lambda i: (0, 0)),
            pl.BlockSpec((tm, D_MODEL), lambda i: (i, 0)),
            pl.BlockSpec((1, D_MODEL), lambda i: (0, 0)),
            pl.BlockSpec((1, D_MODEL), lambda i: (0, 0)),
        ],
        out_specs=pl.BlockSpec((tm, D_MODEL), lambda i: (i, 0)),
        out_shape=jax.ShapeDtypeStruct((m, D_MODEL), F32),
        compiler_params=_params(("arbitrary",), 48),
        name="out_proj",
    )(merged, w_out, x, ln_g, ln_b)


def kernel(x_prompt, x_sample, mem_prompt, cache_mem_k, cache_mem_v, state_conv, state_lru_h, ffn1_w_gu, ffn1_w_down, ln1_g, ln1_b, w_in, gate_b, gmlp_ln_g, gmlp_ln_b, gmlp_w_s, gmlp_b_s, conv_w, conv_b, lru_w_a, lru_b_a, lru_w_x, lru_b_x, lru_lambda, mem_ln_g, mem_ln_b, w_mem_kv, w_branch, w_out, ln2_g, ln2_b, ffn2_w_gu, ffn2_w_down, ln3_g, ln3_b):
    bsz, seq, _ = x_prompt.shape
    n_s = x_sample.shape[0]
    l = 0

    wa, wx = lru_w_a[l].astype(BF16), lru_w_x[l].astype(BF16)

    row = lambda p: p[l].reshape(1, -1)
    gb = gate_b[l].reshape(1, -1)
    bs_mat = jnp.repeat(gmlp_b_s[l].T, GROUP_W, axis=1)
    wvec = jnp.repeat(gmlp_w_s[l][:, 0, 0], GROUP_W).reshape(1, -1)
    bvec = jnp.repeat(gmlp_b_s[l][:, 0], GROUP_W).reshape(1, -1)
    lru = (wa, row(lru_b_a), wx, row(lru_b_x), row(lru_lambda))

    xp = x_prompt.reshape(bsz * seq, D_MODEL)
    xs = x_sample.reshape(n_s, D_MODEL)

    x1s, w1g, w1u, w1d = _ffn(xs, ffn1_w_gu[l], ffn1_w_gu[l], ffn1_w_down[l], row(ln1_g), row(ln1_b),
                              tm=n_s, emit_bf16=True)
    k_p, v_p = _memkv(mem_prompt.reshape(bsz * N_MEM, D_MODEL), w_mem_kv[l], row(mem_ln_g), row(mem_ln_b))
    x1p, win, w2d = _ffn(xp, w1g, w1u, w1d, row(ln1_g), row(ln1_b), tm=1024, side=(w_in[l], ffn2_w_down[l]))

    zs, rxs = _inproj(x1s, win, gb, row(gmlp_ln_g), row(gmlp_ln_b), tm=n_s, z_dtype=F32)
    y01, conv_s, h_s = _mixer_sample(
        zs, rxs, jnp.swapaxes(state_conv[l], 0, 1), state_lru_h[l], wvec, bvec, conv_w[l], row(conv_b), *lru)
    q_s = zs[:, Z_Q * BR_W:(Z_Q + 1) * BR_W].reshape(n_s, XA_HEADS, XA_HEAD_DIM)
    yxa = _xattn_sample(_head_rows(q_s), _head_rows(cache_mem_k[l]), _head_rows(cache_mem_v[l]))
    yxa = jnp.swapaxes(yxa.reshape(n_s, HEAD_SPLIT, XA_HEADS, 128), 1, 2)
    ys = jnp.concatenate([y01, yxa.reshape(1, n_s, BR_W)], axis=0)

    zp, rxp, w2gu = _inproj(x1p, win, gb, row(gmlp_ln_g), row(gmlp_ln_b), tm=1024, z_dtype=BF16,
                            side=(ffn2_w_gu[l],))
    yp, conv_p, hlast_p, wbr, wout = _mixer_prompt(
        zp, rxp, k_p.reshape(bsz, N_MEM, BR_W), v_p.reshape(bsz, N_MEM, BR_W), gmlp_w_s[l], bs_mat,
        conv_w[l], row(conv_b), *lru, bsz=bsz, seq=seq,
        side=(w_branch[l].reshape(N_BRANCH * BR_W, D_MODEL), w_out[l]))
    wbr = wbr.reshape(N_BRANCH, BR_W, D_MODEL)

    ms = _merge(ys, wbr, zs, tm=n_s)
    x2s = _outproj(ms, wout, x1s, row(ln2_g), row(ln2_b), tm=n_s)
    x3s = _ffn(x2s, w2gu, w2gu, w2d, row(ln3_g), row(ln3_b), tm=n_s)
    mp = _merge(yp, wbr, zp, tm=1024)
    x2p = _outproj(mp, wout, x1p, row(ln2_g), row(ln2_b), tm=512)
    x3p = _ffn(x2p, w2gu, w2gu, w2d, row(ln3_g), row(ln3_b), tm=1024)

    kv_shape = (1, bsz, N_MEM, XA_HEADS, XA_HEAD_DIM)
    return (x3p.reshape(bsz, seq, D_MODEL),
            x3s.reshape(n_s, 1, D_MODEL),
            k_p.reshape(kv_shape),
            v_p.reshape(kv_shape),
            conv_p[None],
            hlast_p.reshape(1, bsz, BR_W),
            jnp.swapaxes(conv_s, 0, 1)[None],
            h_s[None],
            zs[:, Z_V * BR_W:(Z_V + 1) * BR_W].reshape(1, n_s, 1, BR_W))
```

```python
import functools

import jax
import jax.numpy as jnp
from jax import lax
from jax.experimental import pallas as pl
from jax.experimental.pallas import tpu as pltpu

F32 = jnp.float32
BF16 = jnp.bfloat16

D_MODEL = 2048
BR_W = 1024
D_FF = 5632
N_MEM = 256
XA_HEADS = 4
XA_HEAD_DIM = 256
GMLP_GROUPS = 4
GROUP_W = BR_W // GMLP_GROUPS
CHUNK = 128
LRU_BLOCKS = 8
LRU_BLOCK = 128
CONV_W = 4
LRU_C = 8.0
LN_EPS = 1e-5
ALPHA = 2.0 ** 0.25
IN_BLOCKS = 11
RX_BLOCK = 2
GATE_BLOCK0 = 5
Z_BLOCKS = IN_BLOCKS - 1
Z_GU, Z_V, Z_GRG, Z_Q, Z_GATE0 = 0, 1, 2, 3, 4
SUBLANES = 8
FFN_ROW_GROUP = 512
FFN_COL_BLOCK = 512
OUTPROJ_ROW_GROUP = 128
MiB = 1024 * 1024


def _params(semantics, vmem_mib):
    return pltpu.CompilerParams(dimension_semantics=semantics, vmem_limit_bytes=vmem_mib * MiB)


def _layer_norm(x, g, b):
    mu = jnp.mean(x, axis=-1, keepdims=True)
    xc = x - mu
    var = jnp.mean(xc * xc, axis=-1, keepdims=True)
    return xc * lax.rsqrt(var + LN_EPS) * g + b


def _dot(a, b):
    return jnp.dot(a, b, preferred_element_type=F32)


def _sigmoid(x):
    return 0.5 * jnp.tanh(0.5 * x) + 0.5


def _mxu_operand(w_ref, w16_ref=None):
    w = w_ref[...].astype(BF16)
    if w16_ref is not None:
        w16_ref[...] = w
    return w


def _side_cast_plan(side, grid):
    n_steps = 1
    for g in grid:
        n_steps *= g

    def linear(*idx):
        lin = idx[0]
        for g, i in zip(grid[1:], idx[1:]):
            lin = lin * g + i
        return lin

    in_specs, out_specs, shapes = [], [], []
    for entry in side:
        w, col_block = entry if isinstance(entry, tuple) else (entry, None)
        r, c = w.shape
        if c % (n_steps * 128) == 0:
            cw = c // n_steps
            in_spec = pl.BlockSpec((r, cw), lambda *idx: (0, linear(*idx)))
        else:
            assert col_block is None and r % (n_steps * 16) == 0, (w.shape, n_steps)
            in_spec = pl.BlockSpec((r // n_steps, c), lambda *idx: (linear(*idx), 0))
        if col_block is None:
            out_spec, shape = in_spec, w.shape
        else:
            per = col_block // cw
            out_spec = pl.BlockSpec((None, r, cw), lambda *idx, per=per: (linear(*idx) // per, 0, linear(*idx) % per))
            shape = (c // col_block, r, col_block)
        in_specs.append(in_spec)
        out_specs.append(out_spec)
        shapes.append(jax.ShapeDtypeStruct(shape, BF16))
    return in_specs, out_specs, shapes


def _side_cast(in_refs, out_refs):
    for src, dst in zip(in_refs, out_refs):
        dst[...] = src[...].astype(BF16)


def _ffn_kernel(x_ref, wg_ref, wu_ref, wd_ref, g_ref, b_ref, *rest, emit_bf16, row_splits, n_side):
    side_in, rest = rest[:n_side], rest[n_side:]
    o_ref, rest = rest[0], rest[1:]
    if emit_bf16:
        (wg16_ref, wu16_ref, wd16_ref), rest = rest[:3], rest[3:]
    else:
        wg16_ref = wu16_ref = wd16_ref = None
    side_out, (xb_ref,) = rest[:n_side], rest[n_side:]
    _side_cast(side_in, side_out)
    j = pl.program_id(1)
    last = pl.num_programs(1) - 1
    rows_per = xb_ref.shape[0] // row_splits

    def step(first, final):
        wg = _mxu_operand(wg_ref, wg16_ref)
        wu = _mxu_operand(wu_ref, wu16_ref)
        wd = _mxu_operand(wd_ref, wd16_ref)
        for r in range(row_splits):
            rows = slice(r * rows_per, (r + 1) * rows_per)
            if first:
                xb = x_ref[rows, :].astype(BF16)
                xb_ref[rows, :] = xb
            else:
                xb = xb_ref[rows, :]
            g = _dot(xb, wg)
            u = _dot(xb, wu)
            h = (g * _sigmoid(g) * u).astype(BF16)
            acc = _dot(h, wd)
            if not first:
                acc = o_ref[rows, :] + acc
            if final:
                acc = _layer_norm(ALPHA * x_ref[rows, :] + 0.5 * acc, g_ref[...], b_ref[...])
            o_ref[rows, :] = acc

    pl.when(j == 0)(functools.partial(step, True, False))
    pl.when((j > 0) & (j < last))(functools.partial(step, False, False))
    pl.when(j == last)(functools.partial(step, False, True))


def _ffn(x, w_g, w_u, w_down, ln_g, ln_b, *, tm, emit_bf16=False, side=(), tf=FFN_COL_BLOCK):
    m = x.shape[0]
    nf = D_FF // tf
    grid = (m // tm, nf)
    assert not emit_bf16 or m == tm
    side_in_specs, side_out_specs, side_shapes = _side_cast_plan(side, grid)
    side_arrays = [e[0] if isinstance(e, tuple) else e for e in side]

    def gu_spec(w, is_u):
        if w.ndim == 3:
            off = nf if (is_u and w.shape[0] == 2 * nf) else 0
            return pl.BlockSpec((None, D_MODEL, tf), lambda i, j: (j + off, 0, 0))
        off = nf if (is_u and w.shape[1] == 2 * D_FF) else 0
        return pl.BlockSpec((D_MODEL, tf), lambda i, j: (0, j + off))

    out_specs = [pl.BlockSpec((tm, D_MODEL), lambda i, j: (i, 0))]
    out_shape = [jax.ShapeDtypeStruct((m, D_MODEL), F32)]
    if emit_bf16:
        blocked = pl.BlockSpec((None, D_MODEL, tf), lambda i, j: (j, 0, 0))
        out_specs += [blocked, blocked, pl.BlockSpec((tf, D_MODEL), lambda i, j: (j, 0))]
        out_shape += [jax.ShapeDtypeStruct((nf, D_MODEL, tf), BF16),
                      jax.ShapeDtypeStruct((nf, D_MODEL, tf), BF16),
                      jax.ShapeDtypeStruct((D_FF, D_MODEL), BF16)]
    outs = pl.pallas_call(
        functools.partial(_ffn_kernel, emit_bf16=emit_bf16, row_splits=max(1, tm // FFN_ROW_GROUP),
                          n_side=len(side)),
        grid=grid,
        in_specs=[
            pl.BlockSpec((tm, D_MODEL), lambda i, j: (i, 0)),
            gu_spec(w_g, False),
            gu_spec(w_u, True),
            pl.BlockSpec((tf, D_MODEL), lambda i, j: (j, 0)),
            pl.BlockSpec((1, D_MODEL), lambda i, j: (0, 0)),
            pl.BlockSpec((1, D_MODEL), lambda i, j: (0, 0)),
        ] + side_in_specs,
        out_specs=out_specs + side_out_specs,
        out_shape=out_shape + side_shapes,
        scratch_shapes=[pltpu.VMEM((tm, D_MODEL), BF16)],
        compiler_params=_params(("arbitrary", "arbitrary"), 62),
        name="ffn_ln",
    )(x, w_g, w_u, w_down, ln_g, ln_b, *side_arrays)
    return outs if (emit_bf16 or side) else outs[0]


def _memkv_kernel(x_ref, wk_ref, wv_ref, g_ref, b_ref, k_ref, v_ref, xb_ref):
    @pl.when(pl.program_id(1) == 0)
    def _():
        xb_ref[...] = _layer_norm(x_ref[...], g_ref[...], b_ref[...]).astype(BF16)

    xb = xb_ref[...]
    k_ref[...] = _dot(xb, _mxu_operand(wk_ref))
    v_ref[...] = _dot(xb, _mxu_operand(wv_ref))


def _memkv(mem, w_kv, ln_g, ln_b, *, tm=1024, tn=512):
    m = mem.shape[0]
    nn = BR_W // tn
    return pl.pallas_call(
        _memkv_kernel,
        grid=(m // tm, nn),
        in_specs=[
            pl.BlockSpec((tm, D_MODEL), lambda i, j: (i, 0)),
            pl.BlockSpec((D_MODEL, tn), lambda i, j: (0, j)),
            pl.BlockSpec((D_MODEL, tn), lambda i, j: (0, j + nn)),
            pl.BlockSpec((1, D_MODEL), lambda i, j: (0, 0)),
            pl.BlockSpec((1, D_MODEL), lambda i, j: (0, 0)),
        ],
        out_specs=[pl.BlockSpec((tm, tn), lambda i, j: (i, j)),
                   pl.BlockSpec((tm, tn), lambda i, j: (i, j))],
        out_shape=[jax.ShapeDtypeStruct((m, BR_W), F32), jax.ShapeDtypeStruct((m, BR_W), F32)],
        scratch_shapes=[pltpu.VMEM((tm, D_MODEL), BF16)],
        compiler_params=_params(("arbitrary", "arbitrary"), 48),
        name="mem_kv",
    )(mem, w_kv, w_kv, ln_g, ln_b)


def _inproj_kernel(x_ref, w_ref, gb_ref, lg_ref, lb_ref, *rest, n_side):
    side_in, rest = rest[:n_side], rest[n_side:]
    (o_ref, rx_ref), rest = rest[:2], rest[2:]
    side_out, (xb_ref,) = rest[:n_side], rest[n_side:]
    _side_cast(side_in, side_out)
    j = pl.program_id(1)

    @pl.when(j == 0)
    def _():
        xb_ref[...] = x_ref[...].astype(BF16)

    def emit(dst_ref, act):
        dst_ref[...] = act(_dot(xb_ref[...], w_ref[...])).astype(dst_ref.dtype)

    @pl.when((j == 0) | (j == 3))
    def _():
        emit(o_ref, jax.nn.gelu)

    @pl.when(j == 1)
    def _():
        emit(o_ref, lambda z: _layer_norm(jax.nn.gelu(z), lg_ref[...], lb_ref[...]))

    @pl.when(j == 2)
    def _():
        emit(rx_ref, lambda z: z)

    @pl.when(j == 4)
    def _():
        emit(o_ref, lambda z: z)

    @pl.when(j >= GATE_BLOCK0)
    def _():
        emit(o_ref, lambda z: _sigmoid(z + gb_ref[...]))


def _inproj(x, w_in, gate_b, gln_g, gln_b, *, tm, z_dtype, side=()):
    m = x.shape[0]
    grid = (m // tm, IN_BLOCKS)
    side_in_specs, side_out_specs, side_shapes = _side_cast_plan(side, grid)
    side_arrays = [e[0] if isinstance(e, tuple) else e for e in side]
    out_specs = [pl.BlockSpec((tm, BR_W), lambda i, j: (i, jnp.where(j < RX_BLOCK, j, j - 1))),
                 pl.BlockSpec((tm, BR_W), lambda i, j: (i, 0))]
    out_shape = [jax.ShapeDtypeStruct((m, Z_BLOCKS * BR_W), z_dtype),
                 jax.ShapeDtypeStruct((m, BR_W), F32)]
    return pl.pallas_call(
        functools.partial(_inproj_kernel, n_side=len(side)),
        grid=grid,
        in_specs=[
            pl.BlockSpec((tm, D_MODEL), lambda i, j: (i, 0)),
            pl.BlockSpec((D_MODEL, BR_W), lambda i, j: (0, j)),
            pl.BlockSpec((1, BR_W), lambda i, j: (0, jnp.maximum(j - GATE_BLOCK0, 0))),
            pl.BlockSpec((1, BR_W), lambda i, j: (0, 0)),
            pl.BlockSpec((1, BR_W), lambda i, j: (0, 0)),
        ] + side_in_specs,
        out_specs=out_specs + side_out_specs,
        out_shape=out_shape + side_shapes,
        scratch_shapes=[pltpu.VMEM((tm, D_MODEL), BF16)],
        compiler_params=_params(("arbitrary", "arbitrary"), 56),
        name="in_proj",
    )(x, w_in, gate_b, gln_g, gln_b, *side_arrays)


def _softplus(x):
    return jnp.maximum(x, 0.0) + jnp.log1p(jnp.exp(-jnp.abs(x)))


def _lru_coeffs(xc, wa_ref, ba_ref, wx_ref, bx_ref, lam_ref):
    xcb = xc.astype(BF16)
    r_parts, i_parts = [], []
    for k in range(LRU_BLOCKS):
        xk = xcb[:, k * LRU_BLOCK:(k + 1) * LRU_BLOCK]
        r_parts.append(_dot(xk, wa_ref[k]))
        i_parts.append(_dot(xk, wx_ref[k]))
    r = _sigmoid(jnp.concatenate(r_parts, axis=1) + ba_ref[...])
    i = _sigmoid(jnp.concatenate(i_parts, axis=1) + bx_ref[...])
    log_a = (-LRU_C) * r * _softplus(-lam_ref[...])
    a = jnp.exp(log_a)
    m = -jnp.tanh(log_a) * (a * a + 1.0)
    root = jnp.where(m > 0.0, m * lax.rsqrt(m), 0.0)
    return a, root * (i * xc)


def _mixer_kernel(gu_ref, v_ref, rx_ref, grg_ref, q_ref, k_ref, vm_ref,
                  ws_ref, bs_ref, cw_ref, cb_ref, wa_ref, ba_ref, wx_ref, bx_ref, lam_ref,
                  *rest, tm, n_side):
    side_in, rest = rest[:n_side], rest[n_side:]
    (y_ref, conv_ref, hlast_ref), rest = rest[:3], rest[3:]
    side_out, (xpad_ref, a_ref, b_ref, h_ref, hc_ref) = rest[:n_side], rest[n_side:]
    _side_cast(side_in, side_out)
    t = pl.program_id(1)

    tri = (lax.broadcasted_iota(jnp.int32, (CHUNK, CHUNK), 0)
           >= lax.broadcasted_iota(jnp.int32, (CHUNK, CHUNK), 1))
    for g in range(GMLP_GROUPS):
        wg = jnp.where(tri, ws_ref[g], 0.0).astype(BF16)
        cols = slice(g * GROUP_W, (g + 1) * GROUP_W)
        for c in range(tm // CHUNK):
            rows = slice(c * CHUNK, (c + 1) * CHUNK)
            s = _dot(wg, v_ref[rows, cols].astype(BF16)) + bs_ref[:, cols]
            y_ref[0, rows, cols] = (gu_ref[rows, cols] * s).astype(BF16)

    @pl.when(t == 0)
    def _():
        xpad_ref[0:SUBLANES, :] = jnp.zeros((SUBLANES, BR_W), F32)
        hc_ref[...] = jnp.zeros((1, BR_W), F32)

    @pl.when(t > 0)
    def _():
        xpad_ref[0:SUBLANES, :] = xpad_ref[tm:tm + SUBLANES, :]

    xpad_ref[SUBLANES:SUBLANES + tm, :] = rx_ref[...]
    xfull = xpad_ref[...]
    acc = xfull * cw_ref[0:1, :]
    for k in range(1, CONV_W):
        acc = pltpu.roll(acc, 1, 0) + xfull * cw_ref[k:k + 1, :]
    xc = acc[SUBLANES:, :] + cb_ref[...]

    a, b = _lru_coeffs(xc, wa_ref, ba_ref, wx_ref, bx_ref, lam_ref)
    a_ref[...] = a
    b_ref[...] = b

    row = lax.broadcasted_iota(jnp.int32, (SUBLANES, BR_W), 0)
    keep = [row >= d for d in (1, 2, 4)]

    def scan_block(blk, h):
        base = pl.multiple_of(blk * SUBLANES, SUBLANES)
        ca = a_ref[pl.ds(base, SUBLANES), :]
        cb = b_ref[pl.ds(base, SUBLANES), :]
        for d, kp in zip((1, 2, 4), keep):
            a_sh = jnp.where(kp, pltpu.roll(ca, d, 0), 1.0)
            b_sh = jnp.where(kp, pltpu.roll(cb, d, 0), 0.0)
            cb = ca * b_sh + cb
            ca = ca * a_sh
        hh = cb + ca * h
        h_ref[pl.ds(base, SUBLANES), :] = hh
        return hh[SUBLANES - 1:SUBLANES, :]

    h_end = lax.fori_loop(0, tm // SUBLANES, scan_block, hc_ref[...])
    hc_ref[...] = h_end
    y_ref[1] = (grg_ref[...] * h_ref[...]).astype(BF16)

    @pl.when(t == pl.num_programs(1) - 1)
    def _():
        conv_ref[0] = rx_ref[tm - (CONV_W - 1):tm, :]
        hlast_ref[0] = h_end

    scale = XA_HEAD_DIM ** -0.5
    for hd in range(XA_HEADS):
        cols = slice(hd * XA_HEAD_DIM, (hd + 1) * XA_HEAD_DIM)
        qh = q_ref[:, cols].astype(BF16)
        kh = k_ref[0, :, cols].astype(BF16)
        vh = vm_ref[0, :, cols].astype(BF16)
        s = lax.dot_general(qh, kh, (((1,), (1,)), ((), ())), preferred_element_type=F32) * scale
        e = jnp.exp(s - jnp.max(s, axis=-1, keepdims=True))
        p = e * (1.0 / jnp.sum(e, axis=-1, keepdims=True))
        y_ref[2, :, cols] = _dot(p.astype(BF16), vh).astype(BF16)


def _mixer_prompt(z, rx, k_mem, v_mem, ws, bs_mat, conv_w, conv_b, wa, ba, wx, bx, lam, *, bsz, seq, side=(),
                  tm=512):
    nt = seq // tm
    zspec = lambda c: pl.BlockSpec((tm, BR_W), lambda b, t, c=c: (b * nt + t, c))
    full = lambda shape: pl.BlockSpec(shape, lambda b, t: (0,) * len(shape))
    side_in_specs, side_out_specs, side_shapes = _side_cast_plan(side, (bsz, nt))
    side_arrays = [e[0] if isinstance(e, tuple) else e for e in side]
    return pl.pallas_call(
        functools.partial(_mixer_kernel, tm=tm, n_side=len(side)),
        grid=(bsz, nt),
        in_specs=[zspec(Z_GU), zspec(Z_V), zspec(0), zspec(Z_GRG), zspec(Z_Q),
                  pl.BlockSpec((1, N_MEM, BR_W), lambda b, t: (b, 0, 0)),
                  pl.BlockSpec((1, N_MEM, BR_W), lambda b, t: (b, 0, 0)),
                  full((GMLP_GROUPS, CHUNK, CHUNK)), full((CHUNK, BR_W)),
                  full((CONV_W, BR_W)), full((1, BR_W)),
                  full((LRU_BLOCKS, LRU_BLOCK, LRU_BLOCK)), full((1, BR_W)),
                  full((LRU_BLOCKS, LRU_BLOCK, LRU_BLOCK)), full((1, BR_W)),
                  full((1, BR_W))] + side_in_specs,
        out_specs=[pl.BlockSpec((3, tm, BR_W), lambda b, t: (0, b * nt + t, 0)),
                   pl.BlockSpec((1, CONV_W - 1, BR_W), lambda b, t: (b, 0, 0)),
                   pl.BlockSpec((1, 1, BR_W), lambda b, t: (b, 0, 0))] + side_out_specs,
        out_shape=[jax.ShapeDtypeStruct((3, bsz * seq, BR_W), BF16),
                   jax.ShapeDtypeStruct((bsz, CONV_W - 1, BR_W), F32),
                   jax.ShapeDtypeStruct((bsz, 1, BR_W), F32)] + side_shapes,
        scratch_shapes=[pltpu.VMEM((tm + SUBLANES, BR_W), F32),
                        pltpu.VMEM((tm, BR_W), F32), pltpu.VMEM((tm, BR_W), F32),
                        pltpu.VMEM((tm, BR_W), F32), pltpu.VMEM((1, BR_W), F32)],
        compiler_params=_params(("arbitrary", "arbitrary"), 48),
        name="mixer_prompt",
    )(z, z, rx, z, z, k_mem, v_mem, ws, bs_mat, conv_w, conv_b, wa, ba, wx, bx, lam, *side_arrays)


def _mixer_sample_kernel(gu_ref, v_ref, rx_ref, grg_ref, cs_ref, h0_ref,
                         wv_ref, bv_ref, cw_ref, cb_ref, wa_ref, ba_ref, wx_ref, bx_ref, lam_ref,
                         y_ref, conv_ref, h_ref):
    y_ref[0] = (gu_ref[...] * (wv_ref[...] * v_ref[...] + bv_ref[...])).astype(BF16)

    rx = rx_ref[...]
    xc = cb_ref[...] + rx * cw_ref[CONV_W - 1:CONV_W, :]
    for k in range(CONV_W - 1):
        xc = xc + cs_ref[k] * cw_ref[k:k + 1, :]
    for k in range(CONV_W - 2):
        conv_ref[k] = cs_ref[k + 1]
    conv_ref[CONV_W - 2] = rx

    a, b = _lru_coeffs(xc, wa_ref, ba_ref, wx_ref, bx_ref, lam_ref)
    h = a * h0_ref[...] + b
    h_ref[...] = h
    y_ref[1] = (grg_ref[...] * h).astype(BF16)


def _mixer_sample(z, rx, conv_state, h0, wvec, bvec, conv_w, conv_b, wa, ba, wx, bx, lam):
    n = z.shape[0]
    zspec = lambda c: pl.BlockSpec((n, BR_W), lambda i, c=c: (0, c))
    full = lambda shape: pl.BlockSpec(shape, lambda i: (0,) * len(shape))
    return pl.pallas_call(
        _mixer_sample_kernel,
        grid=(1,),
        in_specs=[zspec(Z_GU), zspec(Z_V), zspec(0), zspec(Z_GRG),
                  full((CONV_W - 1, n, BR_W)), full((n, BR_W)),
                  full((1, BR_W)), full((1, BR_W)), full((CONV_W, BR_W)), full((1, BR_W)),
                  full((LRU_BLOCKS, LRU_BLOCK, LRU_BLOCK)), full((1, BR_W)),
                  full((LRU_BLOCKS, LRU_BLOCK, LRU_BLOCK)), full((1, BR_W)),
                  full((1, BR_W))],
        out_specs=[full((2, n, BR_W)), full((CONV_W - 1, n, BR_W)), full((n, BR_W))],
        out_shape=[jax.ShapeDtypeStruct((2, n, BR_W), BF16),
                   jax.ShapeDtypeStruct((CONV_W - 1, n, BR_W), F32),
                   jax.ShapeDtypeStruct((n, BR_W), F32)],
        compiler_params=_params(("arbitrary",), 32),
        name="mixer_sample",
    )(z, z, rx, z, conv_state, h0, wvec, bvec, conv_w, conv_b, wa, ba, wx, bx, lam)


HEAD_SPLIT = XA_HEAD_DIM // 128
HEAD_ROWS = HEAD_SPLIT * XA_HEADS


def _head_rows(x):
    lead = x.shape[:-2]
    x = x.reshape(*lead, XA_HEADS, HEAD_SPLIT, 128)
    return jnp.swapaxes(x, -2, -3).reshape(*lead, HEAD_ROWS, 128)


def _xattn_sample_kernel(q_ref, k_ref, v_ref, o_ref, *, tb):
    scale = XA_HEAD_DIM ** -0.5
    n_rows = N_MEM * HEAD_ROWS
    lane = lax.broadcasted_iota(jnp.int32, (HEAD_ROWS, n_rows), 1)
    sub = lax.broadcasted_iota(jnp.int32, (HEAD_ROWS, n_rows), 0)
    own = (lane % HEAD_ROWS) == sub
    low_piece = (lax.broadcasted_iota(jnp.int32, (1, n_rows), 1) % HEAD_ROWS) < XA_HEADS

    def lane_group_reduce(x, op):
        shift = HEAD_ROWS
        while shift < 128:
            x = op(x, pltpu.roll(x, shift, 1))
            shift *= 2
        return x

    def per_kind(x, op):
        acc = x[:, 0:128]
        for c in range(1, n_rows // 128):
            acc = op(acc, x[:, c * 128:(c + 1) * 128])
        acc = lane_group_reduce(acc, op)
        return jnp.concatenate([acc] * (n_rows // 128), axis=1)

    parts = []
    for b in range(tb):
        qb = q_ref[b].astype(BF16)
        kb = k_ref[b].reshape(n_rows, 128).astype(BF16)
        c = lax.dot_general(qb, kb, (((1,), (1,)), ((), ())), preferred_element_type=F32)
        parts.append(jnp.sum(jnp.where(own, c, 0.0), axis=0, keepdims=True))
    part = jnp.concatenate(parts, axis=0)
    other = jnp.where(low_piece, pltpu.roll(part, n_rows - XA_HEADS, 1), pltpu.roll(part, XA_HEADS, 1))
    s = (part + other) * scale
    e = jnp.exp(s - per_kind(s, jnp.maximum))
    p = e * (1.0 / per_kind(e, jnp.add))
    for b in range(tb):
        vb = v_ref[b].reshape(n_rows, 128).astype(BF16)
        pm = jnp.where(own, jnp.broadcast_to(p[b:b + 1, :], (HEAD_ROWS, n_rows)), 0.0).astype(BF16)
        o_ref[b] = _dot(pm, vb).astype(BF16)


def _xattn_sample(q, k_cache, v_cache, *, tb=8):
    n = q.shape[0]
    kv_spec = pl.BlockSpec((tb, N_MEM, HEAD_ROWS, 128), lambda i: (i, 0, 0, 0))
    q_spec = pl.BlockSpec((tb, HEAD_ROWS, 128), lambda i: (i, 0, 0))
    return pl.pallas_call(
        functools.partial(_xattn_sample_kernel, tb=tb),
        grid=(n // tb,),
        in_specs=[q_spec, kv_spec, kv_spec],
        out_specs=q_spec,
        out_shape=jax.ShapeDtypeStruct((n, HEAD_ROWS, 128), BF16),
        compiler_params=_params(("arbitrary",), 48),
        name="xattn_sample",
    )(q, k_cache, v_cache)


N_BRANCH = 3


def _merge_kernel(y_ref, wb_ref, g0_ref, g1_ref, g2_ref, o_ref):
    acc = None
    for k, gate_ref in enumerate((g0_ref, g1_ref, g2_ref)):
        term = gate_ref[...].astype(F32) * _dot(y_ref[k], wb_ref[k])
        acc = term if acc is None else acc + term
    o_ref[...] = acc.astype(BF16)


def _merge(y, w_branch, z, *, tm):
    m = y.shape[1]
    gate_spec = lambda k: pl.BlockSpec((tm, BR_W), lambda i, n, k=k: (i, Z_GATE0 + 2 * k + n))
    return pl.pallas_call(
        _merge_kernel,
        grid=(m // tm, D_MODEL // BR_W),
        in_specs=[
            pl.BlockSpec((N_BRANCH, tm, BR_W), lambda i, n: (0, i, 0)),
            pl.BlockSpec((N_BRANCH, BR_W, BR_W), lambda i, n: (0, 0, n)),
            gate_spec(0), gate_spec(1), gate_spec(2),
        ],
        out_specs=pl.BlockSpec((tm, BR_W), lambda i, n: (i, n)),
        out_shape=jax.ShapeDtypeStruct((m, D_MODEL), BF16),
        compiler_params=_params(("arbitrary", "arbitrary"), 56),
        name="merge",
    )(y, w_branch, z, z, z)


def _outproj_kernel(m_ref, wo_ref, x_ref, g_ref, b_ref, o_ref, *, row_splits):
    rows_per = m_ref.shape[0] // row_splits
    for r in range(row_splits):
        rows = slice(r * rows_per, (r + 1) * rows_per)
        y = ALPHA * x_ref[rows, :] + _dot(m_ref[rows, :], wo_ref[...])
        o_ref[rows, :] = _layer_norm(y, g_ref[...], b_ref[...])


def _outproj(merged, w_out, x, ln_g, ln_b, *, tm):
    m = x.shape[0]
    return pl.pallas_call(
        functools.partial(_outproj_kernel, row_splits=max(1, tm // OUTPROJ_ROW_GROUP)),
        grid=(m // tm,),
        in_specs=[
            pl.BlockSpec((tm, D_MODEL), lambda i: (i, 0)),
            pl.BlockSpec((D_MODEL, D_MODEL), lambda i: (0, 0), pipeline_mode=pl.Buffered(1)),
            pl.BlockSpec((tm, D_MODEL), lambda i: (i, 0)),
            pl.BlockSpec((1, D_MODEL), lambda i: (0, 0)),
            pl.BlockSpec((1, D_MODEL), lambda i: (0, 0)),
        ],
        out_specs=pl.BlockSpec((tm, D_MODEL), lambda i: (i, 0)),
        out_shape=jax.ShapeDtypeStruct((m, D_MODEL), F32),
        compiler_params=_params(("arbitrary",), 60),
        name="out_proj",
    )(merged, w_out, x, ln_g, ln_b)


def kernel(x_prompt, x_sample, mem_prompt, cache_mem_k, cache_mem_v, state_conv, state_lru_h, ffn1_w_gu, ffn1_w_down, ln1_g, ln1_b, w_in, gate_b, gmlp_ln_g, gmlp_ln_b, gmlp_w_s, gmlp_b_s, conv_w, conv_b, lru_w_a, lru_b_a, lru_w_x, lru_b_x, lru_lambda, mem_ln_g, mem_ln_b, w_mem_kv, w_branch, w_out, ln2_g, ln2_b, ffn2_w_gu, ffn2_w_down, ln3_g, ln3_b):
    bsz, seq, _ = x_prompt.shape
    n_s = x_sample.shape[0]
    l = 0

    wa, wx = lru_w_a[l].astype(BF16), lru_w_x[l].astype(BF16)

    row = lambda p: p[l].reshape(1, -1)
    gb = gate_b[l].reshape(1, -1)
    bs_mat = jnp.repeat(gmlp_b_s[l].T, GROUP_W, axis=1)
    wvec = jnp.repeat(gmlp_w_s[l][:, 0, 0], GROUP_W).reshape(1, -1)
    bvec = jnp.repeat(gmlp_b_s[l][:, 0], GROUP_W).reshape(1, -1)
    lru = (wa, row(lru_b_a), wx, row(lru_b_x), row(lru_lambda))

    xp = x_prompt.reshape(bsz * seq, D_MODEL)
    xs = x_sample.reshape(n_s, D_MODEL)

    x1s, w1g, w1u, w1d = _ffn(xs, ffn1_w_gu[l], ffn1_w_gu[l], ffn1_w_down[l], row(ln1_g), row(ln1_b),
                              tm=n_s, emit_bf16=True)
    k_p, v_p = _memkv(mem_prompt.reshape(bsz * N_MEM, D_MODEL), w_mem_kv[l], row(mem_ln_g), row(mem_ln_b))
    x1p, win, w2d = _ffn(xp, w1g, w1u, w1d, row(ln1_g), row(ln1_b), tm=1024, side=(w_in[l], ffn2_w_down[l]))

    zs, rxs = _inproj(x1s, win, gb, row(gmlp_ln_g), row(gmlp_ln_b), tm=n_s, z_dtype=F32)
    y01, conv_s, h_s = _mixer_sample(
        zs, rxs, jnp.swapaxes(state_conv[l], 0, 1), state_lru_h[l], wvec, bvec, conv_w[l], row(conv_b), *lru)
    q_s = zs[:, Z_Q * BR_W:(Z_Q + 1) * BR_W].reshape(n_s, XA_HEADS, XA_HEAD_DIM)
    yxa = _xattn_sample(_head_rows(q_s), _head_rows(cache_mem_k[l]), _head_rows(cache_mem_v[l]))
    yxa = jnp.swapaxes(yxa.reshape(n_s, HEAD_SPLIT, XA_HEADS, 128), 1, 2)
    ys = jnp.concatenate([y01, yxa.reshape(1, n_s, BR_W)], axis=0)

    zp, rxp, w2gu = _inproj(x1p, win, gb, row(gmlp_ln_g), row(gmlp_ln_b), tm=1024, z_dtype=BF16,
                            side=((ffn2_w_gu[l], FFN_COL_BLOCK),))
    yp, conv_p, hlast_p, wbr, wout = _mixer_prompt(
        zp, rxp, k_p.reshape(bsz, N_MEM, BR_W), v_p.reshape(bsz, N_MEM, BR_W), gmlp_w_s[l], bs_mat,
        conv_w[l], row(conv_b), *lru, bsz=bsz, seq=seq,
        side=(w_branch[l].reshape(N_BRANCH * BR_W, D_MODEL), w_out[l]))
    wbr = wbr.reshape(N_BRANCH, BR_W, D_MODEL)

    ms = _merge(ys, wbr, zs, tm=n_s)
    x2s = _outproj(ms, wout, x1s, row(ln2_g), row(ln2_b), tm=n_s)
    x3s = _ffn(x2s, w2gu, w2gu, w2d, row(ln3_g), row(ln3_b), tm=n_s)
    mp = _merge(yp, wbr, zp, tm=1024)
    x2p = _outproj(mp, wout, x1p, row(ln2_g), row(ln2_b), tm=1024)
    x3p = _ffn(x2p, w2gu, w2gu, w2d, row(ln3_g), row(ln3_b), tm=1024)

    kv_shape = (1, bsz, N_MEM, XA_HEADS, XA_HEAD_DIM)
    return (x3p.reshape(bsz, seq, D_MODEL),
            x3s.reshape(n_s, 1, D_MODEL),
            k_p.reshape(kv_shape),
            v_p.reshape(kv_shape),
            conv_p[None],
            hlast_p.reshape(1, bsz, BR_W),
            jnp.swapaxes(conv_s, 0, 1)[None],
            h_s[None],
            zs[:, Z_V * BR_W:(Z_V + 1) * BR_W].reshape(1, n_s, 1, BR_W))
```

```python
import functools

import jax
import jax.numpy as jnp
from jax import lax
from jax.experimental import pallas as pl
from jax.experimental.pallas import tpu as pltpu

F32 = jnp.float32
BF16 = jnp.bfloat16

D_MODEL = 2048
BR_W = 1024
D_FF = 5632
N_MEM = 256
XA_HEADS = 4
XA_HEAD_DIM = 256
GMLP_GROUPS = 4
GROUP_W = BR_W // GMLP_GROUPS
CHUNK = 128
LRU_BLOCKS = 8
LRU_BLOCK = 128
CONV_W = 4
LRU_C = 8.0
LN_EPS = 1e-5
ALPHA = 2.0 ** 0.25
IN_BLOCKS = 11
RX_BLOCK = 2
GATE_BLOCK0 = 5
Z_BLOCKS = IN_BLOCKS - 1
Z_GU, Z_V, Z_GRG, Z_Q, Z_GATE0 = 0, 1, 2, 3, 4
SUBLANES = 8
FFN_ROW_GROUP = 512
OUTPROJ_ROW_GROUP = 128
MiB = 1024 * 1024


VMEM_LIMIT_MIB = {
    "ffn_ln": 62,
    "mem_kv": 48,
    "in_proj": 56,
    "mixer_prompt": 48,
    "mixer_sample": 32,
    "xattn_sample": 48,
    "merge": 56,
    "out_proj": 48,
}


def _call_params(name, n_grid_axes):
    return dict(name=name, compiler_params=pltpu.CompilerParams(
        dimension_semantics=("arbitrary",) * n_grid_axes, vmem_limit_bytes=VMEM_LIMIT_MIB[name] * MiB))


def _layer_norm(x, g, b):
    mu = jnp.mean(x, axis=-1, keepdims=True)
    xc = x - mu
    var = jnp.mean(xc * xc, axis=-1, keepdims=True)
    return xc * lax.rsqrt(var + LN_EPS) * g + b


def _dot(a, b):
    return jnp.dot(a, b, preferred_element_type=F32)


def _sigmoid(x):
    return 0.5 * jnp.tanh(0.5 * x) + 0.5


def _mxu_operand(w_ref, w16_ref=None):
    w = w_ref[...].astype(BF16)
    if w16_ref is not None:
        w16_ref[...] = w
    return w


def _side_cast_plan(side, grid):
    n_steps = 1
    for g in grid:
        n_steps *= g

    def linear(*idx):
        lin = idx[0]
        for g, i in zip(grid[1:], idx[1:]):
            lin = lin * g + i
        return lin

    specs, shapes = [], []
    for w in side:
        r, c = w.shape
        if c % (n_steps * 128) == 0:
            spec = pl.BlockSpec((r, c // n_steps), lambda *idx: (0, linear(*idx)))
        else:
            assert r % (n_steps * 16) == 0, (w.shape, n_steps)
            spec = pl.BlockSpec((r // n_steps, c), lambda *idx: (linear(*idx), 0))
        specs.append(spec)
        shapes.append(jax.ShapeDtypeStruct(w.shape, BF16))
    return specs, shapes


def _side_cast(in_refs, out_refs):
    for src, dst in zip(in_refs, out_refs):
        dst[...] = src[...].astype(BF16)


def _ffn_kernel(x_ref, wg_ref, wu_ref, wd_ref, g_ref, b_ref, *rest, emit_bf16, row_splits, n_side):
    side_in, rest = rest[:n_side], rest[n_side:]
    o_ref, rest = rest[0], rest[1:]
    if emit_bf16:
        (wg16_ref, wu16_ref, wd16_ref), rest = rest[:3], rest[3:]
    else:
        wg16_ref = wu16_ref = wd16_ref = None
    side_out, (xb_ref,) = rest[:n_side], rest[n_side:]
    _side_cast(side_in, side_out)
    j = pl.program_id(1)
    last = pl.num_programs(1) - 1
    rows_per = xb_ref.shape[0] // row_splits

    def step(first, final):
        wg = _mxu_operand(wg_ref, wg16_ref)
        wu = _mxu_operand(wu_ref, wu16_ref)
        wd = _mxu_operand(wd_ref, wd16_ref)
        for r in range(row_splits):
            rows = slice(r * rows_per, (r + 1) * rows_per)
            if first:
                xb = x_ref[rows, :].astype(BF16)
                xb_ref[rows, :] = xb
            else:
                xb = xb_ref[rows, :]
            g = _dot(xb, wg)
            u = _dot(xb, wu)
            h = (g * _sigmoid(g) * u).astype(BF16)
            acc = _dot(h, wd)
            if not first:
                acc = o_ref[rows, :] + acc
            if final:
                acc = _layer_norm(ALPHA * x_ref[rows, :] + 0.5 * acc, g_ref[...], b_ref[...])
            o_ref[rows, :] = acc

    pl.when(j == 0)(functools.partial(step, True, False))
    pl.when((j > 0) & (j < last))(functools.partial(step, False, False))
    pl.when(j == last)(functools.partial(step, False, True))


def _ffn(x, w_g, w_u, w_down, ln_g, ln_b, *, tm, emit_bf16=False, side=(), tf=512):
    m = x.shape[0]
    nf = D_FF // tf
    grid = (m // tm, nf)
    u_off = nf if w_u.shape[1] == 2 * D_FF else 0
    assert not emit_bf16 or m == tm
    side_specs, side_shapes = _side_cast_plan(side, grid)
    out_specs = [pl.BlockSpec((tm, D_MODEL), lambda i, j: (i, 0))]
    out_shape = [jax.ShapeDtypeStruct((m, D_MODEL), F32)]
    if emit_bf16:
        out_specs += [pl.BlockSpec((D_MODEL, tf), lambda i, j: (0, j)),
                      pl.BlockSpec((D_MODEL, tf), lambda i, j: (0, j)),
                      pl.BlockSpec((tf, D_MODEL), lambda i, j: (j, 0))]
        out_shape += [jax.ShapeDtypeStruct((D_MODEL, D_FF), BF16),
                      jax.ShapeDtypeStruct((D_MODEL, D_FF), BF16),
                      jax.ShapeDtypeStruct((D_FF, D_MODEL), BF16)]
    outs = pl.pallas_call(
        functools.partial(_ffn_kernel, emit_bf16=emit_bf16, row_splits=max(1, tm // FFN_ROW_GROUP),
                          n_side=len(side)),
        grid=grid,
        in_specs=[
            pl.BlockSpec((tm, D_MODEL), lambda i, j: (i, 0)),
            pl.BlockSpec((D_MODEL, tf), lambda i, j: (0, j)),
            pl.BlockSpec((D_MODEL, tf), lambda i, j: (0, j + u_off)),
            pl.BlockSpec((tf, D_MODEL), lambda i, j: (j, 0)),
            pl.BlockSpec((1, D_MODEL), lambda i, j: (0, 0)),
            pl.BlockSpec((1, D_MODEL), lambda i, j: (0, 0)),
        ] + side_specs,
        out_specs=out_specs + side_specs,
        out_shape=out_shape + side_shapes,
        scratch_shapes=[pltpu.VMEM((tm, D_MODEL), BF16)],
        **_call_params("ffn_ln", 2),
    )(x, w_g, w_u, w_down, ln_g, ln_b, *side)
    return outs if (emit_bf16 or side) else outs[0]


def _memkv_kernel(x_ref, wk_ref, wv_ref, g_ref, b_ref, k_ref, v_ref, xb_ref):
    @pl.when(pl.program_id(1) == 0)
    def _():
        xb_ref[...] = _layer_norm(x_ref[...], g_ref[...], b_ref[...]).astype(BF16)

    xb = xb_ref[...]
    k_ref[...] = _dot(xb, _mxu_operand(wk_ref))
    v_ref[...] = _dot(xb, _mxu_operand(wv_ref))


def _memkv(mem, w_kv, ln_g, ln_b, *, tm=1024, tn=512):
    m = mem.shape[0]
    nn = BR_W // tn
    return pl.pallas_call(
        _memkv_kernel,
        grid=(m // tm, nn),
        in_specs=[
            pl.BlockSpec((tm, D_MODEL), lambda i, j: (i, 0)),
            pl.BlockSpec((D_MODEL, tn), lambda i, j: (0, j)),
            pl.BlockSpec((D_MODEL, tn), lambda i, j: (0, j + nn)),
            pl.BlockSpec((1, D_MODEL), lambda i, j: (0, 0)),
            pl.BlockSpec((1, D_MODEL), lambda i, j: (0, 0)),
        ],
        out_specs=[pl.BlockSpec((tm, tn), lambda i, j: (i, j)),
                   pl.BlockSpec((tm, tn), lambda i, j: (i, j))],
        out_shape=[jax.ShapeDtypeStruct((m, BR_W), F32), jax.ShapeDtypeStruct((m, BR_W), F32)],
        scratch_shapes=[pltpu.VMEM((tm, D_MODEL), BF16)],
        **_call_params("mem_kv", 2),
    )(mem, w_kv, w_kv, ln_g, ln_b)


def _inproj_kernel(x_ref, w_ref, gb_ref, lg_ref, lb_ref, *rest, n_side):
    side_in, rest = rest[:n_side], rest[n_side:]
    (o_ref, rx_ref), rest = rest[:2], rest[2:]
    side_out, (xb_ref,) = rest[:n_side], rest[n_side:]
    _side_cast(side_in, side_out)
    j = pl.program_id(1)

    @pl.when(j == 0)
    def _():
        xb_ref[...] = x_ref[...].astype(BF16)

    def emit(dst_ref, act):
        dst_ref[...] = act(_dot(xb_ref[...], w_ref[...])).astype(dst_ref.dtype)

    @pl.when((j == 0) | (j == 3))
    def _():
        emit(o_ref, jax.nn.gelu)

    @pl.when(j == 1)
    def _():
        emit(o_ref, lambda z: _layer_norm(jax.nn.gelu(z), lg_ref[...], lb_ref[...]))

    @pl.when(j == 2)
    def _():
        emit(rx_ref, lambda z: z)

    @pl.when(j == 4)
    def _():
        emit(o_ref, lambda z: z)

    @pl.when(j >= GATE_BLOCK0)
    def _():
        emit(o_ref, lambda z: _sigmoid(z + gb_ref[...]))


def _inproj(x, w_in, gate_b, gln_g, gln_b, *, tm, z_dtype, side=()):
    m = x.shape[0]
    grid = (m // tm, IN_BLOCKS)
    side_specs, side_shapes = _side_cast_plan(side, grid)
    out_specs = [pl.BlockSpec((tm, BR_W), lambda i, j: (i, jnp.where(j < RX_BLOCK, j, j - 1))),
                 pl.BlockSpec((tm, BR_W), lambda i, j: (i, 0))]
    out_shape = [jax.ShapeDtypeStruct((m, Z_BLOCKS * BR_W), z_dtype),
                 jax.ShapeDtypeStruct((m, BR_W), F32)]
    return pl.pallas_call(
        functools.partial(_inproj_kernel, n_side=len(side)),
        grid=grid,
        in_specs=[
            pl.BlockSpec((tm, D_MODEL), lambda i, j: (i, 0)),
            pl.BlockSpec((D_MODEL, BR_W), lambda i, j: (0, j)),
            pl.BlockSpec((1, BR_W), lambda i, j: (0, jnp.maximum(j - GATE_BLOCK0, 0))),
            pl.BlockSpec((1, BR_W), lambda i, j: (0, 0)),
            pl.BlockSpec((1, BR_W), lambda i, j: (0, 0)),
        ] + side_specs,
        out_specs=out_specs + side_specs,
        out_shape=out_shape + side_shapes,
        scratch_shapes=[pltpu.VMEM((tm, D_MODEL), BF16)],
        **_call_params("in_proj", 2),
    )(x, w_in, gate_b, gln_g, gln_b, *side)


def _softplus(x):
    return jnp.maximum(x, 0.0) + jnp.log1p(jnp.exp(-jnp.abs(x)))


def _lru_coeffs(xc, wa_ref, ba_ref, wx_ref, bx_ref, lam_ref):
    xcb = xc.astype(BF16)
    r_parts, i_parts = [], []
    for k in range(LRU_BLOCKS):
        xk = xcb[:, k * LRU_BLOCK:(k + 1) * LRU_BLOCK]
        r_parts.append(_dot(xk, wa_ref[k]))
        i_parts.append(_dot(xk, wx_ref[k]))
    r = _sigmoid(jnp.concatenate(r_parts, axis=1) + ba_ref[...])
    i = _sigmoid(jnp.concatenate(i_parts, axis=1) + bx_ref[...])
    log_a = (-LRU_C) * r * _softplus(-lam_ref[...])
    a = jnp.exp(log_a)
    m = -jnp.tanh(log_a) * (a * a + 1.0)
    root = jnp.where(m > 0.0, m * lax.rsqrt(m), 0.0)
    return a, root * (i * xc)


def _mixer_kernel(gu_ref, v_ref, rx_ref, grg_ref, q_ref, k_ref, vm_ref,
                  ws_ref, bs_ref, cw_ref, cb_ref, wa_ref, ba_ref, wx_ref, bx_ref, lam_ref,
                  *rest, tm, n_side):
    side_in, rest = rest[:n_side], rest[n_side:]
    (y_ref, conv_ref, hlast_ref), rest = rest[:3], rest[3:]
    side_out, (xpad_ref, a_ref, b_ref, h_ref, hc_ref) = rest[:n_side], rest[n_side:]
    _side_cast(side_in, side_out)
    t = pl.program_id(1)

    tri = (lax.broadcasted_iota(jnp.int32, (CHUNK, CHUNK), 0)
           >= lax.broadcasted_iota(jnp.int32, (CHUNK, CHUNK), 1))
    for g in range(GMLP_GROUPS):
        wg = jnp.where(tri, ws_ref[g], 0.0).astype(BF16)
        cols = slice(g * GROUP_W, (g + 1) * GROUP_W)
        for c in range(tm // CHUNK):
            rows = slice(c * CHUNK, (c + 1) * CHUNK)
            s = _dot(wg, v_ref[rows, cols].astype(BF16)) + bs_ref[:, cols]
            y_ref[0, rows, cols] = (gu_ref[rows, cols] * s).astype(BF16)

    @pl.when(t == 0)
    def _():
        xpad_ref[0:SUBLANES, :] = jnp.zeros((SUBLANES, BR_W), F32)
        hc_ref[...] = jnp.zeros((1, BR_W), F32)

    @pl.when(t > 0)
    def _():
        xpad_ref[0:SUBLANES, :] = xpad_ref[tm:tm + SUBLANES, :]

    xpad_ref[SUBLANES:SUBLANES + tm, :] = rx_ref[...]
    xfull = xpad_ref[...]
    acc = xfull * cw_ref[0:1, :]
    for k in range(1, CONV_W):
        acc = pltpu.roll(acc, 1, 0) + xfull * cw_ref[k:k + 1, :]
    xc = acc[SUBLANES:, :] + cb_ref[...]

    a, b = _lru_coeffs(xc, wa_ref, ba_ref, wx_ref, bx_ref, lam_ref)
    a_ref[...] = a
    b_ref[...] = b

    row = lax.broadcasted_iota(jnp.int32, (SUBLANES, BR_W), 0)
    keep = [row >= d for d in (1, 2, 4)]

    def scan_block(blk, h):
        base = pl.multiple_of(blk * SUBLANES, SUBLANES)
        ca = a_ref[pl.ds(base, SUBLANES), :]
        cb = b_ref[pl.ds(base, SUBLANES), :]
        for d, kp in zip((1, 2, 4), keep):
            a_sh = jnp.where(kp, pltpu.roll(ca, d, 0), 1.0)
            b_sh = jnp.where(kp, pltpu.roll(cb, d, 0), 0.0)
            cb = ca * b_sh + cb
            ca = ca * a_sh
        hh = cb + ca * h
        h_ref[pl.ds(base, SUBLANES), :] = hh
        return hh[SUBLANES - 1:SUBLANES, :]

    h_end = lax.fori_loop(0, tm // SUBLANES, scan_block, hc_ref[...])
    hc_ref[...] = h_end
    y_ref[1] = (grg_ref[...] * h_ref[...]).astype(BF16)

    @pl.when(t == pl.num_programs(1) - 1)
    def _():
        conv_ref[0] = rx_ref[tm - (CONV_W - 1):tm, :]
        hlast_ref[0] = h_end

    scale = XA_HEAD_DIM ** -0.5
    for hd in range(XA_HEADS):
        cols = slice(hd * XA_HEAD_DIM, (hd + 1) * XA_HEAD_DIM)
        qh = q_ref[:, cols].astype(BF16)
        kh = k_ref[0, :, cols].astype(BF16)
        vh = vm_ref[0, :, cols].astype(BF16)
        s = lax.dot_general(qh, kh, (((1,), (1,)), ((), ())), preferred_element_type=F32) * scale
        e = jnp.exp(s - jnp.max(s, axis=-1, keepdims=True))
        p = e * (1.0 / jnp.sum(e, axis=-1, keepdims=True))
        y_ref[2, :, cols] = _dot(p.astype(BF16), vh).astype(BF16)


def _mixer_prompt(z, rx, k_mem, v_mem, ws, bs_mat, conv_w, conv_b, wa, ba, wx, bx, lam, *, bsz, seq, side=(),
                  tm=512):
    nt = seq // tm
    zspec = lambda c: pl.BlockSpec((tm, BR_W), lambda b, t, c=c: (b * nt + t, c))
    full = lambda shape: pl.BlockSpec(shape, lambda b, t: (0,) * len(shape))
    side_specs, side_shapes = _side_cast_plan(side, (bsz, nt))
    return pl.pallas_call(
        functools.partial(_mixer_kernel, tm=tm, n_side=len(side)),
        grid=(bsz, nt),
        in_specs=[zspec(Z_GU), zspec(Z_V), zspec(0), zspec(Z_GRG), zspec(Z_Q),
                  pl.BlockSpec((1, N_MEM, BR_W), lambda b, t: (b, 0, 0)),
                  pl.BlockSpec((1, N_MEM, BR_W), lambda b, t: (b, 0, 0)),
                  full((GMLP_GROUPS, CHUNK, CHUNK)), full((CHUNK, BR_W)),
                  full((CONV_W, BR_W)), full((1, BR_W)),
                  full((LRU_BLOCKS, LRU_BLOCK, LRU_BLOCK)), full((1, BR_W)),
                  full((LRU_BLOCKS, LRU_BLOCK, LRU_BLOCK)), full((1, BR_W)),
                  full((1, BR_W))] + side_specs,
        out_specs=[pl.BlockSpec((3, tm, BR_W), lambda b, t: (0, b * nt + t, 0)),
                   pl.BlockSpec((1, CONV_W - 1, BR_W), lambda b, t: (b, 0, 0)),
                   pl.BlockSpec((1, 1, BR_W), lambda b, t: (b, 0, 0))] + side_specs,
        out_shape=[jax.ShapeDtypeStruct((3, bsz * seq, BR_W), BF16),
                   jax.ShapeDtypeStruct((bsz, CONV_W - 1, BR_W), F32),
                   jax.ShapeDtypeStruct((bsz, 1, BR_W), F32)] + side_shapes,
        scratch_shapes=[pltpu.VMEM((tm + SUBLANES, BR_W), F32),
                        pltpu.VMEM((tm, BR_W), F32), pltpu.VMEM((tm, BR_W), F32),
                        pltpu.VMEM((tm, BR_W), F32), pltpu.VMEM((1, BR_W), F32)],
        **_call_params("mixer_prompt", 2),
    )(z, z, rx, z, z, k_mem, v_mem, ws, bs_mat, conv_w, conv_b, wa, ba, wx, bx, lam, *side)


def _mixer_sample_kernel(gu_ref, v_ref, rx_ref, grg_ref, cs_ref, h0_ref,
                         wv_ref, bv_ref, cw_ref, cb_ref, wa_ref, ba_ref, wx_ref, bx_ref, lam_ref,
                         y_ref, conv_ref, h_ref):
    y_ref[0] = (gu_ref[...] * (wv_ref[...] * v_ref[...] + bv_ref[...])).astype(BF16)

    rx = rx_ref[...]
    xc = cb_ref[...] + rx * cw_ref[CONV_W - 1:CONV_W, :]
    for k in range(CONV_W - 1):
        xc = xc + cs_ref[k] * cw_ref[k:k + 1, :]
    for k in range(CONV_W - 2):
        conv_ref[k] = cs_ref[k + 1]
    conv_ref[CONV_W - 2] = rx

    a, b = _lru_coeffs(xc, wa_ref, ba_ref, wx_ref, bx_ref, lam_ref)
    h = a * h0_ref[...] + b
    h_ref[...] = h
    y_ref[1] = (grg_ref[...] * h).astype(BF16)


def _mixer_sample(z, rx, conv_state, h0, wvec, bvec, conv_w, conv_b, wa, ba, wx, bx, lam):
    n = z.shape[0]
    zspec = lambda c: pl.BlockSpec((n, BR_W), lambda i, c=c: (0, c))
    full = lambda shape: pl.BlockSpec(shape, lambda i: (0,) * len(shape))
    return pl.pallas_call(
        _mixer_sample_kernel,
        grid=(1,),
        in_specs=[zspec(Z_GU), zspec(Z_V), zspec(0), zspec(Z_GRG),
                  full((CONV_W - 1, n, BR_W)), full((n, BR_W)),
                  full((1, BR_W)), full((1, BR_W)), full((CONV_W, BR_W)), full((1, BR_W)),
                  full((LRU_BLOCKS, LRU_BLOCK, LRU_BLOCK)), full((1, BR_W)),
                  full((LRU_BLOCKS, LRU_BLOCK, LRU_BLOCK)), full((1, BR_W)),
                  full((1, BR_W))],
        out_specs=[full((2, n, BR_W)), full((CONV_W - 1, n, BR_W)), full((n, BR_W))],
        out_shape=[jax.ShapeDtypeStruct((2, n, BR_W), BF16),
                   jax.ShapeDtypeStruct((CONV_W - 1, n, BR_W), F32),
                   jax.ShapeDtypeStruct((n, BR_W), F32)],
        **_call_params("mixer_sample", 1),
    )(z, z, rx, z, conv_state, h0, wvec, bvec, conv_w, conv_b, wa, ba, wx, bx, lam)


HEAD_SPLIT = XA_HEAD_DIM // 128
HEAD_ROWS = HEAD_SPLIT * XA_HEADS


def _head_rows(x):
    lead = x.shape[:-2]
    x = x.reshape(*lead, XA_HEADS, HEAD_SPLIT, 128)
    return jnp.swapaxes(x, -2, -3).reshape(*lead, HEAD_ROWS, 128)


def _xattn_sample_kernel(q_ref, k_ref, v_ref, o_ref, *, tb):
    scale = XA_HEAD_DIM ** -0.5
    n_rows = N_MEM * HEAD_ROWS
    lane = lax.broadcasted_iota(jnp.int32, (HEAD_ROWS, n_rows), 1)
    sub = lax.broadcasted_iota(jnp.int32, (HEAD_ROWS, n_rows), 0)
    own = (lane % HEAD_ROWS) == sub
    low_piece = (lax.broadcasted_iota(jnp.int32, (1, n_rows), 1) % HEAD_ROWS) < XA_HEADS

    def lane_group_reduce(x, op):
        shift = HEAD_ROWS
        while shift < 128:
            x = op(x, pltpu.roll(x, shift, 1))
            shift *= 2
        return x

    def per_kind(x, op):
        acc = x[:, 0:128]
        for c in range(1, n_rows // 128):
            acc = op(acc, x[:, c * 128:(c + 1) * 128])
        acc = lane_group_reduce(acc, op)
        return jnp.concatenate([acc] * (n_rows // 128), axis=1)

    parts = []
    for b in range(tb):
        qb = q_ref[b].astype(BF16)
        kb = k_ref[b].reshape(n_rows, 128).astype(BF16)
        c = lax.dot_general(qb, kb, (((1,), (1,)), ((), ())), preferred_element_type=F32)
        parts.append(jnp.sum(jnp.where(own, c, 0.0), axis=0, keepdims=True))
    part = jnp.concatenate(parts, axis=0)
    other = jnp.where(low_piece, pltpu.roll(part, n_rows - XA_HEADS, 1), pltpu.roll(part, XA_HEADS, 1))
    s = (part + other) * scale
    e = jnp.exp(s - per_kind(s, jnp.maximum))
    p = e * (1.0 / per_kind(e, jnp.add))
    for b in range(tb):
        vb = v_ref[b].reshape(n_rows, 128).astype(BF16)
        pm = jnp.where(own, jnp.broadcast_to(p[b:b + 1, :], (HEAD_ROWS, n_rows)), 0.0).astype(BF16)
        o_ref[b] = _dot(pm, vb).astype(BF16)


def _xattn_sample(q, k_cache, v_cache, *, tb=8):
    n = q.shape[0]
    kv_spec = pl.BlockSpec((tb, N_MEM, HEAD_ROWS, 128), lambda i: (i, 0, 0, 0))
    q_spec = pl.BlockSpec((tb, HEAD_ROWS, 128), lambda i: (i, 0, 0))
    return pl.pallas_call(
        functools.partial(_xattn_sample_kernel, tb=tb),
        grid=(n // tb,),
        in_specs=[q_spec, kv_spec, kv_spec],
        out_specs=q_spec,
        out_shape=jax.ShapeDtypeStruct((n, HEAD_ROWS, 128), BF16),
        **_call_params("xattn_sample", 1),
    )(q, k_cache, v_cache)


N_BRANCH = 3


def _merge_kernel(y_ref, wb_ref, g0_ref, g1_ref, g2_ref, o_ref):
    acc = None
    for k, gate_ref in enumerate((g0_ref, g1_ref, g2_ref)):
        term = gate_ref[...].astype(F32) * _dot(y_ref[k], wb_ref[k])
        acc = term if acc is None else acc + term
    o_ref[...] = acc.astype(BF16)


def _merge(y, w_branch, z, *, tm):
    m = y.shape[1]
    gate_spec = lambda k: pl.BlockSpec((tm, BR_W), lambda i, n, k=k: (i, Z_GATE0 + 2 * k + n))
    return pl.pallas_call(
        _merge_kernel,
        grid=(m // tm, D_MODEL // BR_W),
        in_specs=[
            pl.BlockSpec((N_BRANCH, tm, BR_W), lambda i, n: (0, i, 0)),
            pl.BlockSpec((N_BRANCH, BR_W, BR_W), lambda i, n: (0, 0, n)),
            gate_spec(0), gate_spec(1), gate_spec(2),
        ],
        out_specs=pl.BlockSpec((tm, BR_W), lambda i, n: (i, n)),
        out_shape=jax.ShapeDtypeStruct((m, D_MODEL), BF16),
        **_call_params("merge", 2),
    )(y, w_branch, z, z, z)


def _outproj_kernel(m_ref, wo_ref, x_ref, g_ref, b_ref, o_ref, *, row_splits):
    rows_per = m_ref.shape[0] // row_splits
    for r in range(row_splits):
        rows = slice(r * rows_per, (r + 1) * rows_per)
        y = ALPHA * x_ref[rows, :] + _dot(m_ref[rows, :], wo_ref[...])
        o_ref[rows, :] = _layer_norm(y, g_ref[...], b_ref[...])


def _outproj(merged, w_out, x, ln_g, ln_b, *, tm):
    m = x.shape[0]
    return pl.pallas_call(
        functools.partial(_outproj_kernel, row_splits=max(1, tm // OUTPROJ_ROW_GROUP)),
        grid=(m // tm,),
        in_specs=[
            pl.BlockSpec((tm, D_MODEL), lambda i: (i, 0)),
            pl.BlockSpec((D_MODEL, D_MODEL), lambda i: (0, 0)),
            pl.BlockSpec((tm, D_MODEL), lambda i: (i, 0)),
            pl.BlockSpec((1, D_MODEL), lambda i: (0, 0)),
            pl.BlockSpec((1, D_MODEL), lambda i: (0, 0)),
        ],
        out_specs=pl.BlockSpec((tm, D_MODEL), lambda i: (i, 0)),
        out_shape=jax.ShapeDtypeStruct((m, D_MODEL), F32),
        **_call_params("out_proj", 1),
    )(merged, w_out, x, ln_g, ln_b)


def kernel(x_prompt, x_sample, mem_prompt, cache_mem_k, cache_mem_v, state_conv, state_lru_h, ffn1_w_gu, ffn1_w_down, ln1_g, ln1_b, w_in, gate_b, gmlp_ln_g, gmlp_ln_b, gmlp_w_s, gmlp_b_s, conv_w, conv_b, lru_w_a, lru_b_a, lru_w_x, lru_b_x, lru_lambda, mem_ln_g, mem_ln_b, w_mem_kv, w_branch, w_out, ln2_g, ln2_b, ffn2_w_gu, ffn2_w_down, ln3_g, ln3_b):
    bsz, seq, _ = x_prompt.shape
    n_s = x_sample.shape[0]
    l = 0

    wa, wx = lru_w_a[l].astype(BF16), lru_w_x[l].astype(BF16)

    row = lambda p: p[l].reshape(1, -1)
    gb = gate_b[l].reshape(1, -1)
    bs_mat = jnp.repeat(gmlp_b_s[l].T, GROUP_W, axis=1)
    wvec = jnp.repeat(gmlp_w_s[l][:, 0, 0], GROUP_W).reshape(1, -1)
    bvec = jnp.repeat(gmlp_b_s[l][:, 0], GROUP_W).reshape(1, -1)
    lru = (wa, row(lru_b_a), wx, row(lru_b_x), row(lru_lambda))

    xp = x_prompt.reshape(bsz * seq, D_MODEL)
    xs = x_sample.reshape(n_s, D_MODEL)

    x1s, w1g, w1u, w1d = _ffn(xs, ffn1_w_gu[l], ffn1_w_gu[l], ffn1_w_down[l], row(ln1_g), row(ln1_b),
                              tm=n_s, emit_bf16=True)
    k_p, v_p = _memkv(mem_prompt.reshape(bsz * N_MEM, D_MODEL), w_mem_kv[l], row(mem_ln_g), row(mem_ln_b))
    x1p, win, w2d = _ffn(xp, w1g, w1u, w1d, row(ln1_g), row(ln1_b), tm=1024, side=(w_in[l], ffn2_w_down[l]))

    zs, rxs = _inproj(x1s, win, gb, row(gmlp_ln_g), row(gmlp_ln_b), tm=n_s, z_dtype=F32)
    y01, conv_s, h_s = _mixer_sample(
        zs, rxs, jnp.swapaxes(state_conv[l], 0, 1), state_lru_h[l], wvec, bvec, conv_w[l], row(conv_b), *lru)
    q_s = zs[:, Z_Q * BR_W:(Z_Q + 1) * BR_W].reshape(n_s, XA_HEADS, XA_HEAD_DIM)
    yxa = _xattn_sample(_head_rows(q_s), _head_rows(cache_mem_k[l]), _head_rows(cache_mem_v[l]))
    yxa = jnp.swapaxes(yxa.reshape(n_s, HEAD_SPLIT, XA_HEADS, 128), 1, 2)
    ys = jnp.concatenate([y01, yxa.reshape(1, n_s, BR_W)], axis=0)

    zp, rxp, w2gu = _inproj(x1p, win, gb, row(gmlp_ln_g), row(gmlp_ln_b), tm=1024, z_dtype=BF16,
                            side=(ffn2_w_gu[l],))
    yp, conv_p, hlast_p, wbr, wout = _mixer_prompt(
        zp, rxp, k_p.reshape(bsz, N_MEM, BR_W), v_p.reshape(bsz, N_MEM, BR_W), gmlp_w_s[l], bs_mat,
        conv_w[l], row(conv_b), *lru, bsz=bsz, seq=seq,
        side=(w_branch[l].reshape(N_BRANCH * BR_W, D_MODEL), w_out[l]))
    wbr = wbr.reshape(N_BRANCH, BR_W, D_MODEL)

    ms = _merge(ys, wbr, zs, tm=n_s)
    x2s = _outproj(ms, wout, x1s, row(ln2_g), row(ln2_b), tm=n_s)
    x3s = _ffn(x2s, w2gu, w2gu, w2d, row(ln3_g), row(ln3_b), tm=n_s)
    mp = _merge(yp, wbr, zp, tm=1024)
    x2p = _outproj(mp, wout, x1p, row(ln2_g), row(ln2_b), tm=512)
    x3p = _ffn(x2p, w2gu, w2gu, w2d, row(ln3_g), row(ln3_b), tm=1024)

    kv_shape = (1, bsz, N_MEM, XA_HEADS, XA_HEAD_DIM)
    return (x3p.reshape(bsz, seq, D_MODEL),
            x3s.reshape(n_s, 1, D_MODEL),
            k_p.reshape(kv_shape),
            v_p.reshape(kv_shape),
            conv_p[None],
            hlast_p.reshape(1, bsz, BR_W),
            jnp.swapaxes(conv_s, 0, 1)[None],
            h_s[None],
            zs[:, Z_V * BR_W:(Z_V + 1) * BR_W].reshape(1, n_s, 1, BR_W))
```

```python
import functools

import jax
import jax.numpy as jnp
from jax import lax
from jax.experimental import pallas as pl
from jax.experimental.pallas import tpu as pltpu

F32 = jnp.float32
BF16 = jnp.bfloat16

D_MODEL = 2048
BR_W = 1024
D_FF = 5632
N_MEM = 256
XA_HEADS = 4
XA_HEAD_DIM = 256
GMLP_GROUPS = 4
GROUP_W = BR_W // GMLP_GROUPS
CHUNK = 128
LRU_BLOCKS = 8
LRU_BLOCK = 128
CONV_W = 4
LRU_C = 8.0
LN_EPS = 1e-5
ALPHA = 2.0 ** 0.25
IN_BLOCKS = 11
RX_BLOCK = 2
GATE_BLOCK0 = 5
Z_BLOCKS = IN_BLOCKS - 1
Z_GU, Z_V, Z_GRG, Z_Q, Z_GATE0 = 0, 1, 2, 3, 4
SUBLANES = 8
FFN_ROW_GROUP = 512
OUTPROJ_ROW_GROUP = 128
MiB = 1024 * 1024


VMEM_LIMIT_MIB = {
    "ffn_ln": 62,
    "mem_kv": 48,
    "in_proj": 60,
    "mixer_prompt": 48,
    "mixer_sample": 32,
    "xattn_sample": 48,
    "merge": 56,
    "out_proj": 48,
}


def _call_params(name, n_grid_axes):
    return dict(name=name, compiler_params=pltpu.CompilerParams(
        dimension_semantics=("arbitrary",) * n_grid_axes, vmem_limit_bytes=VMEM_LIMIT_MIB[name] * MiB))


def _layer_norm(x, g, b):
    mu = jnp.mean(x, axis=-1, keepdims=True)
    xc = x - mu
    var = jnp.mean(xc * xc, axis=-1, keepdims=True)
    return xc * lax.rsqrt(var + LN_EPS) * g + b


def _dot(a, b):
    return jnp.dot(a, b, preferred_element_type=F32)


def _sigmoid(x):
    return 0.5 * jnp.tanh(0.5 * x) + 0.5


def _mxu_operand(w_ref, w16_ref=None):
    w = w_ref[...].astype(BF16)
    if w16_ref is not None:
        w16_ref[...] = w
    return w


def _side_cast_plan(side, grid):
    n_steps = 1
    for g in grid:
        n_steps *= g

    def linear(*idx):
        lin = idx[0]
        for g, i in zip(grid[1:], idx[1:]):
            lin = lin * g + i
        return lin

    specs, shapes = [], []
    for w in side:
        r, c = w.shape
        if c % (n_steps * 128) == 0:
            spec = pl.BlockSpec((r, c // n_steps), lambda *idx: (0, linear(*idx)))
        else:
            assert r % (n_steps * 16) == 0, (w.shape, n_steps)
            spec = pl.BlockSpec((r // n_steps, c), lambda *idx: (linear(*idx), 0))
        specs.append(spec)
        shapes.append(jax.ShapeDtypeStruct(w.shape, BF16))
    return specs, shapes


def _side_cast(in_refs, out_refs):
    for src, dst in zip(in_refs, out_refs):
        dst[...] = src[...].astype(BF16)


def _ffn_kernel(x_ref, wg_ref, wu_ref, wd_ref, g_ref, b_ref, *rest, emit_bf16, row_splits, n_side):
    side_in, rest = rest[:n_side], rest[n_side:]
    o_ref, rest = rest[0], rest[1:]
    if emit_bf16:
        (wg16_ref, wu16_ref, wd16_ref), rest = rest[:3], rest[3:]
    else:
        wg16_ref = wu16_ref = wd16_ref = None
    side_out, (xb_ref,) = rest[:n_side], rest[n_side:]
    _side_cast(side_in, side_out)
    j = pl.program_id(1)
    last = pl.num_programs(1) - 1
    rows_per = xb_ref.shape[0] // row_splits

    def step(first, final):
        wg = _mxu_operand(wg_ref, wg16_ref)
        wu = _mxu_operand(wu_ref, wu16_ref)
        wd = _mxu_operand(wd_ref, wd16_ref)
        for r in range(row_splits):
            rows = slice(r * rows_per, (r + 1) * rows_per)
            if first:
                xb = x_ref[rows, :].astype(BF16)
                xb_ref[rows, :] = xb
            else:
                xb = xb_ref[rows, :]
            g = _dot(xb, wg)
            u = _dot(xb, wu)
            h = (g * _sigmoid(g) * u).astype(BF16)
            acc = _dot(h, wd)
            if not first:
                acc = o_ref[rows, :] + acc
            if final:
                acc = _layer_norm(ALPHA * x_ref[rows, :] + 0.5 * acc, g_ref[...], b_ref[...])
            o_ref[rows, :] = acc

    pl.when(j == 0)(functools.partial(step, True, False))
    pl.when((j > 0) & (j < last))(functools.partial(step, False, False))
    pl.when(j == last)(functools.partial(step, False, True))


def _ffn(x, w_g, w_u, w_down, ln_g, ln_b, *, tm, emit_bf16=False, side=(), tf=512):
    m = x.shape[0]
    nf = D_FF // tf
    grid = (m // tm, nf)
    u_off = nf if w_u.shape[1] == 2 * D_FF else 0
    assert not emit_bf16 or m == tm
    side_specs, side_shapes = _side_cast_plan(side, grid)
    out_specs = [pl.BlockSpec((tm, D_MODEL), lambda i, j: (i, 0))]
    out_shape = [jax.ShapeDtypeStruct((m, D_MODEL), F32)]
    if emit_bf16:
        out_specs += [pl.BlockSpec((D_MODEL, tf), lambda i, j: (0, j)),
                      pl.BlockSpec((D_MODEL, tf), lambda i, j: (0, j)),
                      pl.BlockSpec((tf, D_MODEL), lambda i, j: (j, 0))]
        out_shape += [jax.ShapeDtypeStruct((D_MODEL, D_FF), BF16),
                      jax.ShapeDtypeStruct((D_MODEL, D_FF), BF16),
                      jax.ShapeDtypeStruct((D_FF, D_MODEL), BF16)]
    outs = pl.pallas_call(
        functools.partial(_ffn_kernel, emit_bf16=emit_bf16, row_splits=max(1, tm // FFN_ROW_GROUP),
                          n_side=len(side)),
        grid=grid,
        in_specs=[
            pl.BlockSpec((tm, D_MODEL), lambda i, j: (i, 0)),
            pl.BlockSpec((D_MODEL, tf), lambda i, j: (0, j)),
            pl.BlockSpec((D_MODEL, tf), lambda i, j: (0, j + u_off)),
            pl.BlockSpec((tf, D_MODEL), lambda i, j: (j, 0)),
            pl.BlockSpec((1, D_MODEL), lambda i, j: (0, 0)),
            pl.BlockSpec((1, D_MODEL), lambda i, j: (0, 0)),
        ] + side_specs,
        out_specs=out_specs + side_specs,
        out_shape=out_shape + side_shapes,
        scratch_shapes=[pltpu.VMEM((tm, D_MODEL), BF16)],
        **_call_params("ffn_ln", 2),
    )(x, w_g, w_u, w_down, ln_g, ln_b, *side)
    return outs if (emit_bf16 or side) else outs[0]


def _memkv_kernel(x_ref, wk_ref, wv_ref, g_ref, b_ref, k_ref, v_ref, xb_ref):
    @pl.when(pl.program_id(1) == 0)
    def _():
        xb_ref[...] = _layer_norm(x_ref[...], g_ref[...], b_ref[...]).astype(BF16)

    xb = xb_ref[...]
    k_ref[...] = _dot(xb, _mxu_operand(wk_ref))
    v_ref[...] = _dot(xb, _mxu_operand(wv_ref))


def _memkv(mem, w_kv, ln_g, ln_b, *, tm=1024, tn=512):
    m = mem.shape[0]
    nn = BR_W // tn
    return pl.pallas_call(
        _memkv_kernel,
        grid=(m // tm, nn),
        in_specs=[
            pl.BlockSpec((tm, D_MODEL), lambda i, j: (i, 0)),
            pl.BlockSpec((D_MODEL, tn), lambda i, j: (0, j)),
            pl.BlockSpec((D_MODEL, tn), lambda i, j: (0, j + nn)),
            pl.BlockSpec((1, D_MODEL), lambda i, j: (0, 0)),
            pl.BlockSpec((1, D_MODEL), lambda i, j: (0, 0)),
        ],
        out_specs=[pl.BlockSpec((tm, tn), lambda i, j: (i, j)),
                   pl.BlockSpec((tm, tn), lambda i, j: (i, j))],
        out_shape=[jax.ShapeDtypeStruct((m, BR_W), F32), jax.ShapeDtypeStruct((m, BR_W), F32)],
        scratch_shapes=[pltpu.VMEM((tm, D_MODEL), BF16)],
        **_call_params("mem_kv", 2),
    )(mem, w_kv, w_kv, ln_g, ln_b)


W_RING = 3


def _inproj_kernel(x_ref, w_hbm, gb_ref, lg_ref, lb_ref, *rest, n_side):
    side_in, rest = rest[:n_side], rest[n_side:]
    (o_ref, rx_ref), rest = rest[:2], rest[2:]
    side_out, (xb_ref, wbuf_ref, wsem) = rest[:n_side], rest[n_side:]
    _side_cast(side_in, side_out)
    j = pl.program_id(1)
    n_blocks = pl.num_programs(1)
    step = pl.program_id(0) * n_blocks + j
    n_steps = pl.num_programs(0) * n_blocks

    def weight_copy(s):
        slot = s % W_RING
        col0 = pl.multiple_of((s % n_blocks) * BR_W, BR_W)
        return pltpu.make_async_copy(w_hbm.at[:, pl.ds(col0, BR_W)], wbuf_ref.at[slot], wsem.at[slot])

    @pl.when(step == 0)
    def _():
        for s in range(W_RING - 1):
            weight_copy(step + s).start()

    @pl.when(step + (W_RING - 1) < n_steps)
    def _():
        weight_copy(step + (W_RING - 1)).start()

    weight_copy(step).wait()
    w_ref = wbuf_ref.at[step % W_RING]

    @pl.when(j == 0)
    def _():
        xb_ref[...] = x_ref[...].astype(BF16)

    def emit(dst_ref, act):
        dst_ref[...] = act(_dot(xb_ref[...], w_ref[...])).astype(dst_ref.dtype)

    @pl.when((j == 0) | (j == 3))
    def _():
        emit(o_ref, jax.nn.gelu)

    @pl.when(j == 1)
    def _():
        emit(o_ref, lambda z: _layer_norm(jax.nn.gelu(z), lg_ref[...], lb_ref[...]))

    @pl.when(j == 2)
    def _():
        emit(rx_ref, lambda z: z)

    @pl.when(j == 4)
    def _():
        emit(o_ref, lambda z: z)

    @pl.when(j >= GATE_BLOCK0)
    def _():
        emit(o_ref, lambda z: _sigmoid(z + gb_ref[...]))


def _inproj(x, w_in, gate_b, gln_g, gln_b, *, tm, z_dtype, side=()):
    m = x.shape[0]
    grid = (m // tm, IN_BLOCKS)
    side_specs, side_shapes = _side_cast_plan(side, grid)
    out_specs = [pl.BlockSpec((tm, BR_W), lambda i, j: (i, jnp.where(j < RX_BLOCK, j, j - 1))),
                 pl.BlockSpec((tm, BR_W), lambda i, j: (i, 0))]
    out_shape = [jax.ShapeDtypeStruct((m, Z_BLOCKS * BR_W), z_dtype),
                 jax.ShapeDtypeStruct((m, BR_W), F32)]
    return pl.pallas_call(
        functools.partial(_inproj_kernel, n_side=len(side)),
        grid=grid,
        in_specs=[
            pl.BlockSpec((tm, D_MODEL), lambda i, j: (i, 0)),
            pl.BlockSpec(memory_space=pl.ANY),
            pl.BlockSpec((1, BR_W), lambda i, j: (0, jnp.maximum(j - GATE_BLOCK0, 0))),
            pl.BlockSpec((1, BR_W), lambda i, j: (0, 0)),
            pl.BlockSpec((1, BR_W), lambda i, j: (0, 0)),
        ] + side_specs,
        out_specs=out_specs + side_specs,
        out_shape=out_shape + side_shapes,
        scratch_shapes=[pltpu.VMEM((tm, D_MODEL), BF16),
                        pltpu.VMEM((W_RING, D_MODEL, BR_W), BF16),
                        pltpu.SemaphoreType.DMA((W_RING,))],
        **_call_params("in_proj", 2),
    )(x, w_in, gate_b, gln_g, gln_b, *side)


def _softplus(x):
    return jnp.maximum(x, 0.0) + jnp.log1p(jnp.exp(-jnp.abs(x)))


def _lru_coeffs(xc, wa_ref, ba_ref, wx_ref, bx_ref, lam_ref):
    xcb = xc.astype(BF16)
    r_parts, i_parts = [], []
    for k in range(LRU_BLOCKS):
        xk = xcb[:, k * LRU_BLOCK:(k + 1) * LRU_BLOCK]
        r_parts.append(_dot(xk, wa_ref[k]))
        i_parts.append(_dot(xk, wx_ref[k]))
    r = _sigmoid(jnp.concatenate(r_parts, axis=1) + ba_ref[...])
    i = _sigmoid(jnp.concatenate(i_parts, axis=1) + bx_ref[...])
    log_a = (-LRU_C) * r * _softplus(-lam_ref[...])
    a = jnp.exp(log_a)
    m = -jnp.tanh(log_a) * (a * a + 1.0)
    root = jnp.where(m > 0.0, m * lax.rsqrt(m), 0.0)
    return a, root * (i * xc)


def _mixer_kernel(gu_ref, v_ref, rx_ref, grg_ref, q_ref, k_ref, vm_ref,
                  ws_ref, bs_ref, cw_ref, cb_ref, wa_ref, ba_ref, wx_ref, bx_ref, lam_ref,
                  *rest, tm, n_side):
    side_in, rest = rest[:n_side], rest[n_side:]
    (y_ref, conv_ref, hlast_ref), rest = rest[:3], rest[3:]
    side_out, (xpad_ref, a_ref, b_ref, h_ref, hc_ref) = rest[:n_side], rest[n_side:]
    _side_cast(side_in, side_out)
    t = pl.program_id(1)

    tri = (lax.broadcasted_iota(jnp.int32, (CHUNK, CHUNK), 0)
           >= lax.broadcasted_iota(jnp.int32, (CHUNK, CHUNK), 1))
    for g in range(GMLP_GROUPS):
        wg = jnp.where(tri, ws_ref[g], 0.0).astype(BF16)
        cols = slice(g * GROUP_W, (g + 1) * GROUP_W)
        for c in range(tm // CHUNK):
            rows = slice(c * CHUNK, (c + 1) * CHUNK)
            s = _dot(wg, v_ref[rows, cols].astype(BF16)) + bs_ref[:, cols]
            y_ref[0, rows, cols] = (gu_ref[rows, cols] * s).astype(BF16)

    @pl.when(t == 0)
    def _():
        xpad_ref[0:SUBLANES, :] = jnp.zeros((SUBLANES, BR_W), F32)
        hc_ref[...] = jnp.zeros((1, BR_W), F32)

    @pl.when(t > 0)
    def _():
        xpad_ref[0:SUBLANES, :] = xpad_ref[tm:tm + SUBLANES, :]

    xpad_ref[SUBLANES:SUBLANES + tm, :] = rx_ref[...]
    xfull = xpad_ref[...]
    acc = xfull * cw_ref[0:1, :]
    for k in range(1, CONV_W):
        acc = pltpu.roll(acc, 1, 0) + xfull * cw_ref[k:k + 1, :]
    xc = acc[SUBLANES:, :] + cb_ref[...]

    a, b = _lru_coeffs(xc, wa_ref, ba_ref, wx_ref, bx_ref, lam_ref)
    a_ref[...] = a
    b_ref[...] = b

    row = lax.broadcasted_iota(jnp.int32, (SUBLANES, BR_W), 0)
    keep = [row >= d for d in (1, 2, 4)]

    def scan_block(blk, h):
        base = pl.multiple_of(blk * SUBLANES, SUBLANES)
        ca = a_ref[pl.ds(base, SUBLANES), :]
        cb = b_ref[pl.ds(base, SUBLANES), :]
        for d, kp in zip((1, 2, 4), keep):
            a_sh = jnp.where(kp, pltpu.roll(ca, d, 0), 1.0)
            b_sh = jnp.where(kp, pltpu.roll(cb, d, 0), 0.0)
            cb = ca * b_sh + cb
            ca = ca * a_sh
        hh = cb + ca * h
        h_ref[pl.ds(base, SUBLANES), :] = hh
        return hh[SUBLANES - 1:SUBLANES, :]

    h_end = lax.fori_loop(0, tm // SUBLANES, scan_block, hc_ref[...])
    hc_ref[...] = h_end
    y_ref[1] = (grg_ref[...] * h_ref[...]).astype(BF16)

    @pl.when(t == pl.num_programs(1) - 1)
    def _():
        conv_ref[0] = rx_ref[tm - (CONV_W - 1):tm, :]
        hlast_ref[0] = h_end

    scale = XA_HEAD_DIM ** -0.5
    for hd in range(XA_HEADS):
        cols = slice(hd * XA_HEAD_DIM, (hd + 1) * XA_HEAD_DIM)
        qh = q_ref[:, cols].astype(BF16)
        kh = k_ref[0, :, cols].astype(BF16)
        vh = vm_ref[0, :, cols].astype(BF16)
        s = lax.dot_general(qh, kh, (((1,), (1,)), ((), ())), preferred_element_type=F32) * scale
        e = jnp.exp(s - jnp.max(s, axis=-1, keepdims=True))
        p = e * (1.0 / jnp.sum(e, axis=-1, keepdims=True))
        y_ref[2, :, cols] = _dot(p.astype(BF16), vh).astype(BF16)


def _mixer_prompt(z, rx, k_mem, v_mem, ws, bs_mat, conv_w, conv_b, wa, ba, wx, bx, lam, *, bsz, seq, side=(),
                  tm=512):
    nt = seq // tm
    zspec = lambda c: pl.BlockSpec((tm, BR_W), lambda b, t, c=c: (b * nt + t, c))
    full = lambda shape: pl.BlockSpec(shape, lambda b, t: (0,) * len(shape))
    side_specs, side_shapes = _side_cast_plan(side, (bsz, nt))
    return pl.pallas_call(
        functools.partial(_mixer_kernel, tm=tm, n_side=len(side)),
        grid=(bsz, nt),
        in_specs=[zspec(Z_GU), zspec(Z_V), zspec(0), zspec(Z_GRG), zspec(Z_Q),
                  pl.BlockSpec((1, N_MEM, BR_W), lambda b, t: (b, 0, 0)),
                  pl.BlockSpec((1, N_MEM, BR_W), lambda b, t: (b, 0, 0)),
                  full((GMLP_GROUPS, CHUNK, CHUNK)), full((CHUNK, BR_W)),
                  full((CONV_W, BR_W)), full((1, BR_W)),
                  full((LRU_BLOCKS, LRU_BLOCK, LRU_BLOCK)), full((1, BR_W)),
                  full((LRU_BLOCKS, LRU_BLOCK, LRU_BLOCK)), full((1, BR_W)),
                  full((1, BR_W))] + side_specs,
        out_specs=[pl.BlockSpec((3, tm, BR_W), lambda b, t: (0, b * nt + t, 0)),
                   pl.BlockSpec((1, CONV_W - 1, BR_W), lambda b, t: (b, 0, 0)),
                   pl.BlockSpec((1, 1, BR_W), lambda b, t: (b, 0, 0))] + side_specs,
        out_shape=[jax.ShapeDtypeStruct((3, bsz * seq, BR_W), BF16),
                   jax.ShapeDtypeStruct((bsz, CONV_W - 1, BR_W), F32),
                   jax.ShapeDtypeStruct((bsz, 1, BR_W), F32)] + side_shapes,
        scratch_shapes=[pltpu.VMEM((tm + SUBLANES, BR_W), F32),
                        pltpu.VMEM((tm, BR_W), F32), pltpu.VMEM((tm, BR_W), F32),
                        pltpu.VMEM((tm, BR_W), F32), pltpu.VMEM((1, BR_W), F32)],
        **_call_params("mixer_prompt", 2),
    )(z, z, rx, z, z, k_mem, v_mem, ws, bs_mat, conv_w, conv_b, wa, ba, wx, bx, lam, *side)


def _mixer_sample_kernel(gu_ref, v_ref, rx_ref, grg_ref, cs_ref, h0_ref,
                         wv_ref, bv_ref, cw_ref, cb_ref, wa_ref, ba_ref, wx_ref, bx_ref, lam_ref,
                         y_ref, conv_ref, h_ref):
    y_ref[0] = (gu_ref[...] * (wv_ref[...] * v_ref[...] + bv_ref[...])).astype(BF16)

    rx = rx_ref[...]
    xc = cb_ref[...] + rx * cw_ref[CONV_W - 1:CONV_W, :]
    for k in range(CONV_W - 1):
        xc = xc + cs_ref[k] * cw_ref[k:k + 1, :]
    for k in range(CONV_W - 2):
        conv_ref[k] = cs_ref[k + 1]
    conv_ref[CONV_W - 2] = rx

    a, b = _lru_coeffs(xc, wa_ref, ba_ref, wx_ref, bx_ref, lam_ref)
    h = a * h0_ref[...] + b
    h_ref[...] = h
    y_ref[1] = (grg_ref[...] * h).astype(BF16)


def _mixer_sample(z, rx, conv_state, h0, wvec, bvec, conv_w, conv_b, wa, ba, wx, bx, lam):
    n = z.shape[0]
    zspec = lambda c: pl.BlockSpec((n, BR_W), lambda i, c=c: (0, c))
    full = lambda shape: pl.BlockSpec(shape, lambda i: (0,) * len(shape))
    return pl.pallas_call(
        _mixer_sample_kernel,
        grid=(1,),
        in_specs=[zspec(Z_GU), zspec(Z_V), zspec(0), zspec(Z_GRG),
                  full((CONV_W - 1, n, BR_W)), full((n, BR_W)),
                  full((1, BR_W)), full((1, BR_W)), full((CONV_W, BR_W)), full((1, BR_W)),
                  full((LRU_BLOCKS, LRU_BLOCK, LRU_BLOCK)), full((1, BR_W)),
                  full((LRU_BLOCKS, LRU_BLOCK, LRU_BLOCK)), full((1, BR_W)),
                  full((1, BR_W))],
        out_specs=[full((2, n, BR_W)), full((CONV_W - 1, n, BR_W)), full((n, BR_W))],
        out_shape=[jax.ShapeDtypeStruct((2, n, BR_W), BF16),
                   jax.ShapeDtypeStruct((CONV_W - 1, n, BR_W), F32),
                   jax.ShapeDtypeStruct((n, BR_W), F32)],
        **_call_params("mixer_sample", 1),
    )(z, z, rx, z, conv_state, h0, wvec, bvec, conv_w, conv_b, wa, ba, wx, bx, lam)


HEAD_SPLIT = XA_HEAD_DIM // 128
HEAD_ROWS = HEAD_SPLIT * XA_HEADS


def _head_rows(x):
    lead = x.shape[:-2]
    x = x.reshape(*lead, XA_HEADS, HEAD_SPLIT, 128)
    return jnp.swapaxes(x, -2, -3).reshape(*lead, HEAD_ROWS, 128)


def _xattn_sample_kernel(q_ref, k_ref, v_ref, o_ref, *, tb):
    scale = XA_HEAD_DIM ** -0.5
    n_rows = N_MEM * HEAD_ROWS
    lane = lax.broadcasted_iota(jnp.int32, (HEAD_ROWS, n_rows), 1)
    sub = lax.broadcasted_iota(jnp.int32, (HEAD_ROWS, n_rows), 0)
    own = (lane % HEAD_ROWS) == sub
    low_piece = (lax.broadcasted_iota(jnp.int32, (1, n_rows), 1) % HEAD_ROWS) < XA_HEADS

    def lane_group_reduce(x, op):
        shift = HEAD_ROWS
        while shift < 128:
            x = op(x, pltpu.roll(x, shift, 1))
            shift *= 2
        return x

    def per_kind(x, op):
        acc = x[:, 0:128]
        for c in range(1, n_rows // 128):
            acc = op(acc, x[:, c * 128:(c + 1) * 128])
        acc = lane_group_reduce(acc, op)
        return jnp.concatenate([acc] * (n_rows // 128), axis=1)

    parts = []
    for b in range(tb):
        qb = q_ref[b].astype(BF16)
        kb = k_ref[b].reshape(n_rows, 128).astype(BF16)
        c = lax.dot_general(qb, kb, (((1,), (1,)), ((), ())), preferred_element_type=F32)
        parts.append(jnp.sum(jnp.where(own, c, 0.0), axis=0, keepdims=True))
    part = jnp.concatenate(parts, axis=0)
    other = jnp.where(low_piece, pltpu.roll(part, n_rows - XA_HEADS, 1), pltpu.roll(part, XA_HEADS, 1))
    s = (part + other) * scale
    e = jnp.exp(s - per_kind(s, jnp.maximum))
    p = e * (1.0 / per_kind(e, jnp.add))
    for b in range(tb):
        vb = v_ref[b].reshape(n_rows, 128).astype(BF16)
        pm = jnp.where(own, jnp.broadcast_to(p[b:b + 1, :], (HEAD_ROWS, n_rows)), 0.0).astype(BF16)
        o_ref[b] = _dot(pm, vb).astype(BF16)


def _xattn_sample(q, k_cache, v_cache, *, tb=8):
    n = q.shape[0]
    kv_spec = pl.BlockSpec((tb, N_MEM, HEAD_ROWS, 128), lambda i: (i, 0, 0, 0))
    q_spec = pl.BlockSpec((tb, HEAD_ROWS, 128), lambda i: (i, 0, 0))
    return pl.pallas_call(
        functools.partial(_xattn_sample_kernel, tb=tb),
        grid=(n // tb,),
        in_specs=[q_spec, kv_spec, kv_spec],
        out_specs=q_spec,
        out_shape=jax.ShapeDtypeStruct((n, HEAD_ROWS, 128), BF16),
        **_call_params("xattn_sample", 1),
    )(q, k_cache, v_cache)


N_BRANCH = 3


def _merge_kernel(y_ref, wb_ref, g0_ref, g1_ref, g2_ref, o_ref):
    acc = None
    for k, gate_ref in enumerate((g0_ref, g1_ref, g2_ref)):
        term = gate_ref[...].astype(F32) * _dot(y_ref[k], wb_ref[k])
        acc = term if acc is None else acc + term
    o_ref[...] = acc.astype(BF16)


def _merge(y, w_branch, z, *, tm):
    m = y.shape[1]
    gate_spec = lambda k: pl.BlockSpec((tm, BR_W), lambda i, n, k=k: (i, Z_GATE0 + 2 * k + n))
    return pl.pallas_call(
        _merge_kernel,
        grid=(m // tm, D_MODEL // BR_W),
        in_specs=[
            pl.BlockSpec((N_BRANCH, tm, BR_W), lambda i, n: (0, i, 0)),
            pl.BlockSpec((N_BRANCH, BR_W, BR_W), lambda i, n: (0, 0, n)),
            gate_spec(0), gate_spec(1), gate_spec(2),
        ],
        out_specs=pl.BlockSpec((tm, BR_W), lambda i, n: (i, n)),
        out_shape=jax.ShapeDtypeStruct((m, D_MODEL), BF16),
        **_call_params("merge", 2),
    )(y, w_branch, z, z, z)


def _outproj_kernel(m_ref, wo_ref, x_ref, g_ref, b_ref, o_ref, *, row_splits):
    rows_per = m_ref.shape[0] // row_splits
    for r in range(row_splits):
        rows = slice(r * rows_per, (r + 1) * rows_per)
        y = ALPHA * x_ref[rows, :] + _dot(m_ref[rows, :], wo_ref[...])
        o_ref[rows, :] = _layer_norm(y, g_ref[...], b_ref[...])


def _outproj(merged, w_out, x, ln_g, ln_b, *, tm):
    m = x.shape[0]
    return pl.pallas_call(
        functools.partial(_outproj_kernel, row_splits=max(1, tm // OUTPROJ_ROW_GROUP)),
        grid=(m // tm,),
        in_specs=[
            pl.BlockSpec((tm, D_MODEL), lambda i: (i, 0)),
            pl.BlockSpec((D_MODEL, D_MODEL), lambda i: (0, 0)),
            pl.BlockSpec((tm, D_MODEL), lambda i: (i, 0)),
            pl.BlockSpec((1, D_MODEL), lambda i: (0, 0)),
            pl.BlockSpec((1, D_MODEL), lambda i: (0, 0)),
        ],
        out_specs=pl.BlockSpec((tm, D_MODEL), lambda i: (i, 0)),
        out_shape=jax.ShapeDtypeStruct((m, D_MODEL), F32),
        **_call_params("out_proj", 1),
    )(merged, w_out, x, ln_g, ln_b)


def kernel(x_prompt, x_sample, mem_prompt, cache_mem_k, cache_mem_v, state_conv, state_lru_h, ffn1_w_gu, ffn1_w_down, ln1_g, ln1_b, w_in, gate_b, gmlp_ln_g, gmlp_ln_b, gmlp_w_s, gmlp_b_s, conv_w, conv_b, lru_w_a, lru_b_a, lru_w_x, lru_b_x, lru_lambda, mem_ln_g, mem_ln_b, w_mem_kv, w_branch, w_out, ln2_g, ln2_b, ffn2_w_gu, ffn2_w_down, ln3_g, ln3_b):
    bsz, seq, _ = x_prompt.shape
    n_s = x_sample.shape[0]
    l = 0

    wa, wx = lru_w_a[l].astype(BF16), lru_w_x[l].astype(BF16)

    row = lambda p: p[l].reshape(1, -1)
    gb = gate_b[l].reshape(1, -1)
    bs_mat = jnp.repeat(gmlp_b_s[l].T, GROUP_W, axis=1)
    wvec = jnp.repeat(gmlp_w_s[l][:, 0, 0], GROUP_W).reshape(1, -1)
    bvec = jnp.repeat(gmlp_b_s[l][:, 0], GROUP_W).reshape(1, -1)
    lru = (wa, row(lru_b_a), wx, row(lru_b_x), row(lru_lambda))

    xp = x_prompt.reshape(bsz * seq, D_MODEL)
    xs = x_sample.reshape(n_s, D_MODEL)

    x1s, w1g, w1u, w1d = _ffn(xs, ffn1_w_gu[l], ffn1_w_gu[l], ffn1_w_down[l], row(ln1_g), row(ln1_b),
                              tm=n_s, emit_bf16=True)
    k_p, v_p = _memkv(mem_prompt.reshape(bsz * N_MEM, D_MODEL), w_mem_kv[l], row(mem_ln_g), row(mem_ln_b))
    x1p, win, w2d = _ffn(xp, w1g, w1u, w1d, row(ln1_g), row(ln1_b), tm=1024, side=(w_in[l], ffn2_w_down[l]))

    zs, rxs = _inproj(x1s, win, gb, row(gmlp_ln_g), row(gmlp_ln_b), tm=n_s, z_dtype=F32)
    y01, conv_s, h_s = _mixer_sample(
        zs, rxs, jnp.swapaxes(state_conv[l], 0, 1), state_lru_h[l], wvec, bvec, conv_w[l], row(conv_b), *lru)
    q_s = zs[:, Z_Q * BR_W:(Z_Q + 1) * BR_W].reshape(n_s, XA_HEADS, XA_HEAD_DIM)
    yxa = _xattn_sample(_head_rows(q_s), _head_rows(cache_mem_k[l]), _head_rows(cache_mem_v[l]))
    yxa = jnp.swapaxes(yxa.reshape(n_s, HEAD_SPLIT, XA_HEADS, 128), 1, 2)
    ys = jnp.concatenate([y01, yxa.reshape(1, n_s, BR_W)], axis=0)

    zp, rxp, w2gu = _inproj(x1p, win, gb, row(gmlp_ln_g), row(gmlp_ln_b), tm=1024, z_dtype=BF16,
                            side=(ffn2_w_gu[l],))
    yp, conv_p, hlast_p, wbr, wout = _mixer_prompt(
        zp, rxp, k_p.reshape(bsz, N_MEM, BR_W), v_p.reshape(bsz, N_MEM, BR_W), gmlp_w_s[l], bs_mat,
        conv_w[l], row(conv_b), *lru, bsz=bsz, seq=seq,
        side=(w_branch[l].reshape(N_BRANCH * BR_W, D_MODEL), w_out[l]))
    wbr = wbr.reshape(N_BRANCH, BR_W, D_MODEL)

    ms = _merge(ys, wbr, zs, tm=n_s)
    x2s = _outproj(ms, wout, x1s, row(ln2_g), row(ln2_b), tm=n_s)
    x3s = _ffn(x2s, w2gu, w2gu, w2d, row(ln3_g), row(ln3_b), tm=n_s)
    mp = _merge(yp, wbr, zp, tm=1024)
    x2p = _outproj(mp, wout, x1p, row(ln2_g), row(ln2_b), tm=512)
    x3p = _ffn(x2p, w2gu, w2gu, w2d, row(ln3_g), row(ln3_b), tm=1024)

    kv_shape = (1, bsz, N_MEM, XA_HEADS, XA_HEAD_DIM)
    return (x3p.reshape(bsz, seq, D_MODEL),
            x3s.reshape(n_s, 1, D_MODEL),
            k_p.reshape(kv_shape),
            v_p.reshape(kv_shape),
            conv_p[None],
            hlast_p.reshape(1, bsz, BR_W),
            jnp.swapaxes(conv_s, 0, 1)[None],
            h_s[None],
            zs[:, Z_V * BR_W:(Z_V + 1) * BR_W].reshape(1, n_s, 1, BR_W))
```

```python
import functools

import jax
import jax.numpy as jnp
from jax import lax
from jax.experimental import pallas as pl
from jax.experimental.pallas import tpu as pltpu

F32 = jnp.float32
BF16 = jnp.bfloat16

D_MODEL = 2048
BR_W = 1024
D_FF = 5632
N_MEM = 256
XA_HEADS = 4
XA_HEAD_DIM = 256
GMLP_GROUPS = 4
GROUP_W = BR_W // GMLP_GROUPS
CHUNK = 128
LRU_BLOCKS = 8
LRU_BLOCK = 128
CONV_W = 4
LRU_C = 8.0
LN_EPS = 1e-5
ALPHA = 2.0 ** 0.25
IN_BLOCKS = 11
RX_BLOCK = 2
GATE_BLOCK0 = 5
Z_BLOCKS = IN_BLOCKS - 1
Z_GU, Z_V, Z_GRG, Z_Q, Z_GATE0 = 0, 1, 2, 3, 4
SUBLANES = 8
FFN_ROW_GROUP = 512
OUTPROJ_ROW_GROUP = 128
MiB = 1024 * 1024


VMEM_LIMIT_MIB = {
    "ffn_ln": 62,
    "mem_kv": 48,
    "in_proj": 60,
    "mixer_prompt": 48,
    "mixer_sample": 32,
    "xattn_sample": 48,
    "merge": 62,
    "out_proj": 56,
}


def _call_params(name, n_grid_axes):
    return dict(name=name, compiler_params=pltpu.CompilerParams(
        dimension_semantics=("arbitrary",) * n_grid_axes, vmem_limit_bytes=VMEM_LIMIT_MIB[name] * MiB))


def _layer_norm(x, g, b):
    mu = jnp.mean(x, axis=-1, keepdims=True)
    xc = x - mu
    var = jnp.mean(xc * xc, axis=-1, keepdims=True)
    return xc * lax.rsqrt(var + LN_EPS) * g + b


def _dot(a, b):
    return jnp.dot(a, b, preferred_element_type=F32)


def _sigmoid(x):
    return 0.5 * jnp.tanh(0.5 * x) + 0.5


def _mxu_operand(w_ref, w16_ref=None):
    w = w_ref[...].astype(BF16)
    if w16_ref is not None:
        w16_ref[...] = w
    return w


def _side_cast_plan(side, grid):
    n_steps = 1
    for g in grid:
        n_steps *= g

    def linear(*idx):
        lin = idx[0]
        for g, i in zip(grid[1:], idx[1:]):
            lin = lin * g + i
        return lin

    specs, shapes = [], []
    for w in side:
        r, c = w.shape
        if c % (n_steps * 128) == 0:
            spec = pl.BlockSpec((r, c // n_steps), lambda *idx: (0, linear(*idx)))
        else:
            assert r % (n_steps * 16) == 0, (w.shape, n_steps)
            spec = pl.BlockSpec((r // n_steps, c), lambda *idx: (linear(*idx), 0))
        specs.append(spec)
        shapes.append(jax.ShapeDtypeStruct(w.shape, BF16))
    return specs, shapes


def _side_cast(in_refs, out_refs):
    for src, dst in zip(in_refs, out_refs):
        dst[...] = src[...].astype(BF16)


RING = 3


def _ring_advance(step, n_steps, copies_of):
    for ahead in range(RING - 1):
        @pl.when((step == 0) & (ahead < n_steps))
        def _():
            for c in copies_of(step + ahead):
                c.start()

    @pl.when(step + (RING - 1) < n_steps)
    def _():
        for c in copies_of(step + (RING - 1)):
            c.start()

    for c in copies_of(step):
        c.wait()


def _ffn_kernel(x_ref, wg_ref, wu_ref, wd_ref, g_ref, b_ref, *rest, emit_bf16, row_splits, n_side):
    side_in, rest = rest[:n_side], rest[n_side:]
    o_ref, rest = rest[0], rest[1:]
    if emit_bf16:
        (wg16_ref, wu16_ref, wd16_ref), rest = rest[:3], rest[3:]
    else:
        wg16_ref = wu16_ref = wd16_ref = None
    side_out, (xb_ref,) = rest[:n_side], rest[n_side:]
    _side_cast(side_in, side_out)
    j = pl.program_id(1)
    last = pl.num_programs(1) - 1
    rows_per = xb_ref.shape[0] // row_splits

    def step(first, final):
        wg = _mxu_operand(wg_ref, wg16_ref)
        wu = _mxu_operand(wu_ref, wu16_ref)
        wd = _mxu_operand(wd_ref, wd16_ref)
        for r in range(row_splits):
            rows = slice(r * rows_per, (r + 1) * rows_per)
            if first:
                xb = x_ref[rows, :].astype(BF16)
                xb_ref[rows, :] = xb
            else:
                xb = xb_ref[rows, :]
            g = _dot(xb, wg)
            u = _dot(xb, wu)
            h = (g * _sigmoid(g) * u).astype(BF16)
            acc = _dot(h, wd)
            if not first:
                acc = o_ref[rows, :] + acc
            if final:
                acc = _layer_norm(ALPHA * x_ref[rows, :] + 0.5 * acc, g_ref[...], b_ref[...])
            o_ref[rows, :] = acc

    pl.when(j == 0)(functools.partial(step, True, False))
    pl.when((j > 0) & (j < last))(functools.partial(step, False, False))
    pl.when(j == last)(functools.partial(step, False, True))


def _ffn(x, w_g, w_u, w_down, ln_g, ln_b, *, tm, emit_bf16=False, side=(), tf=512):
    m = x.shape[0]
    nf = D_FF // tf
    grid = (m // tm, nf)
    u_off = nf if w_u.shape[1] == 2 * D_FF else 0
    assert not emit_bf16 or m == tm
    side_specs, side_shapes = _side_cast_plan(side, grid)
    out_specs = [pl.BlockSpec((tm, D_MODEL), lambda i, j: (i, 0))]
    out_shape = [jax.ShapeDtypeStruct((m, D_MODEL), F32)]
    if emit_bf16:
        out_specs += [pl.BlockSpec((D_MODEL, tf), lambda i, j: (0, j)),
                      pl.BlockSpec((D_MODEL, tf), lambda i, j: (0, j)),
                      pl.BlockSpec((tf, D_MODEL), lambda i, j: (j, 0))]
        out_shape += [jax.ShapeDtypeStruct((D_MODEL, D_FF), BF16),
                      jax.ShapeDtypeStruct((D_MODEL, D_FF), BF16),
                      jax.ShapeDtypeStruct((D_FF, D_MODEL), BF16)]
    outs = pl.pallas_call(
        functools.partial(_ffn_kernel, emit_bf16=emit_bf16, row_splits=max(1, tm // FFN_ROW_GROUP),
                          n_side=len(side)),
        grid=grid,
        in_specs=[
            pl.BlockSpec((tm, D_MODEL), lambda i, j: (i, 0)),
            pl.BlockSpec((D_MODEL, tf), lambda i, j: (0, j)),
            pl.BlockSpec((D_MODEL, tf), lambda i, j: (0, j + u_off)),
            pl.BlockSpec((tf, D_MODEL), lambda i, j: (j, 0)),
            pl.BlockSpec((1, D_MODEL), lambda i, j: (0, 0)),
            pl.BlockSpec((1, D_MODEL), lambda i, j: (0, 0)),
        ] + side_specs,
        out_specs=out_specs + side_specs,
        out_shape=out_shape + side_shapes,
        scratch_shapes=[pltpu.VMEM((tm, D_MODEL), BF16)],
        **_call_params("ffn_ln", 2),
    )(x, w_g, w_u, w_down, ln_g, ln_b, *side)
    return outs if (emit_bf16 or side) else outs[0]


def _memkv_kernel(x_ref, wk_ref, wv_ref, g_ref, b_ref, k_ref, v_ref, xb_ref):
    @pl.when(pl.program_id(1) == 0)
    def _():
        xb_ref[...] = _layer_norm(x_ref[...], g_ref[...], b_ref[...]).astype(BF16)

    xb = xb_ref[...]
    k_ref[...] = _dot(xb, _mxu_operand(wk_ref))
    v_ref[...] = _dot(xb, _mxu_operand(wv_ref))


def _memkv(mem, w_kv, ln_g, ln_b, *, tm=1024, tn=512):
    m = mem.shape[0]
    nn = BR_W // tn
    return pl.pallas_call(
        _memkv_kernel,
        grid=(m // tm, nn),
        in_specs=[
            pl.BlockSpec((tm, D_MODEL), lambda i, j: (i, 0)),
            pl.BlockSpec((D_MODEL, tn), lambda i, j: (0, j)),
            pl.BlockSpec((D_MODEL, tn), lambda i, j: (0, j + nn)),
            pl.BlockSpec((1, D_MODEL), lambda i, j: (0, 0)),
            pl.BlockSpec((1, D_MODEL), lambda i, j: (0, 0)),
        ],
        out_specs=[pl.BlockSpec((tm, tn), lambda i, j: (i, j)),
                   pl.BlockSpec((tm, tn), lambda i, j: (i, j))],
        out_shape=[jax.ShapeDtypeStruct((m, BR_W), F32), jax.ShapeDtypeStruct((m, BR_W), F32)],
        scratch_shapes=[pltpu.VMEM((tm, D_MODEL), BF16)],
        **_call_params("mem_kv", 2),
    )(mem, w_kv, w_kv, ln_g, ln_b)


def _inproj_kernel(x_ref, w_hbm, gb_ref, lg_ref, lb_ref, *rest, n_side):
    side_in, rest = rest[:n_side], rest[n_side:]
    (o_ref, rx_ref), rest = rest[:2], rest[2:]
    side_out, (xb_ref, wbuf_ref, wsem) = rest[:n_side], rest[n_side:]
    _side_cast(side_in, side_out)
    j = pl.program_id(1)
    n_blocks = pl.num_programs(1)
    step = pl.program_id(0) * n_blocks + j

    def weight_copy(s):
        col0 = pl.multiple_of((s % n_blocks) * BR_W, BR_W)
        return [pltpu.make_async_copy(w_hbm.at[:, pl.ds(col0, BR_W)], wbuf_ref.at[s % RING], wsem.at[s % RING])]

    _ring_advance(step, pl.num_programs(0) * n_blocks, weight_copy)
    w_ref = wbuf_ref.at[step % RING]

    @pl.when(j == 0)
    def _():
        xb_ref[...] = x_ref[...].astype(BF16)

    def emit(dst_ref, act):
        dst_ref[...] = act(_dot(xb_ref[...], w_ref[...])).astype(dst_ref.dtype)

    @pl.when((j == 0) | (j == 3))
    def _():
        emit(o_ref, jax.nn.gelu)

    @pl.when(j == 1)
    def _():
        emit(o_ref, lambda z: _layer_norm(jax.nn.gelu(z), lg_ref[...], lb_ref[...]))

    @pl.when(j == 2)
    def _():
        emit(rx_ref, lambda z: z)

    @pl.when(j == 4)
    def _():
        emit(o_ref, lambda z: z)

    @pl.when(j >= GATE_BLOCK0)
    def _():
        emit(o_ref, lambda z: _sigmoid(z + gb_ref[...]))


def _inproj(x, w_in, gate_b, gln_g, gln_b, *, tm, z_dtype, side=()):
    m = x.shape[0]
    grid = (m // tm, IN_BLOCKS)
    side_specs, side_shapes = _side_cast_plan(side, grid)
    out_specs = [pl.BlockSpec((tm, BR_W), lambda i, j: (i, jnp.where(j < RX_BLOCK, j, j - 1))),
                 pl.BlockSpec((tm, BR_W), lambda i, j: (i, 0))]
    out_shape = [jax.ShapeDtypeStruct((m, Z_BLOCKS * BR_W), z_dtype),
                 jax.ShapeDtypeStruct((m, BR_W), F32)]
    return pl.pallas_call(
        functools.partial(_inproj_kernel, n_side=len(side)),
        grid=grid,
        in_specs=[
            pl.BlockSpec((tm, D_MODEL), lambda i, j: (i, 0)),
            pl.BlockSpec(memory_space=pl.ANY),
            pl.BlockSpec((1, BR_W), lambda i, j: (0, jnp.maximum(j - GATE_BLOCK0, 0))),
            pl.BlockSpec((1, BR_W), lambda i, j: (0, 0)),
            pl.BlockSpec((1, BR_W), lambda i, j: (0, 0)),
        ] + side_specs,
        out_specs=out_specs + side_specs,
        out_shape=out_shape + side_shapes,
        scratch_shapes=[pltpu.VMEM((tm, D_MODEL), BF16),
                        pltpu.VMEM((RING, D_MODEL, BR_W), BF16),
                        pltpu.SemaphoreType.DMA((RING,))],
        **_call_params("in_proj", 2),
    )(x, w_in, gate_b, gln_g, gln_b, *side)


def _softplus(x):
    return jnp.maximum(x, 0.0) + jnp.log1p(jnp.exp(-jnp.abs(x)))


def _lru_coeffs(xc, wa_ref, ba_ref, wx_ref, bx_ref, lam_ref):
    xcb = xc.astype(BF16)
    r_parts, i_parts = [], []
    for k in range(LRU_BLOCKS):
        xk = xcb[:, k * LRU_BLOCK:(k + 1) * LRU_BLOCK]
        r_parts.append(_dot(xk, wa_ref[k]))
        i_parts.append(_dot(xk, wx_ref[k]))
    r = _sigmoid(jnp.concatenate(r_parts, axis=1) + ba_ref[...])
    i = _sigmoid(jnp.concatenate(i_parts, axis=1) + bx_ref[...])
    log_a = (-LRU_C) * r * _softplus(-lam_ref[...])
    a = jnp.exp(log_a)
    m = -jnp.tanh(log_a) * (a * a + 1.0)
    root = jnp.where(m > 0.0, m * lax.rsqrt(m), 0.0)
    return a, root * (i * xc)


def _mixer_kernel(gu_ref, v_ref, rx_ref, grg_ref, q_ref, k_ref, vm_ref,
                  ws_ref, bs_ref, cw_ref, cb_ref, wa_ref, ba_ref, wx_ref, bx_ref, lam_ref,
                  *rest, tm, n_side):
    side_in, rest = rest[:n_side], rest[n_side:]
    (y_ref, conv_ref, hlast_ref), rest = rest[:3], rest[3:]
    side_out, (xpad_ref, a_ref, b_ref, h_ref, hc_ref) = rest[:n_side], rest[n_side:]
    _side_cast(side_in, side_out)
    t = pl.program_id(1)

    tri = (lax.broadcasted_iota(jnp.int32, (CHUNK, CHUNK), 0)
           >= lax.broadcasted_iota(jnp.int32, (CHUNK, CHUNK), 1))
    for g in range(GMLP_GROUPS):
        wg = jnp.where(tri, ws_ref[g], 0.0).astype(BF16)
        cols = slice(g * GROUP_W, (g + 1) * GROUP_W)
        for c in range(tm // CHUNK):
            rows = slice(c * CHUNK, (c + 1) * CHUNK)
            s = _dot(wg, v_ref[rows, cols].astype(BF16)) + bs_ref[:, cols]
            y_ref[0, rows, cols] = (gu_ref[rows, cols] * s).astype(BF16)

    @pl.when(t == 0)
    def _():
        xpad_ref[0:SUBLANES, :] = jnp.zeros((SUBLANES, BR_W), F32)
        hc_ref[...] = jnp.zeros((1, BR_W), F32)

    @pl.when(t > 0)
    def _():
        xpad_ref[0:SUBLANES, :] = xpad_ref[tm:tm + SUBLANES, :]

    xpad_ref[SUBLANES:SUBLANES + tm, :] = rx_ref[...]
    xfull = xpad_ref[...]
    acc = xfull * cw_ref[0:1, :]
    for k in range(1, CONV_W):
        acc = pltpu.roll(acc, 1, 0) + xfull * cw_ref[k:k + 1, :]
    xc = acc[SUBLANES:, :] + cb_ref[...]

    a, b = _lru_coeffs(xc, wa_ref, ba_ref, wx_ref, bx_ref, lam_ref)
    a_ref[...] = a
    b_ref[...] = b

    row = lax.broadcasted_iota(jnp.int32, (SUBLANES, BR_W), 0)
    keep = [row >= d for d in (1, 2, 4)]

    def scan_block(blk, h):
        base = pl.multiple_of(blk * SUBLANES, SUBLANES)
        ca = a_ref[pl.ds(base, SUBLANES), :]
        cb = b_ref[pl.ds(base, SUBLANES), :]
        for d, kp in zip((1, 2, 4), keep):
            a_sh = jnp.where(kp, pltpu.roll(ca, d, 0), 1.0)
            b_sh = jnp.where(kp, pltpu.roll(cb, d, 0), 0.0)
            cb = ca * b_sh + cb
            ca = ca * a_sh
        hh = cb + ca * h
        h_ref[pl.ds(base, SUBLANES), :] = hh
        return hh[SUBLANES - 1:SUBLANES, :]

    h_end = lax.fori_loop(0, tm // SUBLANES, scan_block, hc_ref[...])
    hc_ref[...] = h_end
    y_ref[1] = (grg_ref[...] * h_ref[...]).astype(BF16)

    @pl.when(t == pl.num_programs(1) - 1)
    def _():
        conv_ref[0] = rx_ref[tm - (CONV_W - 1):tm, :]
        hlast_ref[0] = h_end

    scale = XA_HEAD_DIM ** -0.5
    for hd in range(XA_HEADS):
        cols = slice(hd * XA_HEAD_DIM, (hd + 1) * XA_HEAD_DIM)
        qh = q_ref[:, cols].astype(BF16)
        kh = k_ref[0, :, cols].astype(BF16)
        vh = vm_ref[0, :, cols].astype(BF16)
        s = lax.dot_general(qh, kh, (((1,), (1,)), ((), ())), preferred_element_type=F32) * scale
        e = jnp.exp(s - jnp.max(s, axis=-1, keepdims=True))
        p = e * (1.0 / jnp.sum(e, axis=-1, keepdims=True))
        y_ref[2, :, cols] = _dot(p.astype(BF16), vh).astype(BF16)


def _mixer_prompt(z, rx, k_mem, v_mem, ws, bs_mat, conv_w, conv_b, wa, ba, wx, bx, lam, *, bsz, seq, side=(),
                  tm=512):
    nt = seq // tm
    zspec = lambda c: pl.BlockSpec((tm, BR_W), lambda b, t, c=c: (b * nt + t, c))
    full = lambda shape: pl.BlockSpec(shape, lambda b, t: (0,) * len(shape))
    side_specs, side_shapes = _side_cast_plan(side, (bsz, nt))
    return pl.pallas_call(
        functools.partial(_mixer_kernel, tm=tm, n_side=len(side)),
        grid=(bsz, nt),
        in_specs=[zspec(Z_GU), zspec(Z_V), zspec(0), zspec(Z_GRG), zspec(Z_Q),
                  pl.BlockSpec((1, N_MEM, BR_W), lambda b, t: (b, 0, 0)),
                  pl.BlockSpec((1, N_MEM, BR_W), lambda b, t: (b, 0, 0)),
                  full((GMLP_GROUPS, CHUNK, CHUNK)), full((CHUNK, BR_W)),
                  full((CONV_W, BR_W)), full((1, BR_W)),
                  full((LRU_BLOCKS, LRU_BLOCK, LRU_BLOCK)), full((1, BR_W)),
                  full((LRU_BLOCKS, LRU_BLOCK, LRU_BLOCK)), full((1, BR_W)),
                  full((1, BR_W))] + side_specs,
        out_specs=[pl.BlockSpec((3, tm, BR_W), lambda b, t: (0, b * nt + t, 0)),
                   pl.BlockSpec((1, CONV_W - 1, BR_W), lambda b, t: (b, 0, 0)),
                   pl.BlockSpec((1, 1, BR_W), lambda b, t: (b, 0, 0))] + side_specs,
        out_shape=[jax.ShapeDtypeStruct((3, bsz * seq, BR_W), BF16),
                   jax.ShapeDtypeStruct((bsz, CONV_W - 1, BR_W), F32),
                   jax.ShapeDtypeStruct((bsz, 1, BR_W), F32)] + side_shapes,
        scratch_shapes=[pltpu.VMEM((tm + SUBLANES, BR_W), F32),
                        pltpu.VMEM((tm, BR_W), F32), pltpu.VMEM((tm, BR_W), F32),
                        pltpu.VMEM((tm, BR_W), F32), pltpu.VMEM((1, BR_W), F32)],
        **_call_params("mixer_prompt", 2),
    )(z, z, rx, z, z, k_mem, v_mem, ws, bs_mat, conv_w, conv_b, wa, ba, wx, bx, lam, *side)


def _mixer_sample_kernel(gu_ref, v_ref, rx_ref, grg_ref, cs_ref, h0_ref,
                         wv_ref, bv_ref, cw_ref, cb_ref, wa_ref, ba_ref, wx_ref, bx_ref, lam_ref,
                         y_ref, conv_ref, h_ref):
    y_ref[0] = (gu_ref[...] * (wv_ref[...] * v_ref[...] + bv_ref[...])).astype(BF16)

    rx = rx_ref[...]
    xc = cb_ref[...] + rx * cw_ref[CONV_W - 1:CONV_W, :]
    for k in range(CONV_W - 1):
        xc = xc + cs_ref[k] * cw_ref[k:k + 1, :]
    for k in range(CONV_W - 2):
        conv_ref[k] = cs_ref[k + 1]
    conv_ref[CONV_W - 2] = rx

    a, b = _lru_coeffs(xc, wa_ref, ba_ref, wx_ref, bx_ref, lam_ref)
    h = a * h0_ref[...] + b
    h_ref[...] = h
    y_ref[1] = (grg_ref[...] * h).astype(BF16)


def _mixer_sample(z, rx, conv_state, h0, wvec, bvec, conv_w, conv_b, wa, ba, wx, bx, lam):
    n = z.shape[0]
    zspec = lambda c: pl.BlockSpec((n, BR_W), lambda i, c=c: (0, c))
    full = lambda shape: pl.BlockSpec(shape, lambda i: (0,) * len(shape))
    return pl.pallas_call(
        _mixer_sample_kernel,
        grid=(1,),
        in_specs=[zspec(Z_GU), zspec(Z_V), zspec(0), zspec(Z_GRG),
                  full((CONV_W - 1, n, BR_W)), full((n, BR_W)),
                  full((1, BR_W)), full((1, BR_W)), full((CONV_W, BR_W)), full((1, BR_W)),
                  full((LRU_BLOCKS, LRU_BLOCK, LRU_BLOCK)), full((1, BR_W)),
                  full((LRU_BLOCKS, LRU_BLOCK, LRU_BLOCK)), full((1, BR_W)),
                  full((1, BR_W))],
        out_specs=[full((2, n, BR_W)), full((CONV_W - 1, n, BR_W)), full((n, BR_W))],
        out_shape=[jax.ShapeDtypeStruct((2, n, BR_W), BF16),
                   jax.ShapeDtypeStruct((CONV_W - 1, n, BR_W), F32),
                   jax.ShapeDtypeStruct((n, BR_W), F32)],
        **_call_params("mixer_sample", 1),
    )(z, z, rx, z, conv_state, h0, wvec, bvec, conv_w, conv_b, wa, ba, wx, bx, lam)


HEAD_SPLIT = XA_HEAD_DIM // 128
HEAD_ROWS = HEAD_SPLIT * XA_HEADS


def _head_rows(x):
    lead = x.shape[:-2]
    x = x.reshape(*lead, XA_HEADS, HEAD_SPLIT, 128)
    return jnp.swapaxes(x, -2, -3).reshape(*lead, HEAD_ROWS, 128)


def _xattn_sample_kernel(q_ref, k_ref, v_ref, o_ref, *, tb):
    scale = XA_HEAD_DIM ** -0.5
    n_rows = N_MEM * HEAD_ROWS
    lane = lax.broadcasted_iota(jnp.int32, (HEAD_ROWS, n_rows), 1)
    sub = lax.broadcasted_iota(jnp.int32, (HEAD_ROWS, n_rows), 0)
    own = (lane % HEAD_ROWS) == sub
    low_piece = (lax.broadcasted_iota(jnp.int32, (1, n_rows), 1) % HEAD_ROWS) < XA_HEADS

    def lane_group_reduce(x, op):
        shift = HEAD_ROWS
        while shift < 128:
            x = op(x, pltpu.roll(x, shift, 1))
            shift *= 2
        return x

    def per_kind(x, op):
        acc = x[:, 0:128]
        for c in range(1, n_rows // 128):
            acc = op(acc, x[:, c * 128:(c + 1) * 128])
        acc = lane_group_reduce(acc, op)
        return jnp.concatenate([acc] * (n_rows // 128), axis=1)

    parts = []
    for b in range(tb):
        qb = q_ref[b].astype(BF16)
        kb = k_ref[b].reshape(n_rows, 128).astype(BF16)
        c = lax.dot_general(qb, kb, (((1,), (1,)), ((), ())), preferred_element_type=F32)
        parts.append(jnp.sum(jnp.where(own, c, 0.0), axis=0, keepdims=True))
    part = jnp.concatenate(parts, axis=0)
    other = jnp.where(low_piece, pltpu.roll(part, n_rows - XA_HEADS, 1), pltpu.roll(part, XA_HEADS, 1))
    s = (part + other) * scale
    e = jnp.exp(s - per_kind(s, jnp.maximum))
    p = e * (1.0 / per_kind(e, jnp.add))
    for b in range(tb):
        vb = v_ref[b].reshape(n_rows, 128).astype(BF16)
        pm = jnp.where(own, jnp.broadcast_to(p[b:b + 1, :], (HEAD_ROWS, n_rows)), 0.0).astype(BF16)
        o_ref[b] = _dot(pm, vb).astype(BF16)


def _xattn_sample(q, k_cache, v_cache, *, tb=8):
    n = q.shape[0]
    kv_spec = pl.BlockSpec((tb, N_MEM, HEAD_ROWS, 128), lambda i: (i, 0, 0, 0))
    q_spec = pl.BlockSpec((tb, HEAD_ROWS, 128), lambda i: (i, 0, 0))
    return pl.pallas_call(
        functools.partial(_xattn_sample_kernel, tb=tb),
        grid=(n // tb,),
        in_specs=[q_spec, kv_spec, kv_spec],
        out_specs=q_spec,
        out_shape=jax.ShapeDtypeStruct((n, HEAD_ROWS, 128), BF16),
        **_call_params("xattn_sample", 1),
    )(q, k_cache, v_cache)


N_BRANCH = 3


def _merge_kernel(y_ref, wb_hbm, z_hbm, o_ref, wbuf_ref, gbuf_ref, wsem, gsem, *, tm):
    n_halves = pl.num_programs(1)
    step = pl.program_id(0) * n_halves + pl.program_id(1)

    def copies_of(s):
        slot, half = s % RING, s % n_halves
        row0 = pl.multiple_of((s // n_halves) * tm, tm)
        wcol0 = pl.multiple_of(half * BR_W, BR_W)
        copies = [pltpu.make_async_copy(wb_hbm.at[:, :, pl.ds(wcol0, BR_W)], wbuf_ref.at[slot], wsem.at[slot])]
        for k in range(N_BRANCH):
            gcol0 = pl.multiple_of((Z_GATE0 + 2 * k + half) * BR_W, BR_W)
            copies.append(pltpu.make_async_copy(z_hbm.at[pl.ds(row0, tm), pl.ds(gcol0, BR_W)],
                                                gbuf_ref.at[slot, k], gsem.at[slot, k]))
        return copies

    _ring_advance(step, pl.num_programs(0) * n_halves, copies_of)
    slot = step % RING
    acc = None
    for k in range(N_BRANCH):
        term = gbuf_ref[slot, k].astype(F32) * _dot(y_ref[k], wbuf_ref[slot, k])
        acc = term if acc is None else acc + term
    o_ref[...] = acc.astype(BF16)


def _merge(y, w_branch, z, *, tm):
    m = y.shape[1]
    return pl.pallas_call(
        functools.partial(_merge_kernel, tm=tm),
        grid=(m // tm, D_MODEL // BR_W),
        in_specs=[
            pl.BlockSpec((N_BRANCH, tm, BR_W), lambda i, n: (0, i, 0)),
            pl.BlockSpec(memory_space=pl.ANY),
            pl.BlockSpec(memory_space=pl.ANY),
        ],
        out_specs=pl.BlockSpec((tm, BR_W), lambda i, n: (i, n)),
        out_shape=jax.ShapeDtypeStruct((m, D_MODEL), BF16),
        scratch_shapes=[pltpu.VMEM((RING, N_BRANCH, BR_W, BR_W), BF16),
                        pltpu.VMEM((RING, N_BRANCH, tm, BR_W), z.dtype),
                        pltpu.SemaphoreType.DMA((RING,)),
                        pltpu.SemaphoreType.DMA((RING, N_BRANCH))],
        **_call_params("merge", 2),
    )(y, w_branch, z)


def _outproj_kernel(m_hbm, wo_ref, x_hbm, g_ref, b_ref, o_ref, mbuf_ref, xbuf_ref, msem, xsem, *, tm, row_splits):
    step = pl.program_id(0)

    def copies_of(s):
        slot = s % RING
        rows = pl.ds(pl.multiple_of(s * tm, tm), tm)
        return [pltpu.make_async_copy(m_hbm.at[rows, :], mbuf_ref.at[slot], msem.at[slot]),
                pltpu.make_async_copy(x_hbm.at[rows, :], xbuf_ref.at[slot], xsem.at[slot])]

    _ring_advance(step, pl.num_programs(0), copies_of)
    m_ref, x_ref = mbuf_ref.at[step % RING], xbuf_ref.at[step % RING]
    rows_per = tm // row_splits
    for r in range(row_splits):
        rows = slice(r * rows_per, (r + 1) * rows_per)
        y = ALPHA * x_ref[rows, :] + _dot(m_ref[rows, :], wo_ref[...])
        o_ref[rows, :] = _layer_norm(y, g_ref[...], b_ref[...])


def _outproj(merged, w_out, x, ln_g, ln_b, *, tm):
    m = x.shape[0]
    return pl.pallas_call(
        functools.partial(_outproj_kernel, tm=tm, row_splits=max(1, tm // OUTPROJ_ROW_GROUP)),
        grid=(m // tm,),
        in_specs=[
            pl.BlockSpec(memory_space=pl.ANY),
            pl.BlockSpec((D_MODEL, D_MODEL), lambda i: (0, 0)),
            pl.BlockSpec(memory_space=pl.ANY),
            pl.BlockSpec((1, D_MODEL), lambda i: (0, 0)),
            pl.BlockSpec((1, D_MODEL), lambda i: (0, 0)),
        ],
        out_specs=pl.BlockSpec((tm, D_MODEL), lambda i: (i, 0)),
        out_shape=jax.ShapeDtypeStruct((m, D_MODEL), F32),
        scratch_shapes=[pltpu.VMEM((RING, tm, D_MODEL), BF16),
                        pltpu.VMEM((RING, tm, D_MODEL), F32),
                        pltpu.SemaphoreType.DMA((RING,)),
                        pltpu.SemaphoreType.DMA((RING,))],
        **_call_params("out_proj", 1),
    )(merged, w_out, x, ln_g, ln_b)


def kernel(x_prompt, x_sample, mem_prompt, cache_mem_k, cache_mem_v, state_conv, state_lru_h, ffn1_w_gu, ffn1_w_down, ln1_g, ln1_b, w_in, gate_b, gmlp_ln_g, gmlp_ln_b, gmlp_w_s, gmlp_b_s, conv_w, conv_b, lru_w_a, lru_b_a, lru_w_x, lru_b_x, lru_lambda, mem_ln_g, mem_ln_b, w_mem_kv, w_branch, w_out, ln2_g, ln2_b, ffn2_w_gu, ffn2_w_down, ln3_g, ln3_b):
    bsz, seq, _ = x_prompt.shape
    n_s = x_sample.shape[0]
    l = 0

    wa, wx = lru_w_a[l].astype(BF16), lru_w_x[l].astype(BF16)

    row = lambda p: p[l].reshape(1, -1)
    gb = gate_b[l].reshape(1, -1)
    bs_mat = jnp.repeat(gmlp_b_s[l].T, GROUP_W, axis=1)
    wvec = jnp.repeat(gmlp_w_s[l][:, 0, 0], GROUP_W).reshape(1, -1)
    bvec = jnp.repeat(gmlp_b_s[l][:, 0], GROUP_W).reshape(1, -1)
    lru = (wa, row(lru_b_a), wx, row(lru_b_x), row(lru_lambda))

    xp = x_prompt.reshape(bsz * seq, D_MODEL)
    xs = x_sample.reshape(n_s, D_MODEL)

    x1s, w1g, w1u, w1d = _ffn(xs, ffn1_w_gu[l], ffn1_w_gu[l], ffn1_w_down[l], row(ln1_g), row(ln1_b),
                              tm=n_s, emit_bf16=True)
    k_p, v_p = _memkv(mem_prompt.reshape(bsz * N_MEM, D_MODEL), w_mem_kv[l], row(mem_ln_g), row(mem_ln_b))
    x1p, win, w2d = _ffn(xp, w1g, w1u, w1d, row(ln1_g), row(ln1_b), tm=1024, side=(w_in[l], ffn2_w_down[l]))

    zs, rxs = _inproj(x1s, win, gb, row(gmlp_ln_g), row(gmlp_ln_b), tm=n_s, z_dtype=F32)
    y01, conv_s, h_s = _mixer_sample(
        zs, rxs, jnp.swapaxes(state_conv[l], 0, 1), state_lru_h[l], wvec, bvec, conv_w[l], row(conv_b), *lru)
    q_s = zs[:, Z_Q * BR_W:(Z_Q + 1) * BR_W].reshape(n_s, XA_HEADS, XA_HEAD_DIM)
    yxa = _xattn_sample(_head_rows(q_s), _head_rows(cache_mem_k[l]), _head_rows(cache_mem_v[l]))
    yxa = jnp.swapaxes(yxa.reshape(n_s, HEAD_SPLIT, XA_HEADS, 128), 1, 2)
    ys = jnp.concatenate([y01, yxa.reshape(1, n_s, BR_W)], axis=0)

    zp, rxp, w2gu = _inproj(x1p, win, gb, row(gmlp_ln_g), row(gmlp_ln_b), tm=1024, z_dtype=BF16,
                            side=(ffn2_w_gu[l],))
    yp, conv_p, hlast_p, wbr, wout = _mixer_prompt(
        zp, rxp, k_p.reshape(bsz, N_MEM, BR_W), v_p.reshape(bsz, N_MEM, BR_W), gmlp_w_s[l], bs_mat,
        conv_w[l], row(conv_b), *lru, bsz=bsz, seq=seq,
        side=(w_branch[l].reshape(N_BRANCH * BR_W, D_MODEL), w_out[l]))
    wbr = wbr.reshape(N_BRANCH, BR_W, D_MODEL)

    ms = _merge(ys, wbr, zs, tm=n_s)
    x2s = _outproj(ms, wout, x1s, row(ln2_g), row(ln2_b), tm=n_s)
    x3s = _ffn(x2s, w2gu, w2gu, w2d, row(ln3_g), row(ln3_b), tm=n_s)
    mp = _merge(yp, wbr, zp, tm=1024)
    x2p = _outproj(mp, wout, x1p, row(ln2_g), row(ln2_b), tm=512)
    x3p = _ffn(x2p, w2gu, w2gu, w2d, row(ln3_g), row(ln3_b), tm=1024)

    kv_shape = (1, bsz, N_MEM, XA_HEADS, XA_HEAD_DIM)
    return (x3p.reshape(bsz, seq, D_MODEL),
            x3s.reshape(n_s, 1, D_MODEL),
            k_p.reshape(kv_shape),
            v_p.reshape(kv_shape),
            conv_p[None],
            hlast_p.reshape(1, bsz, BR_W),
            jnp.swapaxes(conv_s, 0, 1)[None],
            h_s[None],
            zs[:, Z_V * BR_W:(Z_V + 1) * BR_W].reshape(1, n_s, 1, BR_W))
```

```python
import functools

import jax
import jax.numpy as jnp
from jax import lax
from jax.experimental import pallas as pl
from jax.experimental.pallas import tpu as pltpu

F32 = jnp.float32
BF16 = jnp.bfloat16

D_MODEL = 2048
BR_W = 1024
D_FF = 5632
N_MEM = 256
XA_HEADS = 4
XA_HEAD_DIM = 256
GMLP_GROUPS = 4
GROUP_W = BR_W // GMLP_GROUPS
CHUNK = 128
LRU_BLOCKS = 8
LRU_BLOCK = 128
CONV_W = 4
LRU_C = 8.0
LN_EPS = 1e-5
ALPHA = 2.0 ** 0.25
IN_BLOCKS = 11
RX_BLOCK = 2
GATE_BLOCK0 = 5
Z_BLOCKS = IN_BLOCKS - 1
Z_GU, Z_V, Z_GRG, Z_Q, Z_GATE0 = 0, 1, 2, 3, 4
SUBLANES = 8
FFN_ROW_GROUP = 512
OUTPROJ_ROW_GROUP = 128
MiB = 1024 * 1024


VMEM_LIMIT_MIB = {
    "ffn_ln": 62,
    "mem_kv": 48,
    "in_proj": 60,
    "mixer_prompt": 48,
    "mixer_sample": 32,
    "xattn_sample": 48,
    "merge": 56,
    "out_proj": 48,
}


def _call_params(name, n_grid_axes):
    return dict(name=name, compiler_params=pltpu.CompilerParams(
        dimension_semantics=("arbitrary",) * n_grid_axes, vmem_limit_bytes=VMEM_LIMIT_MIB[name] * MiB))


def _layer_norm(x, g, b):
    mu = jnp.mean(x, axis=-1, keepdims=True)
    xc = x - mu
    var = jnp.mean(xc * xc, axis=-1, keepdims=True)
    return xc * lax.rsqrt(var + LN_EPS) * g + b


def _dot(a, b):
    return jnp.dot(a, b, preferred_element_type=F32)


def _sigmoid(x):
    return 0.5 * jnp.tanh(0.5 * x) + 0.5


def _mxu_operand(w_ref, w16_ref=None):
    w = w_ref[...].astype(BF16)
    if w16_ref is not None:
        w16_ref[...] = w
    return w


def _side_cast_plan(side, grid):
    n_steps = 1
    for g in grid:
        n_steps *= g

    def linear(*idx):
        lin = idx[0]
        for g, i in zip(grid[1:], idx[1:]):
            lin = lin * g + i
        return lin

    specs, shapes = [], []
    for w in side:
        r, c = w.shape
        if c % (n_steps * 128) == 0:
            spec = pl.BlockSpec((r, c // n_steps), lambda *idx: (0, linear(*idx)))
        else:
            assert r % (n_steps * 16) == 0, (w.shape, n_steps)
            spec = pl.BlockSpec((r // n_steps, c), lambda *idx: (linear(*idx), 0))
        specs.append(spec)
        shapes.append(jax.ShapeDtypeStruct(w.shape, BF16))
    return specs, shapes


def _side_cast(in_refs, out_refs):
    for src, dst in zip(in_refs, out_refs):
        dst[...] = src[...].astype(BF16)


def _ffn_kernel(x_ref, wg_ref, wu_ref, wd_ref, g_ref, b_ref, *rest, emit_bf16, row_splits, n_side):
    side_in, rest = rest[:n_side], rest[n_side:]
    o_ref, rest = rest[0], rest[1:]
    if emit_bf16:
        (wg16_ref, wu16_ref, wd16_ref), rest = rest[:3], rest[3:]
    else:
        wg16_ref = wu16_ref = wd16_ref = None
    side_out, (xb_ref,) = rest[:n_side], rest[n_side:]
    _side_cast(side_in, side_out)
    j = pl.program_id(1)
    last = pl.num_programs(1) - 1
    rows_per = xb_ref.shape[0] // row_splits

    def step(first, final):
        wg = _mxu_operand(wg_ref, wg16_ref)
        wu = _mxu_operand(wu_ref, wu16_ref)
        wd = _mxu_operand(wd_ref, wd16_ref)
        for r in range(row_splits):
            rows = slice(r * rows_per, (r + 1) * rows_per)
            if first:
                xb = x_ref[rows, :].astype(BF16)
                xb_ref[rows, :] = xb
            else:
                xb = xb_ref[rows, :]
            g = _dot(xb, wg)
            u = _dot(xb, wu)
            h = (g * _sigmoid(g) * u).astype(BF16)
            acc = _dot(h, wd)
            if not first:
                acc = o_ref[rows, :] + acc
            if final:
                acc = _layer_norm(ALPHA * x_ref[rows, :] + 0.5 * acc, g_ref[...], b_ref[...])
            o_ref[rows, :] = acc

    pl.when(j == 0)(functools.partial(step, True, False))
    pl.when((j > 0) & (j < last))(functools.partial(step, False, False))
    pl.when(j == last)(functools.partial(step, False, True))


def _ffn(x, w_g, w_u, w_down, ln_g, ln_b, *, tm, emit_bf16=False, side=(), tf=512):
    m = x.shape[0]
    nf = D_FF // tf
    grid = (m // tm, nf)
    u_off = nf if w_u.shape[1] == 2 * D_FF else 0
    assert not emit_bf16 or m == tm
    side_specs, side_shapes = _side_cast_plan(side, grid)
    out_specs = [pl.BlockSpec((tm, D_MODEL), lambda i, j: (i, 0))]
    out_shape = [jax.ShapeDtypeStruct((m, D_MODEL), F32)]
    if emit_bf16:
        out_specs += [pl.BlockSpec((D_MODEL, tf), lambda i, j: (0, j)),
                      pl.BlockSpec((D_MODEL, tf), lambda i, j: (0, j)),
                      pl.BlockSpec((tf, D_MODEL), lambda i, j: (j, 0))]
        out_shape += [jax.ShapeDtypeStruct((D_MODEL, D_FF), BF16),
                      jax.ShapeDtypeStruct((D_MODEL, D_FF), BF16),
                      jax.ShapeDtypeStruct((D_FF, D_MODEL), BF16)]
    outs = pl.pallas_call(
        functools.partial(_ffn_kernel, emit_bf16=emit_bf16, row_splits=max(1, tm // FFN_ROW_GROUP),
                          n_side=len(side)),
        grid=grid,
        in_specs=[
            pl.BlockSpec((tm, D_MODEL), lambda i, j: (i, 0)),
            pl.BlockSpec((D_MODEL, tf), lambda i, j: (0, j)),
            pl.BlockSpec((D_MODEL, tf), lambda i, j: (0, j + u_off)),
            pl.BlockSpec((tf, D_MODEL), lambda i, j: (j, 0)),
            pl.BlockSpec((1, D_MODEL), lambda i, j: (0, 0)),
            pl.BlockSpec((1, D_MODEL), lambda i, j: (0, 0)),
        ] + side_specs,
        out_specs=out_specs + side_specs,
        out_shape=out_shape + side_shapes,
        scratch_shapes=[pltpu.VMEM((tm, D_MODEL), BF16)],
        **_call_params("ffn_ln", 2),
    )(x, w_g, w_u, w_down, ln_g, ln_b, *side)
    return outs if (emit_bf16 or side) else outs[0]


def _memkv_kernel(x_ref, wk_ref, wv_ref, g_ref, b_ref, k_ref, v_ref, xb_ref):
    @pl.when(pl.program_id(1) == 0)
    def _():
        xb_ref[...] = _layer_norm(x_ref[...], g_ref[...], b_ref[...]).astype(BF16)

    xb = xb_ref[...]
    k_ref[...] = _dot(xb, _mxu_operand(wk_ref))
    v_ref[...] = _dot(xb, _mxu_operand(wv_ref))


def _memkv(mem, w_kv, ln_g, ln_b, *, tm=1024, tn=512):
    m = mem.shape[0]
    nn = BR_W // tn
    return pl.pallas_call(
        _memkv_kernel,
        grid=(m // tm, nn),
        in_specs=[
            pl.BlockSpec((tm, D_MODEL), lambda i, j: (i, 0)),
            pl.BlockSpec((D_MODEL, tn), lambda i, j: (0, j)),
            pl.BlockSpec((D_MODEL, tn), lambda i, j: (0, j + nn)),
            pl.BlockSpec((1, D_MODEL), lambda i, j: (0, 0)),
            pl.BlockSpec((1, D_MODEL), lambda i, j: (0, 0)),
        ],
        out_specs=[pl.BlockSpec((tm, tn), lambda i, j: (i, j)),
                   pl.BlockSpec((tm, tn), lambda i, j: (i, j))],
        out_shape=[jax.ShapeDtypeStruct((m, BR_W), F32), jax.ShapeDtypeStruct((m, BR_W), F32)],
        scratch_shapes=[pltpu.VMEM((tm, D_MODEL), BF16)],
        **_call_params("mem_kv", 2),
    )(mem, w_kv, w_kv, ln_g, ln_b)


W_RING = 3


def _inproj_kernel(x_ref, w_hbm, gb_ref, lg_ref, lb_ref, *rest, n_side):
    side_in, rest = rest[:n_side], rest[n_side:]
    (o_ref, rx_ref), rest = rest[:2], rest[2:]
    side_out, (xb_ref, wbuf_ref, wsem) = rest[:n_side], rest[n_side:]
    _side_cast(side_in, side_out)
    j = pl.program_id(1)
    n_blocks = pl.num_programs(1)
    step = pl.program_id(0) * n_blocks + j
    n_steps = pl.num_programs(0) * n_blocks

    def weight_copy(s):
        slot = s % W_RING
        col0 = pl.multiple_of((s % n_blocks) * BR_W, BR_W)
        return pltpu.make_async_copy(w_hbm.at[:, pl.ds(col0, BR_W)], wbuf_ref.at[slot], wsem.at[slot])

    @pl.when(step == 0)
    def _():
        for s in range(W_RING - 1):
            weight_copy(step + s).start()

    @pl.when(step + (W_RING - 1) < n_steps)
    def _():
        weight_copy(step + (W_RING - 1)).start()

    weight_copy(step).wait()
    w_ref = wbuf_ref.at[step % W_RING]

    @pl.when(j == 0)
    def _():
        xb_ref[...] = x_ref[...].astype(BF16)

    def emit(dst_ref, act):
        dst_ref[...] = act(_dot(xb_ref[...], w_ref[...])).astype(dst_ref.dtype)

    @pl.when((j == 0) | (j == 3))
    def _():
        emit(o_ref, jax.nn.gelu)

    @pl.when(j == 1)
    def _():
        emit(o_ref, lambda z: _layer_norm(jax.nn.gelu(z), lg_ref[...], lb_ref[...]))

    @pl.when(j == 2)
    def _():
        emit(rx_ref, lambda z: z)

    @pl.when(j == 4)
    def _():
        emit(o_ref, lambda z: z)

    @pl.when(j >= GATE_BLOCK0)
    def _():
        emit(o_ref, lambda z: _sigmoid(z + gb_ref[...]))


def _inproj(x, w_in, gate_b, gln_g, gln_b, *, tm, z_dtype, side=()):
    m = x.shape[0]
    grid = (m // tm, IN_BLOCKS)
    side_specs, side_shapes = _side_cast_plan(side, grid)
    out_specs = [pl.BlockSpec((tm, BR_W), lambda i, j: (i, jnp.where(j < RX_BLOCK, j, j - 1))),
                 pl.BlockSpec((tm, BR_W), lambda i, j: (i, 0))]
    out_shape = [jax.ShapeDtypeStruct((m, Z_BLOCKS * BR_W), z_dtype),
                 jax.ShapeDtypeStruct((m, BR_W), F32)]
    return pl.pallas_call(
        functools.partial(_inproj_kernel, n_side=len(side)),
        grid=grid,
        in_specs=[
            pl.BlockSpec((tm, D_MODEL), lambda i, j: (i, 0)),
            pl.BlockSpec(memory_space=pl.ANY),
            pl.BlockSpec((1, BR_W), lambda i, j: (0, jnp.maximum(j - GATE_BLOCK0, 0))),
            pl.BlockSpec((1, BR_W), lambda i, j: (0, 0)),
            pl.BlockSpec((1, BR_W), lambda i, j: (0, 0)),
        ] + side_specs,
        out_specs=out_specs + side_specs,
        out_shape=out_shape + side_shapes,
        scratch_shapes=[pltpu.VMEM((tm, D_MODEL), BF16),
                        pltpu.VMEM((W_RING, D_MODEL, BR_W), BF16),
                        pltpu.SemaphoreType.DMA((W_RING,))],
        **_call_params("in_proj", 2),
    )(x, w_in, gate_b, gln_g, gln_b, *side)


def _softplus(x):
    return jnp.maximum(x, 0.0) + jnp.log1p(jnp.exp(-jnp.abs(x)))


def _lru_coeffs(xc, wa_ref, ba_ref, wx_ref, bx_ref, lam_ref):
    xcb = xc.astype(BF16)
    r_parts, i_parts = [], []
    for k in range(LRU_BLOCKS):
        xk = xcb[:, k * LRU_BLOCK:(k + 1) * LRU_BLOCK]
        r_parts.append(_dot(xk, wa_ref[k]))
        i_parts.append(_dot(xk, wx_ref[k]))
    r = _sigmoid(jnp.concatenate(r_parts, axis=1) + ba_ref[...])
    i = _sigmoid(jnp.concatenate(i_parts, axis=1) + bx_ref[...])
    log_a = (-LRU_C) * r * _softplus(-lam_ref[...])
    a = jnp.exp(log_a)
    m = -jnp.tanh(log_a) * (a * a + 1.0)
    root = jnp.where(m > 0.0, m * lax.rsqrt(m), 0.0)
    return a, root * (i * xc)


def _mixer_kernel(gu_ref, v_ref, rx_ref, grg_ref, q_ref, k_ref, vm_ref,
                  ws_ref, bs_ref, cw_ref, cb_ref, wa_ref, ba_ref, wx_ref, bx_ref, lam_ref,
                  *rest, tm, n_side):
    side_in, rest = rest[:n_side], rest[n_side:]
    (y_ref, conv_ref, hlast_ref), rest = rest[:3], rest[3:]
    side_out, (xpad_ref, a_ref, b_ref, h_ref, hc_ref) = rest[:n_side], rest[n_side:]
    _side_cast(side_in, side_out)
    t = pl.program_id(1)

    tri = (lax.broadcasted_iota(jnp.int32, (CHUNK, CHUNK), 0)
           >= lax.broadcasted_iota(jnp.int32, (CHUNK, CHUNK), 1))
    for g in range(GMLP_GROUPS):
        wg = jnp.where(tri, ws_ref[g], 0.0).astype(BF16)
        cols = slice(g * GROUP_W, (g + 1) * GROUP_W)
        for c in range(tm // CHUNK):
            rows = slice(c * CHUNK, (c + 1) * CHUNK)
            s = _dot(wg, v_ref[rows, cols].astype(BF16)) + bs_ref[:, cols]
            y_ref[0, rows, cols] = (gu_ref[rows, cols] * s).astype(BF16)

    @pl.when(t == 0)
    def _():
        xpad_ref[0:SUBLANES, :] = jnp.zeros((SUBLANES, BR_W), F32)
        hc_ref[...] = jnp.zeros((1, BR_W), F32)

    @pl.when(t > 0)
    def _():
        xpad_ref[0:SUBLANES, :] = xpad_ref[tm:tm + SUBLANES, :]

    xpad_ref[SUBLANES:SUBLANES + tm, :] = rx_ref[...]
    xfull = xpad_ref[...]
    acc = xfull * cw_ref[0:1, :]
    for k in range(1, CONV_W):
        acc = pltpu.roll(acc, 1, 0) + xfull * cw_ref[k:k + 1, :]
    xc = acc[SUBLANES:, :] + cb_ref[...]

    a, b = _lru_coeffs(xc, wa_ref, ba_ref, wx_ref, bx_ref, lam_ref)
    a_ref[...] = a
    b_ref[...] = b

    row = lax.broadcasted_iota(jnp.int32, (SUBLANES, BR_W), 0)
    keep = [row >= d for d in (1, 2, 4)]

    def scan_block(blk, h):
        base = pl.multiple_of(blk * SUBLANES, SUBLANES)
        ca = a_ref[pl.ds(base, SUBLANES), :]
        cb = b_ref[pl.ds(base, SUBLANES), :]
        for d, kp in zip((1, 2, 4), keep):
            a_sh = jnp.where(kp, pltpu.roll(ca, d, 0), 1.0)
            b_sh = jnp.where(kp, pltpu.roll(cb, d, 0), 0.0)
            cb = ca * b_sh + cb
            ca = ca * a_sh
        hh = cb + ca * h
        h_ref[pl.ds(base, SUBLANES), :] = hh
        return hh[SUBLANES - 1:SUBLANES, :]

    h_end = lax.fori_loop(0, tm // SUBLANES, scan_block, hc_ref[...])
    hc_ref[...] = h_end
    y_ref[1] = (grg_ref[...] * h_ref[...]).astype(BF16)

    @pl.when(t == pl.num_programs(1) - 1)
    def _():
        conv_ref[0] = rx_ref[tm - (CONV_W - 1):tm, :]
        hlast_ref[0] = h_end

    scale = XA_HEAD_DIM ** -0.5
    for hd in range(XA_HEADS):
        cols = slice(hd * XA_HEAD_DIM, (hd + 1) * XA_HEAD_DIM)
        qh = q_ref[:, cols].astype(BF16)
        kh = k_ref[0, :, cols].astype(BF16)
        vh = vm_ref[0, :, cols].astype(BF16)
        s = lax.dot_general(qh, kh, (((1,), (1,)), ((), ())), preferred_element_type=F32) * scale
        e = jnp.exp(s - jnp.max(s, axis=-1, keepdims=True))
        p = e * (1.0 / jnp.sum(e, axis=-1, keepdims=True))
        y_ref[2, :, cols] = _dot(p.astype(BF16), vh).astype(BF16)


def _mixer_prompt(z, rx, k_mem, v_mem, ws, bs_mat, conv_w, conv_b, wa, ba, wx, bx, lam, *, bsz, seq, side=(),
                  tm=512):
    nt = seq // tm
    zspec = lambda c: pl.BlockSpec((tm, BR_W), lambda b, t, c=c: (b * nt + t, c))
    full = lambda shape: pl.BlockSpec(shape, lambda b, t: (0,) * len(shape))
    side_specs, side_shapes = _side_cast_plan(side, (bsz, nt))
    return pl.pallas_call(
        functools.partial(_mixer_kernel, tm=tm, n_side=len(side)),
        grid=(bsz, nt),
        in_specs=[zspec(Z_GU), zspec(Z_V), zspec(0), zspec(Z_GRG), zspec(Z_Q),
                  pl.BlockSpec((1, N_MEM, BR_W), lambda b, t: (b, 0, 0)),
                  pl.BlockSpec((1, N_MEM, BR_W), lambda b, t: (b, 0, 0)),
                  full((GMLP_GROUPS, CHUNK, CHUNK)), full((CHUNK, BR_W)),
                  full((CONV_W, BR_W)), full((1, BR_W)),
                  full((LRU_BLOCKS, LRU_BLOCK, LRU_BLOCK)), full((1, BR_W)),
                  full((LRU_BLOCKS, LRU_BLOCK, LRU_BLOCK)), full((1, BR_W)),
                  full((1, BR_W))] + side_specs,
        out_specs=[pl.BlockSpec((3, tm, BR_W), lambda b, t: (0, b * nt + t, 0)),
                   pl.BlockSpec((1, CONV_W - 1, BR_W), lambda b, t: (b, 0, 0)),
                   pl.BlockSpec((1, 1, BR_W), lambda b, t: (b, 0, 0))] + side_specs,
        out_shape=[jax.ShapeDtypeStruct((3, bsz * seq, BR_W), BF16),
                   jax.ShapeDtypeStruct((bsz, CONV_W - 1, BR_W), F32),
                   jax.ShapeDtypeStruct((bsz, 1, BR_W), F32)] + side_shapes,
        scratch_shapes=[pltpu.VMEM((tm + SUBLANES, BR_W), F32),
                        pltpu.VMEM((tm, BR_W), F32), pltpu.VMEM((tm, BR_W), F32),
                        pltpu.VMEM((tm, BR_W), F32), pltpu.VMEM((1, BR_W), F32)],
        **_call_params("mixer_prompt", 2),
    )(z, z, rx, z, z, k_mem, v_mem, ws, bs_mat, conv_w, conv_b, wa, ba, wx, bx, lam, *side)


def _mixer_sample_kernel(gu_ref, v_ref, rx_ref, grg_ref, cs_ref, h0_ref,
                         wv_ref, bv_ref, cw_ref, cb_ref, wa_ref, ba_ref, wx_ref, bx_ref, lam_ref,
                         y_ref, conv_ref, h_ref):
    y_ref[0] = (gu_ref[...] * (wv_ref[...] * v_ref[...] + bv_ref[...])).astype(BF16)

    rx = rx_ref[...]
    xc = cb_ref[...] + rx * cw_ref[CONV_W - 1:CONV_W, :]
    for k in range(CONV_W - 1):
        xc = xc + cs_ref[k] * cw_ref[k:k + 1, :]
    for k in range(CONV_W - 2):
        conv_ref[k] = cs_ref[k + 1]
    conv_ref[CONV_W - 2] = rx

    a, b = _lru_coeffs(xc, wa_ref, ba_ref, wx_ref, bx_ref, lam_ref)
    h = a * h0_ref[...] + b
    h_ref[...] = h
    y_ref[1] = (grg_ref[...] * h).astype(BF16)


def _mixer_sample(z, rx, conv_state, h0, wvec, bvec, conv_w, conv_b, wa, ba, wx, bx, lam):
    n = z.shape[0]
    zspec = lambda c: pl.BlockSpec((n, BR_W), lambda i, c=c: (0, c))
    full = lambda shape: pl.BlockSpec(shape, lambda i: (0,) * len(shape))
    return pl.pallas_call(
        _mixer_sample_kernel,
        grid=(1,),
        in_specs=[zspec(Z_GU), zspec(Z_V), zspec(0), zspec(Z_GRG),
                  full((CONV_W - 1, n, BR_W)), full((n, BR_W)),
                  full((1, BR_W)), full((1, BR_W)), full((CONV_W, BR_W)), full((1, BR_W)),
                  full((LRU_BLOCKS, LRU_BLOCK, LRU_BLOCK)), full((1, BR_W)),
                  full((LRU_BLOCKS, LRU_BLOCK, LRU_BLOCK)), full((1, BR_W)),
                  full((1, BR_W))],
        out_specs=[full((2, n, BR_W)), full((CONV_W - 1, n, BR_W)), full((n, BR_W))],
        out_shape=[jax.ShapeDtypeStruct((2, n, BR_W), BF16),
                   jax.ShapeDtypeStruct((CONV_W - 1, n, BR_W), F32),
                   jax.ShapeDtypeStruct((n, BR_W), F32)],
        **_call_params("mixer_sample", 1),
    )(z, z, rx, z, conv_state, h0, wvec, bvec, conv_w, conv_b, wa, ba, wx, bx, lam)


HEAD_SPLIT = XA_HEAD_DIM // 128
HEAD_ROWS = HEAD_SPLIT * XA_HEADS


def _head_rows(x):
    lead = x.shape[:-2]
    x = x.reshape(*lead, XA_HEADS, HEAD_SPLIT, 128)
    return jnp.swapaxes(x, -2, -3).reshape(*lead, HEAD_ROWS, 128)


def _xattn_sample_kernel(q_ref, k_ref, v_ref, o_ref, *, tb):
    scale = XA_HEAD_DIM ** -0.5
    n_rows = N_MEM * HEAD_ROWS
    lane = lax.broadcasted_iota(jnp.int32, (HEAD_ROWS, n_rows), 1)
    sub = lax.broadcasted_iota(jnp.int32, (HEAD_ROWS, n_rows), 0)
    own = (lane % HEAD_ROWS) == sub
    low_piece = (lax.broadcasted_iota(jnp.int32, (1, n_rows), 1) % HEAD_ROWS) < XA_HEADS

    def lane_group_reduce(x, op):
        shift = HEAD_ROWS
        while shift < 128:
            x = op(x, pltpu.roll(x, shift, 1))
            shift *= 2
        return x

    def per_kind(x, op):
        acc = x[:, 0:128]
        for c in range(1, n_rows // 128):
            acc = op(acc, x[:, c * 128:(c + 1) * 128])
        acc = lane_group_reduce(acc, op)
        return jnp.concatenate([acc] * (n_rows // 128), axis=1)

    parts = []
    for b in range(tb):
        qb = q_ref[b].astype(BF16)
        kb = k_ref[b].reshape(n_rows, 128).astype(BF16)
        c = lax.dot_general(qb, kb, (((1,), (1,)), ((), ())), preferred_element_type=F32)
        parts.append(jnp.sum(jnp.where(own, c, 0.0), axis=0, keepdims=True))
    part = jnp.concatenate(parts, axis=0)
    other = jnp.where(low_piece, pltpu.roll(part, n_rows - XA_HEADS, 1), pltpu.roll(part, XA_HEADS, 1))
    s = (part + other) * scale
    e = jnp.exp(s - per_kind(s, jnp.maximum))
    p = e * (1.0 / per_kind(e, jnp.add))
    for b in range(tb):
        vb = v_ref[b].reshape(n_rows, 128).astype(BF16)
        pm = jnp.where(own, jnp.broadcast_to(p[b:b + 1, :], (HEAD_ROWS, n_rows)), 0.0).astype(BF16)
        o_ref[b] = _dot(pm, vb).astype(BF16)


def _xattn_sample(q, k_cache, v_cache, *, tb=8):
    n = q.shape[0]
    kv_spec = pl.BlockSpec((tb, N_MEM, HEAD_ROWS, 128), lambda i: (i, 0, 0, 0))
    q_spec = pl.BlockSpec((tb, HEAD_ROWS, 128), lambda i: (i, 0, 0))
    return pl.pallas_call(
        functools.partial(_xattn_sample_kernel, tb=tb),
        grid=(n // tb,),
        in_specs=[q_spec, kv_spec, kv_spec],
        out_specs=q_spec,
        out_shape=jax.ShapeDtypeStruct((n, HEAD_ROWS, 128), BF16),
        **_call_params("xattn_sample", 1),
    )(q, k_cache, v_cache)


N_BRANCH = 3


def _merge_kernel(y_ref, wb_ref, g0_ref, g1_ref, g2_ref, o_ref):
    acc = None
    for k, gate_ref in enumerate((g0_ref, g1_ref, g2_ref)):
        term = gate_ref[...].astype(F32) * _dot(y_ref[k], wb_ref[k])
        acc = term if acc is None else acc + term
    o_ref[...] = acc.astype(BF16)


def _merge(y, w_branch, z, *, tm):
    m = y.shape[1]
    gate_spec = lambda k: pl.BlockSpec((tm, BR_W), lambda n, i, k=k: (i, Z_GATE0 + 2 * k + n))
    return pl.pallas_call(
        _merge_kernel,
        grid=(D_MODEL // BR_W, m // tm),
        in_specs=[
            pl.BlockSpec((N_BRANCH, tm, BR_W), lambda n, i: (0, i, 0)),
            pl.BlockSpec((N_BRANCH, BR_W, BR_W), lambda n, i: (0, 0, n)),
            gate_spec(0), gate_spec(1), gate_spec(2),
        ],
        out_specs=pl.BlockSpec((tm, BR_W), lambda n, i: (i, n)),
        out_shape=jax.ShapeDtypeStruct((m, D_MODEL), BF16),
        **_call_params("merge", 2),
    )(y, w_branch, z, z, z)


def _outproj_kernel(m_ref, wo_ref, x_ref, g_ref, b_ref, o_ref, *, row_splits):
    rows_per = m_ref.shape[0] // row_splits
    for r in range(row_splits):
        rows = slice(r * rows_per, (r + 1) * rows_per)
        y = ALPHA * x_ref[rows, :] + _dot(m_ref[rows, :], wo_ref[...])
        o_ref[rows, :] = _layer_norm(y, g_ref[...], b_ref[...])


def _outproj(merged, w_out, x, ln_g, ln_b, *, tm):
    m = x.shape[0]
    return pl.pallas_call(
        functools.partial(_outproj_kernel, row_splits=max(1, tm // OUTPROJ_ROW_GROUP)),
        grid=(m // tm,),
        in_specs=[
            pl.BlockSpec((tm, D_MODEL), lambda i: (i, 0)),
            pl.BlockSpec((D_MODEL, D_MODEL), lambda i: (0, 0)),
            pl.BlockSpec((tm, D_MODEL), lambda i: (i, 0)),
            pl.BlockSpec((1, D_MODEL), lambda i: (0, 0)),
            pl.BlockSpec((1, D_MODEL), lambda i: (0, 0)),
        ],
        out_specs=pl.BlockSpec((tm, D_MODEL), lambda i: (i, 0)),
        out_shape=jax.ShapeDtypeStruct((m, D_MODEL), F32),
        **_call_params("out_proj", 1),
    )(merged, w_out, x, ln_g, ln_b)


def kernel(x_prompt, x_sample, mem_prompt, cache_mem_k, cache_mem_v, state_conv, state_lru_h, ffn1_w_gu, ffn1_w_down, ln1_g, ln1_b, w_in, gate_b, gmlp_ln_g, gmlp_ln_b, gmlp_w_s, gmlp_b_s, conv_w, conv_b, lru_w_a, lru_b_a, lru_w_x, lru_b_x, lru_lambda, mem_ln_g, mem_ln_b, w_mem_kv, w_branch, w_out, ln2_g, ln2_b, ffn2_w_gu, ffn2_w_down, ln3_g, ln3_b):
    bsz, seq, _ = x_prompt.shape
    n_s = x_sample.shape[0]
    l = 0

    wa, wx = lru_w_a[l].astype(BF16), lru_w_x[l].astype(BF16)

    row = lambda p: p[l].reshape(1, -1)
    gb = gate_b[l].reshape(1, -1)
    bs_mat = jnp.repeat(gmlp_b_s[l].T, GROUP_W, axis=1)
    wvec = jnp.repeat(gmlp_w_s[l][:, 0, 0], GROUP_W).reshape(1, -1)
    bvec = jnp.repeat(gmlp_b_s[l][:, 0], GROUP_W).reshape(1, -1)
    lru = (wa, row(lru_b_a), wx, row(lru_b_x), row(lru_lambda))

    xp = x_prompt.reshape(bsz * seq, D_MODEL)
    xs = x_sample.reshape(n_s, D_MODEL)

    x1s, w1g, w1u, w1d = _ffn(xs, ffn1_w_gu[l], ffn1_w_gu[l], ffn1_w_down[l], row(ln1_g), row(ln1_b),
                              tm=n_s, emit_bf16=True)
    k_p, v_p = _memkv(mem_prompt.reshape(bsz * N_MEM, D_MODEL), w_mem_kv[l], row(mem_ln_g), row(mem_ln_b))
    x1p, win, w2d = _ffn(xp, w1g, w1u, w1d, row(ln1_g), row(ln1_b), tm=1024, side=(w_in[l], ffn2_w_down[l]))

    zs, rxs = _inproj(x1s, win, gb, row(gmlp_ln_g), row(gmlp_ln_b), tm=n_s, z_dtype=F32)
    y01, conv_s, h_s = _mixer_sample(
        zs, rxs, jnp.swapaxes(state_conv[l], 0, 1), state_lru_h[l], wvec, bvec, conv_w[l], row(conv_b), *lru)
    q_s = zs[:, Z_Q * BR_W:(Z_Q + 1) * BR_W].reshape(n_s, XA_HEADS, XA_HEAD_DIM)
    yxa = _xattn_sample(_head_rows(q_s), _head_rows(cache_mem_k[l]), _head_rows(cache_mem_v[l]))
    yxa = jnp.swapaxes(yxa.reshape(n_s, HEAD_SPLIT, XA_HEADS, 128), 1, 2)
    ys = jnp.concatenate([y01, yxa.reshape(1, n_s, BR_W)], axis=0)

    zp, rxp, w2gu = _inproj(x1p, win, gb, row(gmlp_ln_g), row(gmlp_ln_b), tm=1024, z_dtype=BF16,
                            side=(ffn2_w_gu[l],))
    yp, conv_p, hlast_p, wbr, wout = _mixer_prompt(
        zp, rxp, k_p.reshape(bsz, N_MEM, BR_W), v_p.reshape(bsz, N_MEM, BR_W), gmlp_w_s[l], bs_mat,
        conv_w[l], row(conv_b), *lru, bsz=bsz, seq=seq,
        side=(w_branch[l].reshape(N_BRANCH * BR_W, D_MODEL), w_out[l]))
    wbr = wbr.reshape(N_BRANCH, BR_W, D_MODEL)

    ms = _merge(ys, wbr, zs, tm=n_s)
    x2s = _outproj(ms, wout, x1s, row(ln2_g), row(ln2_b), tm=n_s)
    x3s = _ffn(x2s, w2gu, w2gu, w2d, row(ln3_g), row(ln3_b), tm=n_s)
    mp = _merge(yp, wbr, zp, tm=1024)
    x2p = _outproj(mp, wout, x1p, row(ln2_g), row(ln2_b), tm=512)
    x3p = _ffn(x2p, w2gu, w2gu, w2d, row(ln3_g), row(ln3_b), tm=1024)

    kv_shape = (1, bsz, N_MEM, XA_HEADS, XA_HEAD_DIM)
    return (x3p.reshape(bsz, seq, D_MODEL),
            x3s.reshape(n_s, 1, D_MODEL),
            k_p.reshape(kv_shape),
            v_p.reshape(kv_shape),
            conv_p[None],
            hlast_p.reshape(1, bsz, BR_W),
            jnp.swapaxes(conv_s, 0, 1)[None],
            h_s[None],
            zs[:, Z_V * BR_W:(Z_V + 1) * BR_W].reshape(1, n_s, 1, BR_W))
```

```python
import functools

import jax
import jax.numpy as jnp
from jax import lax
from jax.experimental import pallas as pl
from jax.experimental.pallas import tpu as pltpu

F32 = jnp.float32
BF16 = jnp.bfloat16

D_MODEL = 2048
BR_W = 1024
D_FF = 5632
N_MEM = 256
XA_HEADS = 4
XA_HEAD_DIM = 256
GMLP_GROUPS = 4
GROUP_W = BR_W // GMLP_GROUPS
CHUNK = 128
LRU_BLOCKS = 8
LRU_BLOCK = 128
CONV_W = 4
LRU_C = 8.0
LN_EPS = 1e-5
ALPHA = 2.0 ** 0.25
IN_BLOCKS = 11
RX_BLOCK = 2
GATE_BLOCK0 = 5
Z_BLOCKS = IN_BLOCKS - 1
Z_GU, Z_V, Z_GRG, Z_Q, Z_GATE0 = 0, 1, 2, 3, 4
SUBLANES = 8
FFN_ROW_GROUP = 512
OUTPROJ_ROW_GROUP = 128
MiB = 1024 * 1024


VMEM_LIMIT_MIB = {
    "ffn_ln": 62,
    "mem_kv": 56,
    "in_proj": 60,
    "mixer_prompt": 48,
    "mixer_sample": 32,
    "xattn_sample": 48,
    "merge": 56,
    "out_proj": 48,
}


def _call_params(name, n_grid_axes):
    return dict(name=name, compiler_params=pltpu.CompilerParams(
        dimension_semantics=("arbitrary",) * n_grid_axes, vmem_limit_bytes=VMEM_LIMIT_MIB[name] * MiB))


def _layer_norm(x, g, b):
    mu = jnp.mean(x, axis=-1, keepdims=True)
    xc = x - mu
    var = jnp.mean(xc * xc, axis=-1, keepdims=True)
    return xc * lax.rsqrt(var + LN_EPS) * g + b


def _dot(a, b):
    return jnp.dot(a, b, preferred_element_type=F32)


def _sigmoid(x):
    return 0.5 * jnp.tanh(0.5 * x) + 0.5


def _mxu_operand(w_ref, w16_ref=None):
    w = w_ref[...].astype(BF16)
    if w16_ref is not None:
        w16_ref[...] = w
    return w


def _side_cast_plan(side, grid):
    n_steps = 1
    for g in grid:
        n_steps *= g

    def linear(*idx):
        lin = idx[0]
        for g, i in zip(grid[1:], idx[1:]):
            lin = lin * g + i
        return lin

    specs, shapes = [], []
    for w in side:
        r, c = w.shape
        if c % (n_steps * 128) == 0:
            spec = pl.BlockSpec((r, c // n_steps), lambda *idx: (0, linear(*idx)))
        else:
            assert r % (n_steps * 16) == 0, (w.shape, n_steps)
            spec = pl.BlockSpec((r // n_steps, c), lambda *idx: (linear(*idx), 0))
        specs.append(spec)
        shapes.append(jax.ShapeDtypeStruct(w.shape, BF16))
    return specs, shapes


def _side_cast(in_refs, out_refs):
    for src, dst in zip(in_refs, out_refs):
        dst[...] = src[...].astype(BF16)


def _ffn_kernel(x_ref, wg_ref, wu_ref, wd_ref, g_ref, b_ref, *rest, emit_bf16, row_splits, n_side):
    side_in, rest = rest[:n_side], rest[n_side:]
    o_ref, rest = rest[0], rest[1:]
    if emit_bf16:
        (wg16_ref, wu16_ref, wd16_ref), rest = rest[:3], rest[3:]
    else:
        wg16_ref = wu16_ref = wd16_ref = None
    side_out, (xb_ref,) = rest[:n_side], rest[n_side:]
    _side_cast(side_in, side_out)
    j = pl.program_id(1)
    last = pl.num_programs(1) - 1
    rows_per = xb_ref.shape[0] // row_splits

    def step(first, final):
        wg = _mxu_operand(wg_ref, wg16_ref)
        wu = _mxu_operand(wu_ref, wu16_ref)
        wd = _mxu_operand(wd_ref, wd16_ref)
        for r in range(row_splits):
            rows = slice(r * rows_per, (r + 1) * rows_per)
            if first:
                xb = x_ref[rows, :].astype(BF16)
                xb_ref[rows, :] = xb
            else:
                xb = xb_ref[rows, :]
            g = _dot(xb, wg)
            u = _dot(xb, wu)
            h = (g * _sigmoid(g) * u).astype(BF16)
            acc = _dot(h, wd)
            if not first:
                acc = o_ref[rows, :] + acc
            if final:
                acc = _layer_norm(ALPHA * x_ref[rows, :] + 0.5 * acc, g_ref[...], b_ref[...])
            o_ref[rows, :] = acc

    pl.when(j == 0)(functools.partial(step, True, False))
    pl.when((j > 0) & (j < last))(functools.partial(step, False, False))
    pl.when(j == last)(functools.partial(step, False, True))


def _ffn(x, w_g, w_u, w_down, ln_g, ln_b, *, tm, emit_bf16=False, side=(), tf=512):
    m = x.shape[0]
    nf = D_FF // tf
    grid = (m // tm, nf)
    u_off = nf if w_u.shape[1] == 2 * D_FF else 0
    assert not emit_bf16 or m == tm
    side_specs, side_shapes = _side_cast_plan(side, grid)
    out_specs = [pl.BlockSpec((tm, D_MODEL), lambda i, j: (i, 0))]
    out_shape = [jax.ShapeDtypeStruct((m, D_MODEL), F32)]
    if emit_bf16:
        out_specs += [pl.BlockSpec((D_MODEL, tf), lambda i, j: (0, j)),
                      pl.BlockSpec((D_MODEL, tf), lambda i, j: (0, j)),
                      pl.BlockSpec((tf, D_MODEL), lambda i, j: (j, 0))]
        out_shape += [jax.ShapeDtypeStruct((D_MODEL, D_FF), BF16),
                      jax.ShapeDtypeStruct((D_MODEL, D_FF), BF16),
                      jax.ShapeDtypeStruct((D_FF, D_MODEL), BF16)]
    outs = pl.pallas_call(
        functools.partial(_ffn_kernel, emit_bf16=emit_bf16, row_splits=max(1, tm // FFN_ROW_GROUP),
                          n_side=len(side)),
        grid=grid,
        in_specs=[
            pl.BlockSpec((tm, D_MODEL), lambda i, j: (i, 0)),
            pl.BlockSpec((D_MODEL, tf), lambda i, j: (0, j)),
            pl.BlockSpec((D_MODEL, tf), lambda i, j: (0, j + u_off)),
            pl.BlockSpec((tf, D_MODEL), lambda i, j: (j, 0)),
            pl.BlockSpec((1, D_MODEL), lambda i, j: (0, 0)),
            pl.BlockSpec((1, D_MODEL), lambda i, j: (0, 0)),
        ] + side_specs,
        out_specs=out_specs + side_specs,
        out_shape=out_shape + side_shapes,
        scratch_shapes=[pltpu.VMEM((tm, D_MODEL), BF16)],
        **_call_params("ffn_ln", 2),
    )(x, w_g, w_u, w_down, ln_g, ln_b, *side)
    return outs if (emit_bf16 or side) else outs[0]


def _memkv_kernel(x_ref, wk_ref, wv_ref, g_ref, b_ref, k_ref, v_ref, k16_ref, v16_ref, xb_ref):
    @pl.when(pl.program_id(1) == 0)
    def _():
        xb_ref[...] = _layer_norm(x_ref[...], g_ref[...], b_ref[...]).astype(BF16)

    xb = xb_ref[...]
    k = _dot(xb, _mxu_operand(wk_ref))
    v = _dot(xb, _mxu_operand(wv_ref))
    k_ref[...] = k
    v_ref[...] = v
    k16_ref[...] = k.astype(BF16)
    v16_ref[...] = v.astype(BF16)


def _memkv(mem, w_kv, ln_g, ln_b, *, tm=1024, tn=512):
    m = mem.shape[0]
    nn = BR_W // tn
    return pl.pallas_call(
        _memkv_kernel,
        grid=(m // tm, nn),
        in_specs=[
            pl.BlockSpec((tm, D_MODEL), lambda i, j: (i, 0)),
            pl.BlockSpec((D_MODEL, tn), lambda i, j: (0, j)),
            pl.BlockSpec((D_MODEL, tn), lambda i, j: (0, j + nn)),
            pl.BlockSpec((1, D_MODEL), lambda i, j: (0, 0)),
            pl.BlockSpec((1, D_MODEL), lambda i, j: (0, 0)),
        ],
        out_specs=[pl.BlockSpec((tm, tn), lambda i, j: (i, j))] * 4,
        out_shape=[jax.ShapeDtypeStruct((m, BR_W), F32), jax.ShapeDtypeStruct((m, BR_W), F32),
                   jax.ShapeDtypeStruct((m, BR_W), BF16), jax.ShapeDtypeStruct((m, BR_W), BF16)],
        scratch_shapes=[pltpu.VMEM((tm, D_MODEL), BF16)],
        **_call_params("mem_kv", 2),
    )(mem, w_kv, w_kv, ln_g, ln_b)


W_RING = 3


def _inproj_kernel(x_ref, w_hbm, gb_ref, lg_ref, lb_ref, *rest, n_side):
    side_in, rest = rest[:n_side], rest[n_side:]
    (o_ref, rx_ref), rest = rest[:2], rest[2:]
    side_out, (xb_ref, wbuf_ref, wsem) = rest[:n_side], rest[n_side:]
    _side_cast(side_in, side_out)
    j = pl.program_id(1)
    n_blocks = pl.num_programs(1)
    step = pl.program_id(0) * n_blocks + j
    n_steps = pl.num_programs(0) * n_blocks

    def weight_copy(s):
        slot = s % W_RING
        col0 = pl.multiple_of((s % n_blocks) * BR_W, BR_W)
        return pltpu.make_async_copy(w_hbm.at[:, pl.ds(col0, BR_W)], wbuf_ref.at[slot], wsem.at[slot])

    @pl.when(step == 0)
    def _():
        for s in range(W_RING - 1):
            weight_copy(step + s).start()

    @pl.when(step + (W_RING - 1) < n_steps)
    def _():
        weight_copy(step + (W_RING - 1)).start()

    weight_copy(step).wait()
    w_ref = wbuf_ref.at[step % W_RING]

    @pl.when(j == 0)
    def _():
        xb_ref[...] = x_ref[...].astype(BF16)

    def emit(dst_ref, act):
        dst_ref[...] = act(_dot(xb_ref[...], w_ref[...])).astype(dst_ref.dtype)

    @pl.when((j == 0) | (j == 3))
    def _():
        emit(o_ref, jax.nn.gelu)

    @pl.when(j == 1)
    def _():
        emit(o_ref, lambda z: _layer_norm(jax.nn.gelu(z), lg_ref[...], lb_ref[...]))

    @pl.when(j == 2)
    def _():
        emit(rx_ref, lambda z: z)

    @pl.when(j == 4)
    def _():
        emit(o_ref, lambda z: z)

    @pl.when(j >= GATE_BLOCK0)
    def _():
        emit(o_ref, lambda z: _sigmoid(z + gb_ref[...]))


def _inproj(x, w_in, gate_b, gln_g, gln_b, *, tm, z_dtype, side=()):
    m = x.shape[0]
    grid = (m // tm, IN_BLOCKS)
    side_specs, side_shapes = _side_cast_plan(side, grid)
    out_specs = [pl.BlockSpec((tm, BR_W), lambda i, j: (i, jnp.where(j < RX_BLOCK, j, j - 1))),
                 pl.BlockSpec((tm, BR_W), lambda i, j: (i, 0))]
    out_shape = [jax.ShapeDtypeStruct((m, Z_BLOCKS * BR_W), z_dtype),
                 jax.ShapeDtypeStruct((m, BR_W), F32)]
    return pl.pallas_call(
        functools.partial(_inproj_kernel, n_side=len(side)),
        grid=grid,
        in_specs=[
            pl.BlockSpec((tm, D_MODEL), lambda i, j: (i, 0)),
            pl.BlockSpec(memory_space=pl.ANY),
            pl.BlockSpec((1, BR_W), lambda i, j: (0, jnp.maximum(j - GATE_BLOCK0, 0))),
            pl.BlockSpec((1, BR_W), lambda i, j: (0, 0)),
            pl.BlockSpec((1, BR_W), lambda i, j: (0, 0)),
        ] + side_specs,
        out_specs=out_specs + side_specs,
        out_shape=out_shape + side_shapes,
        scratch_shapes=[pltpu.VMEM((tm, D_MODEL), BF16),
                        pltpu.VMEM((W_RING, D_MODEL, BR_W), BF16),
                        pltpu.SemaphoreType.DMA((W_RING,))],
        **_call_params("in_proj", 2),
    )(x, w_in, gate_b, gln_g, gln_b, *side)


def _softplus(x):
    return jnp.maximum(x, 0.0) + jnp.log1p(jnp.exp(-jnp.abs(x)))


def _lru_coeffs(xc, wa_ref, ba_ref, wx_ref, bx_ref, lam_ref):
    xcb = xc.astype(BF16)
    r_parts, i_parts = [], []
    for k in range(LRU_BLOCKS):
        xk = xcb[:, k * LRU_BLOCK:(k + 1) * LRU_BLOCK]
        r_parts.append(_dot(xk, wa_ref[k]))
        i_parts.append(_dot(xk, wx_ref[k]))
    r = _sigmoid(jnp.concatenate(r_parts, axis=1) + ba_ref[...])
    i = _sigmoid(jnp.concatenate(i_parts, axis=1) + bx_ref[...])
    log_a = (-LRU_C) * r * _softplus(-lam_ref[...])
    a = jnp.exp(log_a)
    m = -jnp.tanh(log_a) * (a * a + 1.0)
    root = jnp.where(m > 0.0, m * lax.rsqrt(m), 0.0)
    return a, root * (i * xc)


def _mixer_kernel(gu_ref, v_ref, rx_ref, grg_ref, q_ref, k_ref, vm_ref,
                  ws_ref, bs_ref, cw_ref, cb_ref, wa_ref, ba_ref, wx_ref, bx_ref, lam_ref,
                  *rest, tm, n_side):
    side_in, rest = rest[:n_side], rest[n_side:]
    (y_ref, conv_ref, hlast_ref), rest = rest[:3], rest[3:]
    side_out, (xpad_ref, a_ref, b_ref, h_ref, hc_ref) = rest[:n_side], rest[n_side:]
    _side_cast(side_in, side_out)
    t = pl.program_id(1)

    tri = (lax.broadcasted_iota(jnp.int32, (CHUNK, CHUNK), 0)
           >= lax.broadcasted_iota(jnp.int32, (CHUNK, CHUNK), 1))
    for g in range(GMLP_GROUPS):
        wg = jnp.where(tri, ws_ref[g], 0.0).astype(BF16)
        cols = slice(g * GROUP_W, (g + 1) * GROUP_W)
        for c in range(tm // CHUNK):
            rows = slice(c * CHUNK, (c + 1) * CHUNK)
            s = _dot(wg, v_ref[rows, cols].astype(BF16)) + bs_ref[:, cols]
            y_ref[0, rows, cols] = (gu_ref[rows, cols] * s).astype(BF16)

    @pl.when(t == 0)
    def _():
        xpad_ref[0:SUBLANES, :] = jnp.zeros((SUBLANES, BR_W), F32)
        hc_ref[...] = jnp.zeros((1, BR_W), F32)

    @pl.when(t > 0)
    def _():
        xpad_ref[0:SUBLANES, :] = xpad_ref[tm:tm + SUBLANES, :]

    xpad_ref[SUBLANES:SUBLANES + tm, :] = rx_ref[...]
    xfull = xpad_ref[...]
    acc = xfull * cw_ref[0:1, :]
    for k in range(1, CONV_W):
        acc = pltpu.roll(acc, 1, 0) + xfull * cw_ref[k:k + 1, :]
    xc = acc[SUBLANES:, :] + cb_ref[...]

    a, b = _lru_coeffs(xc, wa_ref, ba_ref, wx_ref, bx_ref, lam_ref)
    a_ref[...] = a
    b_ref[...] = b

    row = lax.broadcasted_iota(jnp.int32, (SUBLANES, BR_W), 0)
    keep = [row >= d for d in (1, 2, 4)]

    def scan_block(blk, h):
        base = pl.multiple_of(blk * SUBLANES, SUBLANES)
        ca = a_ref[pl.ds(base, SUBLANES), :]
        cb = b_ref[pl.ds(base, SUBLANES), :]
        for d, kp in zip((1, 2, 4), keep):
            a_sh = jnp.where(kp, pltpu.roll(ca, d, 0), 1.0)
            b_sh = jnp.where(kp, pltpu.roll(cb, d, 0), 0.0)
            cb = ca * b_sh + cb
            ca = ca * a_sh
        hh = cb + ca * h
        h_ref[pl.ds(base, SUBLANES), :] = hh
        return hh[SUBLANES - 1:SUBLANES, :]

    h_end = lax.fori_loop(0, tm // SUBLANES, scan_block, hc_ref[...])
    hc_ref[...] = h_end
    y_ref[1] = (grg_ref[...] * h_ref[...]).astype(BF16)

    @pl.when(t == pl.num_programs(1) - 1)
    def _():
        conv_ref[0] = rx_ref[tm - (CONV_W - 1):tm, :]
        hlast_ref[0] = h_end

    scale = XA_HEAD_DIM ** -0.5
    for hd in range(XA_HEADS):
        cols = slice(hd * XA_HEAD_DIM, (hd + 1) * XA_HEAD_DIM)
        qh = q_ref[:, cols].astype(BF16)
        kh = k_ref[0, :, cols]
        vh = vm_ref[0, :, cols]
        s = lax.dot_general(qh, kh, (((1,), (1,)), ((), ())), preferred_element_type=F32) * scale
        e = jnp.exp(s - jnp.max(s, axis=-1, keepdims=True))
        p = e * (1.0 / jnp.sum(e, axis=-1, keepdims=True))
        y_ref[2, :, cols] = _dot(p.astype(BF16), vh).astype(BF16)


def _mixer_prompt(z, rx, k_mem, v_mem, ws, bs_mat, conv_w, conv_b, wa, ba, wx, bx, lam, *, bsz, seq, side=(),
                  tm=512):
    nt = seq // tm
    zspec = lambda c: pl.BlockSpec((tm, BR_W), lambda b, t, c=c: (b * nt + t, c))
    full = lambda shape: pl.BlockSpec(shape, lambda b, t: (0,) * len(shape))
    side_specs, side_shapes = _side_cast_plan(side, (bsz, nt))
    return pl.pallas_call(
        functools.partial(_mixer_kernel, tm=tm, n_side=len(side)),
        grid=(bsz, nt),
        in_specs=[zspec(Z_GU), zspec(Z_V), zspec(0), zspec(Z_GRG), zspec(Z_Q),
                  pl.BlockSpec((1, N_MEM, BR_W), lambda b, t: (b, 0, 0)),
                  pl.BlockSpec((1, N_MEM, BR_W), lambda b, t: (b, 0, 0)),
                  full((GMLP_GROUPS, CHUNK, CHUNK)), full((CHUNK, BR_W)),
                  full((CONV_W, BR_W)), full((1, BR_W)),
                  full((LRU_BLOCKS, LRU_BLOCK, LRU_BLOCK)), full((1, BR_W)),
                  full((LRU_BLOCKS, LRU_BLOCK, LRU_BLOCK)), full((1, BR_W)),
                  full((1, BR_W))] + side_specs,
        out_specs=[pl.BlockSpec((3, tm, BR_W), lambda b, t: (0, b * nt + t, 0)),
                   pl.BlockSpec((1, CONV_W - 1, BR_W), lambda b, t: (b, 0, 0)),
                   pl.BlockSpec((1, 1, BR_W), lambda b, t: (b, 0, 0))] + side_specs,
        out_shape=[jax.ShapeDtypeStruct((3, bsz * seq, BR_W), BF16),
                   jax.ShapeDtypeStruct((bsz, CONV_W - 1, BR_W), F32),
                   jax.ShapeDtypeStruct((bsz, 1, BR_W), F32)] + side_shapes,
        scratch_shapes=[pltpu.VMEM((tm + SUBLANES, BR_W), F32),
                        pltpu.VMEM((tm, BR_W), F32), pltpu.VMEM((tm, BR_W), F32),
                        pltpu.VMEM((tm, BR_W), F32), pltpu.VMEM((1, BR_W), F32)],
        **_call_params("mixer_prompt", 2),
    )(z, z, rx, z, z, k_mem, v_mem, ws, bs_mat, conv_w, conv_b, wa, ba, wx, bx, lam, *side)


def _mixer_sample_kernel(gu_ref, v_ref, rx_ref, grg_ref, cs_ref, h0_ref,
                         wv_ref, bv_ref, cw_ref, cb_ref, wa_ref, ba_ref, wx_ref, bx_ref, lam_ref,
                         y_ref, conv_ref, h_ref):
    y_ref[0] = (gu_ref[...] * (wv_ref[...] * v_ref[...] + bv_ref[...])).astype(BF16)

    rx = rx_ref[...]
    xc = cb_ref[...] + rx * cw_ref[CONV_W - 1:CONV_W, :]
    for k in range(CONV_W - 1):
        xc = xc + cs_ref[k] * cw_ref[k:k + 1, :]
    for k in range(CONV_W - 2):
        conv_ref[k] = cs_ref[k + 1]
    conv_ref[CONV_W - 2] = rx

    a, b = _lru_coeffs(xc, wa_ref, ba_ref, wx_ref, bx_ref, lam_ref)
    h = a * h0_ref[...] + b
    h_ref[...] = h
    y_ref[1] = (grg_ref[...] * h).astype(BF16)


def _mixer_sample(z, rx, conv_state, h0, wvec, bvec, conv_w, conv_b, wa, ba, wx, bx, lam):
    n = z.shape[0]
    zspec = lambda c: pl.BlockSpec((n, BR_W), lambda i, c=c: (0, c))
    full = lambda shape: pl.BlockSpec(shape, lambda i: (0,) * len(shape))
    return pl.pallas_call(
        _mixer_sample_kernel,
        grid=(1,),
        in_specs=[zspec(Z_GU), zspec(Z_V), zspec(0), zspec(Z_GRG),
                  full((CONV_W - 1, n, BR_W)), full((n, BR_W)),
                  full((1, BR_W)), full((1, BR_W)), full((CONV_W, BR_W)), full((1, BR_W)),
                  full((LRU_BLOCKS, LRU_BLOCK, LRU_BLOCK)), full((1, BR_W)),
                  full((LRU_BLOCKS, LRU_BLOCK, LRU_BLOCK)), full((1, BR_W)),
                  full((1, BR_W))],
        out_specs=[full((2, n, BR_W)), full((CONV_W - 1, n, BR_W)), full((n, BR_W))],
        out_shape=[jax.ShapeDtypeStruct((2, n, BR_W), BF16),
                   jax.ShapeDtypeStruct((CONV_W - 1, n, BR_W), F32),
                   jax.ShapeDtypeStruct((n, BR_W), F32)],
        **_call_params("mixer_sample", 1),
    )(z, z, rx, z, conv_state, h0, wvec, bvec, conv_w, conv_b, wa, ba, wx, bx, lam)


HEAD_SPLIT = XA_HEAD_DIM // 128
HEAD_ROWS = HEAD_SPLIT * XA_HEADS


def _head_rows(x):
    lead = x.shape[:-2]
    x = x.reshape(*lead, XA_HEADS, HEAD_SPLIT, 128)
    return jnp.swapaxes(x, -2, -3).reshape(*lead, HEAD_ROWS, 128)


def _xattn_sample_kernel(q_ref, k_ref, v_ref, o_ref, *, tb):
    scale = XA_HEAD_DIM ** -0.5
    n_rows = N_MEM * HEAD_ROWS
    lane = lax.broadcasted_iota(jnp.int32, (HEAD_ROWS, n_rows), 1)
    sub = lax.broadcasted_iota(jnp.int32, (HEAD_ROWS, n_rows), 0)
    own = (lane % HEAD_ROWS) == sub
    low_piece = (lax.broadcasted_iota(jnp.int32, (1, n_rows), 1) % HEAD_ROWS) < XA_HEADS

    def lane_group_reduce(x, op):
        shift = HEAD_ROWS
        while shift < 128:
            x = op(x, pltpu.roll(x, shift, 1))
            shift *= 2
        return x

    def per_kind(x, op):
        acc = x[:, 0:128]
        for c in range(1, n_rows // 128):
            acc = op(acc, x[:, c * 128:(c + 1) * 128])
        acc = lane_group_reduce(acc, op)
        return jnp.concatenate([acc] * (n_rows // 128), axis=1)

    parts = []
    for b in range(tb):
        qb = q_ref[b].astype(BF16)
        kb = k_ref[b].reshape(n_rows, 128).astype(BF16)
        c = lax.dot_general(qb, kb, (((1,), (1,)), ((), ())), preferred_element_type=F32)
        parts.append(jnp.sum(jnp.where(own, c, 0.0), axis=0, keepdims=True))
    part = jnp.concatenate(parts, axis=0)
    other = jnp.where(low_piece, pltpu.roll(part, n_rows - XA_HEADS, 1), pltpu.roll(part, XA_HEADS, 1))
    s = (part + other) * scale
    e = jnp.exp(s - per_kind(s, jnp.maximum))
    p = e * (1.0 / per_kind(e, jnp.add))
    for b in range(tb):
        vb = v_ref[b].reshape(n_rows, 128).astype(BF16)
        pm = jnp.where(own, jnp.broadcast_to(p[b:b + 1, :], (HEAD_ROWS, n_rows)), 0.0).astype(BF16)
        o_ref[b] = _dot(pm, vb).astype(BF16)


def _xattn_sample(q, k_cache, v_cache, *, tb=8):
    n = q.shape[0]
    kv_spec = pl.BlockSpec((tb, N_MEM, HEAD_ROWS, 128), lambda i: (i, 0, 0, 0))
    q_spec = pl.BlockSpec((tb, HEAD_ROWS, 128), lambda i: (i, 0, 0))
    return pl.pallas_call(
        functools.partial(_xattn_sample_kernel, tb=tb),
        grid=(n // tb,),
        in_specs=[q_spec, kv_spec, kv_spec],
        out_specs=q_spec,
        out_shape=jax.ShapeDtypeStruct((n, HEAD_ROWS, 128), BF16),
        **_call_params("xattn_sample", 1),
    )(q, k_cache, v_cache)


N_BRANCH = 3


def _merge_kernel(y_ref, wb_ref, g0_ref, g1_ref, g2_ref, o_ref):
    acc = None
    for k, gate_ref in enumerate((g0_ref, g1_ref, g2_ref)):
        term = gate_ref[...].astype(F32) * _dot(y_ref[k], wb_ref[k])
        acc = term if acc is None else acc + term
    o_ref[...] = acc.astype(BF16)


def _merge(y, w_branch, z, *, tm):
    m = y.shape[1]
    gate_spec = lambda k: pl.BlockSpec((tm, BR_W), lambda n, i, k=k: (i, Z_GATE0 + 2 * k + n))
    return pl.pallas_call(
        _merge_kernel,
        grid=(D_MODEL // BR_W, m // tm),
        in_specs=[
            pl.BlockSpec((N_BRANCH, tm, BR_W), lambda n, i: (0, i, 0)),
            pl.BlockSpec((N_BRANCH, BR_W, BR_W), lambda n, i: (0, 0, n)),
            gate_spec(0), gate_spec(1), gate_spec(2),
        ],
        out_specs=pl.BlockSpec((tm, BR_W), lambda n, i: (i, n)),
        out_shape=jax.ShapeDtypeStruct((m, D_MODEL), BF16),
        **_call_params("merge", 2),
    )(y, w_branch, z, z, z)


def _outproj_kernel(m_ref, wo_ref, x_ref, g_ref, b_ref, o_ref, *, row_splits):
    rows_per = m_ref.shape[0] // row_splits
    for r in range(row_splits):
        rows = slice(r * rows_per, (r + 1) * rows_per)
        y = ALPHA * x_ref[rows, :] + _dot(m_ref[rows, :], wo_ref[...])
        o_ref[rows, :] = _layer_norm(y, g_ref[...], b_ref[...])


def _outproj(merged, w_out, x, ln_g, ln_b, *, tm):
    m = x.shape[0]
    return pl.pallas_call(
        functools.partial(_outproj_kernel, row_splits=max(1, tm // OUTPROJ_ROW_GROUP)),
        grid=(m // tm,),
        in_specs=[
            pl.BlockSpec((tm, D_MODEL), lambda i: (i, 0)),
            pl.BlockSpec((D_MODEL, D_MODEL), lambda i: (0, 0)),
            pl.BlockSpec((tm, D_MODEL), lambda i: (i, 0)),
            pl.BlockSpec((1, D_MODEL), lambda i: (0, 0)),
            pl.BlockSpec((1, D_MODEL), lambda i: (0, 0)),
        ],
        out_specs=pl.BlockSpec((tm, D_MODEL), lambda i: (i, 0)),
        out_shape=jax.ShapeDtypeStruct((m, D_MODEL), F32),
        **_call_params("out_proj", 1),
    )(merged, w_out, x, ln_g, ln_b)


def kernel(x_prompt, x_sample, mem_prompt, cache_mem_k, cache_mem_v, state_conv, state_lru_h, ffn1_w_gu, ffn1_w_down, ln1_g, ln1_b, w_in, gate_b, gmlp_ln_g, gmlp_ln_b, gmlp_w_s, gmlp_b_s, conv_w, conv_b, lru_w_a, lru_b_a, lru_w_x, lru_b_x, lru_lambda, mem_ln_g, mem_ln_b, w_mem_kv, w_branch, w_out, ln2_g, ln2_b, ffn2_w_gu, ffn2_w_down, ln3_g, ln3_b):
    bsz, seq, _ = x_prompt.shape
    n_s = x_sample.shape[0]
    l = 0

    wa, wx = lru_w_a[l].astype(BF16), lru_w_x[l].astype(BF16)

    row = lambda p: p[l].reshape(1, -1)
    gb = gate_b[l].reshape(1, -1)
    bs_mat = jnp.repeat(gmlp_b_s[l].T, GROUP_W, axis=1)
    wvec = jnp.repeat(gmlp_w_s[l][:, 0, 0], GROUP_W).reshape(1, -1)
    bvec = jnp.repeat(gmlp_b_s[l][:, 0], GROUP_W).reshape(1, -1)
    lru = (wa, row(lru_b_a), wx, row(lru_b_x), row(lru_lambda))

    xp = x_prompt.reshape(bsz * seq, D_MODEL)
    xs = x_sample.reshape(n_s, D_MODEL)

    x1s, w1g, w1u, w1d = _ffn(xs, ffn1_w_gu[l], ffn1_w_gu[l], ffn1_w_down[l], row(ln1_g), row(ln1_b),
                              tm=n_s, emit_bf16=True)
    k_p, v_p, k16, v16 = _memkv(mem_prompt.reshape(bsz * N_MEM, D_MODEL), w_mem_kv[l], row(mem_ln_g),
                                row(mem_ln_b))
    x1p, win, w2d = _ffn(xp, w1g, w1u, w1d, row(ln1_g), row(ln1_b), tm=1024, side=(w_in[l], ffn2_w_down[l]))

    zs, rxs = _inproj(x1s, win, gb, row(gmlp_ln_g), row(gmlp_ln_b), tm=n_s, z_dtype=F32)
    y01, conv_s, h_s = _mixer_sample(
        zs, rxs, jnp.swapaxes(state_conv[l], 0, 1), state_lru_h[l], wvec, bvec, conv_w[l], row(conv_b), *lru)
    q_s = zs[:, Z_Q * BR_W:(Z_Q + 1) * BR_W].reshape(n_s, XA_HEADS, XA_HEAD_DIM)
    yxa = _xattn_sample(_head_rows(q_s), _head_rows(cache_mem_k[l]), _head_rows(cache_mem_v[l]))
    yxa = jnp.swapaxes(yxa.reshape(n_s, HEAD_SPLIT, XA_HEADS, 128), 1, 2)
    ys = jnp.concatenate([y01, yxa.reshape(1, n_s, BR_W)], axis=0)

    zp, rxp, w2gu = _inproj(x1p, win, gb, row(gmlp_ln_g), row(gmlp_ln_b), tm=1024, z_dtype=BF16,
                            side=(ffn2_w_gu[l],))
    yp, conv_p, hlast_p, wbr, wout = _mixer_prompt(
        zp, rxp, k16.reshape(bsz, N_MEM, BR_W), v16.reshape(bsz, N_MEM, BR_W), gmlp_w_s[l], bs_mat,
        conv_w[l], row(conv_b), *lru, bsz=bsz, seq=seq,
        side=(w_branch[l].reshape(N_BRANCH * BR_W, D_MODEL), w_out[l]))
    wbr = wbr.reshape(N_BRANCH, BR_W, D_MODEL)

    ms = _merge(ys, wbr, zs, tm=n_s)
    x2s = _outproj(ms, wout, x1s, row(ln2_g), row(ln2_b), tm=n_s)
    x3s = _ffn(x2s, w2gu, w2gu, w2d, row(ln3_g), row(ln3_b), tm=n_s)
    mp = _merge(yp, wbr, zp, tm=1024)
    x2p = _outproj(mp, wout, x1p, row(ln2_g), row(ln2_b), tm=512)
    x3p = _ffn(x2p, w2gu, w2gu, w2d, row(ln3_g), row(ln3_b), tm=1024)

    kv_shape = (1, bsz, N_MEM, XA_HEADS, XA_HEAD_DIM)
    return (x3p.reshape(bsz, seq, D_MODEL),
            x3s.reshape(n_s, 1, D_MODEL),
            k_p.reshape(kv_shape),
            v_p.reshape(kv_shape),
            conv_p[None],
            hlast_p.reshape(1, bsz, BR_W),
            jnp.swapaxes(conv_s, 0, 1)[None],
            h_s[None],
            zs[:, Z_V * BR_W:(Z_V + 1) * BR_W].reshape(1, n_s, 1, BR_W))
```
